```python
import math
import jax, jax.numpy as jnp
from jax import lax
import numpy as np

D_MODEL = 1024
BATCH = 8
SEQ = 4096
DEPTH = 1

N_META = 16
POOL_GROUPS = 4
POOL_GROUP_DIM = 128
POOL_WIDTH = POOL_GROUPS * POOL_GROUP_DIM
POOL_WINDOWS = (2, 4, 8, 16)
DN_HEADS = 8
DN_HEAD_DIM = 128
DN_WIDTH = DN_HEADS * DN_HEAD_DIM
CONV_WIDTH = 4
CHUNK = 64
PEER_HEADS = 8
PEER_QUERY_DIM = 256
N_KEYS = 128
N_EXPERTS = N_KEYS * N_KEYS
PEER_TOPK = 16
PEER_BLOCK = 128
N_IN = POOL_WIDTH + 4 * DN_WIDTH + 2 * DN_HEADS + 2 * D_MODEL
LN_EPS = 1e-5
RMS_EPS = 1e-6
L2_EPS = 1e-6
ALPHA = (2 * DEPTH) ** 0.25
BETA_INIT = (8 * DEPTH) ** -0.25

kernel_name = 'hybrid_pool_deltanet_peer_block'


def _layer_norm(x, g, b):
    xf = x.astype(jnp.float32)
    mu = jnp.mean(xf, axis=-1, keepdims=True)
    var = jnp.mean(jnp.square(xf - mu), axis=-1, keepdims=True)
    y = (xf - mu) * lax.rsqrt(var + LN_EPS)
    return (y * g.astype(jnp.float32) + b.astype(jnp.float32)).astype(x.dtype)


def _l2norm(t):
    return t * lax.rsqrt(jnp.sum(t * t, axis=-1, keepdims=True) + L2_EPS)


def _causal_multiscale_pool(xp, pool_w):
    bsz, length, _ = xp.shape
    xg = xp.reshape(bsz, length, POOL_GROUPS, POOL_GROUP_DIM)
    csum = jnp.cumsum(xg, axis=1)
    count = jnp.arange(1, length + 1, dtype=xp.dtype)
    outs = []
    for gi, w in enumerate(POOL_WINDOWS):
        cg = csum[:, :, gi]
        lagged = jnp.pad(cg, ((0, 0), (w, 0), (0, 0)))[:, :length]
        mean = (cg - lagged) / jnp.minimum(count, w)[None, :, None]
        outs.append(mean - xg[:, :, gi])
    pooled = jnp.stack(outs, axis=2)
    mixed = jnp.einsum('blgc,gcd->blgd', pooled, pool_w)
    return mixed.reshape(bsz, length, POOL_WIDTH)


def _causal_depthwise_conv(x, w):
    ch = x.shape[-1]
    return lax.conv_general_dilated(x, w[:, None, :], window_strides=(1,), padding=[(CONV_WIDTH - 1, 0)], dimension_numbers=('NWC', 'WIO', 'NWC'), feature_group_count=ch)


def _gated_delta_rule_chunked(q, k, v, beta, g):
    bsz, total, heads, dk = q.shape
    dv = v.shape[-1]
    n_chunks = total // CHUNK

    def to_chunks(t):
        t = t.reshape((bsz, n_chunks, CHUNK, heads) + t.shape[3:])
        return jnp.moveaxis(t, (1, 3), (0, 2))

    qc, kc, vc, bc, gch = [to_chunks(t) for t in (q, k, v, beta, g)]
    gcum = jnp.cumsum(gch, axis=-1)
    causal = jnp.tril(jnp.ones((CHUNK, CHUNK), dtype=bool))
    strict = jnp.tril(jnp.ones((CHUNK, CHUNK), dtype=bool), -1)
    diff = gcum[..., :, None] - gcum[..., None, :]
    decay = jnp.where(causal, jnp.exp(jnp.where(causal, diff, 0.0)), 0.0)
    kb = kc * bc[..., None]
    a_mat = jnp.where(strict, jnp.einsum('nbhid,nbhjd->nbhij', kb, kc) * decay, 0.0)
    eye = jnp.eye(CHUNK, dtype=q.dtype)
    rhs = jnp.concatenate([vc * bc[..., None], kb * jnp.exp(gcum)[..., None]], axis=-1)
    sol = lax.linalg.triangular_solve(eye + a_mat, rhs, left_side=True, lower=True, unit_diagonal=True)
    u_c, w_c = sol[..., :dv], sol[..., dv:]
    attn = jnp.einsum('nbhid,nbhjd->nbhij', qc, kc) * decay
    q_dec = qc * jnp.exp(gcum)[..., None]
    k_dec = kc * jnp.exp(gcum[..., -1:] - gcum)[..., None]
    chunk_decay = jnp.exp(gcum[..., -1])

    def step(state, inp):
        qd, kd, uu, ww, at, cd = inp
        v_new = uu - jnp.einsum('bhck,bhkv->bhcv', ww, state)
        out = jnp.einsum('bhck,bhkv->bhcv', qd, state) + jnp.einsum('bhij,bhjv->bhiv', at, v_new)
        state = state * cd[..., None, None] + jnp.einsum('bhck,bhcv->bhkv', kd, v_new)
        return state, out

    s0 = jnp.zeros((bsz, heads, dk, dv), q.dtype)
    _, o = lax.scan(step, s0, (q_dec, k_dec, u_c, w_c, attn, chunk_decay))
    return jnp.moveaxis(o, (0, 2), (1, 3)).reshape(bsz, total, heads, dv)


def _hybrid_mixer(h, w_in, pool_w, pool_scale, w_br_pool, conv_w, a_log, dt_bias, onorm_g, w_br_delta, w_out):
    bsz, length, _ = h.shape
    f32 = jnp.float32
    proj = jnp.einsum('bld,dn->bln', h, w_in)
    sizes = [POOL_WIDTH, 3 * DN_WIDTH, DN_WIDTH, DN_HEADS, DN_HEADS, D_MODEL, D_MODEL]
    cuts = [int(c) for c in np.cumsum(sizes)[:-1]]
    xp, qkv, z, b_logit, a_logit, gate_pool, gate_delta = jnp.split(proj, cuts, axis=-1)
    y_pool = _causal_multiscale_pool(xp.astype(f32), pool_w.astype(f32)) * pool_scale.astype(f32)
    y_a = jnp.einsum('blc,cd->bld', y_pool.astype(h.dtype), w_br_pool)
    qkv = jax.nn.silu(_causal_depthwise_conv(qkv.astype(f32), conv_w.astype(f32)))
    q, k, v = jnp.split(qkv, 3, axis=-1)
    to_heads = lambda t: t.reshape(bsz, length, DN_HEADS, DN_HEAD_DIM)
    q = _l2norm(to_heads(q)) * (DN_HEAD_DIM ** -0.5)
    k = _l2norm(to_heads(k))
    v = to_heads(v)
    beta = jax.nn.sigmoid(b_logit.astype(f32))
    g = -jnp.exp(a_log.astype(f32)) * jax.nn.softplus(a_logit.astype(f32) + dt_bias.astype(f32))
    pad = CHUNK - N_META
    padt = lambda t: jnp.pad(t, ((0, 0), (pad, 0)) + ((0, 0),) * (t.ndim - 2))
    o = _gated_delta_rule_chunked(padt(q), padt(k), padt(v), padt(beta), padt(g))[:, pad:]
    o = o * lax.rsqrt(jnp.mean(jnp.square(o), axis=-1, keepdims=True) + RMS_EPS) * onorm_g.astype(f32)
    o = o * jax.nn.silu(to_heads(z).astype(f32))
    y_b = jnp.einsum('blc,cd->bld', o.reshape(bsz, length, DN_WIDTH).astype(h.dtype), w_br_delta)
    merged = jax.nn.sigmoid(gate_pool) * y_a + jax.nn.sigmoid(gate_delta) * y_b
    return jnp.einsum('bld,de->ble', merged, w_out)


def _peer(h, wq, sub_k1, sub_k2, u_tab, v_tab):
    bsz, length, d = h.shape
    f32 = jnp.float32
    n_tok = bsz * length
    xt = h.reshape(n_tok, d)
    q = jnp.einsum('td,dq->tq', xt, wq).astype(f32).reshape(n_tok, PEER_HEADS, PEER_QUERY_DIM)
    half = PEER_QUERY_DIM // 2
    s1 = jnp.einsum('thc,hnc->thn', q[..., :half], sub_k1.astype(f32))
    s2 = jnp.einsum('thc,hnc->thn', q[..., half:], sub_k2.astype(f32))
    v1, i1 = lax.top_k(s1, PEER_TOPK)
    v2, i2 = lax.top_k(s2, PEER_TOPK)
    cand = (v1[..., :, None] + v2[..., None, :]).reshape(n_tok, PEER_HEADS, PEER_TOPK * PEER_TOPK)
    score, ci = lax.top_k(cand, PEER_TOPK)
    ia = jnp.take_along_axis(i1, ci // PEER_TOPK, axis=-1)
    ib = jnp.take_along_axis(i2, ci % PEER_TOPK, axis=-1)
    idx = ia * N_KEYS + ib
    gate = jax.nn.softmax(score, axis=-1)
    n_blk = -(-n_tok // PEER_BLOCK)
    extra = n_blk * PEER_BLOCK - n_tok
    xb = jnp.pad(xt, ((0, extra), (0, 0))).reshape(n_blk, PEER_BLOCK, d)
    idb = jnp.pad(idx, ((0, extra), (0, 0), (0, 0))).reshape(n_blk, PEER_BLOCK, PEER_HEADS, PEER_TOPK)
    gb = jnp.pad(gate, ((0, extra), (0, 0), (0, 0))).reshape(n_blk, PEER_BLOCK, PEER_HEADS, PEER_TOPK)

    def block(args):
        xx, ii, gg = args
        u = jnp.take(u_tab, ii, axis=0)
        hid = jnp.einsum('thkd,td->thk', u, xx).astype(f32)
        act = jax.nn.gelu(hid, approximate=False) * gg
        vv = jnp.take(v_tab, ii, axis=0)
        return jnp.einsum('thk,thkd->td', act.astype(vv.dtype), vv)

    out = lax.map(block, (xb, idb, gb)).reshape(n_blk * PEER_BLOCK, d)[:n_tok]
    return out.reshape(bsz, length, d)


def setup_inputs(seed: int = 0) -> dict:
    key = jax.random.key(seed)
    ks = jax.random.split(key, 24)
    f32 = jnp.float32
    nrm = lambda k, shape, scale: jax.random.normal(k, shape, f32) * scale
    x = nrm(ks[0], (BATCH, SEQ, D_MODEL), 1.0)
    meta = nrm(ks[1], (N_META, D_MODEL), 1.0)
    ln0_g = 1.0 + nrm(ks[2], (D_MODEL,), 0.02)
    ln0_b = nrm(ks[3], (D_MODEL,), 0.02)
    w_in = nrm(ks[4], (DEPTH, D_MODEL, N_IN), D_MODEL ** -0.5)
    pool_w = nrm(ks[5], (DEPTH, POOL_GROUPS, POOL_GROUP_DIM, POOL_GROUP_DIM), POOL_GROUP_DIM ** -0.5)
    pool_scale = 1.0 + nrm(ks[6], (DEPTH, POOL_WIDTH), 0.02)
    w_br_pool = nrm(ks[7], (DEPTH, POOL_WIDTH, D_MODEL), POOL_WIDTH ** -0.5)
    conv_w = nrm(ks[8], (DEPTH, CONV_WIDTH, 3 * DN_WIDTH), CONV_WIDTH ** -0.5)
    a_log = jnp.log(jax.random.uniform(ks[9], (DEPTH, DN_HEADS), f32, 1.0, 16.0))
    dt = jnp.exp(jax.random.uniform(ks[10], (DEPTH, DN_HEADS), f32, math.log(1e-3), math.log(1e-1)))
    dt_bias = dt + jnp.log(-jnp.expm1(-dt))
    onorm_g = 1.0 + nrm(ks[11], (DEPTH, DN_HEAD_DIM), 0.02)
    w_br_delta = nrm(ks[12], (DEPTH, DN_WIDTH, D_MODEL), DN_WIDTH ** -0.5)
    w_out = nrm(ks[13], (DEPTH, D_MODEL, D_MODEL), (D_MODEL ** -0.5) * BETA_INIT)
    ln1_g = 1.0 + nrm(ks[14], (DEPTH, D_MODEL), 0.02)
    ln1_b = nrm(ks[15], (DEPTH, D_MODEL), 0.02)
    peer_wq = nrm(ks[16], (DEPTH, D_MODEL, PEER_HEADS * PEER_QUERY_DIM), D_MODEL ** -0.5)
    peer_k1 = nrm(ks[17], (DEPTH, PEER_HEADS, N_KEYS, PEER_QUERY_DIM // 2), (PEER_QUERY_DIM // 2) ** -0.5)
    peer_k2 = nrm(ks[18], (DEPTH, PEER_HEADS, N_KEYS, PEER_QUERY_DIM // 2), (PEER_QUERY_DIM // 2) ** -0.5)
    peer_u = nrm(ks[19], (DEPTH, N_EXPERTS, D_MODEL), D_MODEL ** -0.5)
    peer_v = nrm(ks[20], (DEPTH, N_EXPERTS, D_MODEL), BETA_INIT)
    ln2_g = 1.0 + nrm(ks[21], (DEPTH, D_MODEL), 0.02)
    ln2_b = nrm(ks[22], (DEPTH, D_MODEL), 0.02)
    return {'x': x, 'meta': meta, 'ln0_g': ln0_g, 'ln0_b': ln0_b, 'w_in': w_in, 'pool_w': pool_w, 'pool_scale': pool_scale, 'w_br_pool': w_br_pool, 'conv_w': conv_w, 'a_log': a_log, 'dt_bias': dt_bias, 'onorm_g': onorm_g, 'w_br_delta': w_br_delta, 'w_out': w_out, 'ln1_g': ln1_g, 'ln1_b': ln1_b, 'peer_wq': peer_wq, 'peer_k1': peer_k1, 'peer_k2': peer_k2, 'peer_u': peer_u, 'peer_v': peer_v, 'ln2_g': ln2_g, 'ln2_b': ln2_b}


def reference(x, meta, ln0_g, ln0_b, w_in, pool_w, pool_scale, w_br_pool, conv_w, a_log, dt_bias, onorm_g, w_br_delta, w_out, ln1_g, ln1_b, peer_wq, peer_k1, peer_k2, peer_u, peer_v, ln2_g, ln2_b):
    bsz = x.shape[0]
    meta_b = jnp.broadcast_to(meta[None].astype(x.dtype), (bsz, N_META, D_MODEL))
    h = jnp.concatenate([meta_b, x], axis=1)
    h = _layer_norm(h, ln0_g, ln0_b)
    for i in range(DEPTH):
        mix = _hybrid_mixer(h, w_in[i], pool_w[i], pool_scale[i], w_br_pool[i], conv_w[i], a_log[i], dt_bias[i], onorm_g[i], w_br_delta[i], w_out[i])
        h = _layer_norm(ALPHA * h + mix, ln1_g[i], ln1_b[i])
        if i == DEPTH - 1:
            h = h[:, N_META:]
        h = _layer_norm(ALPHA * h + _peer(h, peer_wq[i], peer_k1[i], peer_k2[i], peer_u[i], peer_v[i]), ln2_g[i], ln2_b[i])
    return h
```

```python
import functools
import math

import jax
import jax.numpy as jnp
from jax import lax
from jax.experimental import pallas as pl
from jax.experimental.pallas import tpu as pltpu

F32 = jnp.float32
BF16 = jnp.bfloat16
I32 = jnp.int32

D_MODEL = 1024
N_META = 16
CHUNK = 64
PAD = CHUNK - N_META
POOL_GROUPS = 4
POOL_GROUP_DIM = 128
POOL_WIDTH = POOL_GROUPS * POOL_GROUP_DIM
POOL_WINDOWS = (2, 4, 8, 16)
POOL_HALO = 16
DN_HEADS = 8
DN_HEAD_DIM = 128
DN_WIDTH = DN_HEADS * DN_HEAD_DIM
CONV_WIDTH = 4
PEER_HEADS = 8
PEER_HALF = 128
N_KEYS = 128
PEER_TOPK = 16
N_SLOTS = PEER_HEADS * PEER_TOPK
LN_EPS = 1e-5
RMS_EPS = 1e-6
L2_EPS = 1e-6
ALPHA = 2.0 ** 0.25

COL_QKV = 0
COL_Z = 3 * DN_WIDTH
COL_GP = COL_Z + DN_WIDTH
COL_GD = COL_GP + D_MODEL
COL_POOL = COL_GD + D_MODEL
N_PROJ = COL_POOL + POOL_WIDTH
BA_WIDTH = 256

SUBLANES = 8
LANES = 128
VMEM_LIMIT = 56 * 1024 * 1024

_CAND = [(a, b) for a in range(PEER_TOPK) for b in range(PEER_TOPK) if (a + 1) * (b + 1) <= PEER_TOPK]


def _params(sem):
    return pltpu.CompilerParams(dimension_semantics=sem, vmem_limit_bytes=VMEM_LIMIT)


def _layer_norm(x, g, b):
    mu = jnp.mean(x, axis=-1, keepdims=True)
    xc = x - mu
    var = jnp.mean(xc * xc, axis=-1, keepdims=True)
    return xc * lax.rsqrt(var + LN_EPS) * g + b


def _zero_pad_rows(y, row, seq):
    return jnp.where(row >= seq, jnp.where(row < seq + PAD, 0.0, y), y)


def _sigmoid(x):
    return 1.0 / (1.0 + jnp.exp(-x))


def _dot(a, b):
    return jnp.dot(a, b, preferred_element_type=F32)


def _dot_nt(a, b):
    return lax.dot_general(a, b, (((1,), (1,)), ((), ())), preferred_element_type=F32)


def _inproj_kernel(x_ref, g_ref, b_ref, w_ref, wba_ref, proj_ref, ba_ref, xn_ref, *, tiles_per_batch, seq):
    i = pl.program_id(0)
    j = pl.program_id(1)
    tm = x_ref.shape[0]

    @pl.when(j == 0)
    def _():
        y = _layer_norm(x_ref[...], g_ref[...], b_ref[...])
        row = lax.broadcasted_iota(I32, (tm, 1), 0) + lax.rem(i, tiles_per_batch) * tm
        xn = _zero_pad_rows(y, row, seq).astype(BF16)
        xn_ref[...] = xn
        ba_ref[...] = _dot(xn, wba_ref[...])

    proj_ref[...] = _dot(xn_ref[...], w_ref[...])


def _inproj(h0, ln_g, ln_b, w_main, w_ba, *, seq, t_len):
    rows = h0.shape[0]
    tiles_per_batch = 4
    tm = t_len // tiles_per_batch
    tn = 512
    kern = functools.partial(_inproj_kernel, tiles_per_batch=tiles_per_batch, seq=seq)
    return pl.pallas_call(
        kern,
        grid=(rows // tm, N_PROJ // tn),
        in_specs=[
            pl.BlockSpec((tm, D_MODEL), lambda i, j: (i, 0)),
            pl.BlockSpec((1, D_MODEL), lambda i, j: (0, 0)),
            pl.BlockSpec((1, D_MODEL), lambda i, j: (0, 0)),
            pl.BlockSpec((D_MODEL, tn), lambda i, j: (0, j)),
            pl.BlockSpec((D_MODEL, BA_WIDTH), lambda i, j: (0, 0)),
        ],
        out_specs=[
            pl.BlockSpec((tm, tn), lambda i, j: (i, j)),
            pl.BlockSpec((tm, BA_WIDTH), lambda i, j: (i, 0)),
        ],
        out_shape=[
            jax.ShapeDtypeStruct((rows, N_PROJ), F32),
            jax.ShapeDtypeStruct((rows, BA_WIDTH), F32),
        ],
        scratch_shapes=[pltpu.VMEM((tm, D_MODEL), BF16)],
        compiler_params=_params(("arbitrary", "arbitrary")),
        name="inproj",
    )(h0, ln_g, ln_b, w_main, w_ba)


def _pool_kernel(xp_ref, halo_ref, pw_ref, ps_ref, wbp_ref, ya_ref, *, seq):
    i = pl.program_id(1)
    tm = xp_ref.shape[0]
    v = jnp.concatenate([halo_ref[...], xp_ref[...]], axis=0)
    row = lax.broadcasted_iota(I32, (tm, POOL_GROUP_DIM), 0) + i * tm
    meta_pos = row - (seq + PAD)
    outs = []
    for gi, w in enumerate(POOL_WINDOWS):
        vg = v[:, gi * POOL_GROUP_DIM:(gi + 1) * POOL_GROUP_DIM]
        s = vg
        shift = 1
        while shift < w:
            s = s + pltpu.roll(s, shift, axis=0)
            shift *= 2
        s = s[POOL_HALO:]
        xg = vg[POOL_HALO:]
        count = jnp.where(meta_pos >= 0, jnp.minimum(meta_pos + 1, w), w).astype(F32)
        pooled = s / count - xg
        mixed = _dot(pooled.astype(BF16), pw_ref[gi])
        outs.append(mixed)
    y_pool = jnp.concatenate(outs, axis=1) * ps_ref[...]
    ya_ref[...] = _dot(y_pool.astype(BF16), wbp_ref[...])


def _pool_branch(proj, pool_w, pool_scale, w_br_pool, *, batch, seq, t_len):
    rows = proj.shape[0]
    tiles_per_batch = 4
    tm = t_len // tiles_per_batch
    halo_blocks_per_batch = t_len // POOL_HALO
    halo_blocks_per_tile = tm // POOL_HALO
    pool_col = COL_POOL // POOL_WIDTH

    def halo_map(b, i):
        prev = lax.rem(i * halo_blocks_per_tile + halo_blocks_per_batch - 1, halo_blocks_per_batch)
        return (b * halo_blocks_per_batch + prev, pool_col)

    return pl.pallas_call(
        functools.partial(_pool_kernel, seq=seq),
        grid=(batch, tiles_per_batch),
        in_specs=[
            pl.BlockSpec((tm, POOL_WIDTH), lambda b, i: (b * tiles_per_batch + i, pool_col)),
            pl.BlockSpec((POOL_HALO, POOL_WIDTH), halo_map),
            pl.BlockSpec((POOL_GROUPS, POOL_GROUP_DIM, POOL_GROUP_DIM), lambda b, i: (0, 0, 0)),
            pl.BlockSpec((1, POOL_WIDTH), lambda b, i: (0, 0)),
            pl.BlockSpec((POOL_WIDTH, D_MODEL), lambda b, i: (0, 0)),
        ],
        out_specs=pl.BlockSpec((tm, D_MODEL), lambda b, i: (b * tiles_per_batch + i, 0)),
        out_shape=jax.ShapeDtypeStruct((rows, D_MODEL), F32),
        compiler_params=_params(("arbitrary", "arbitrary")),
        name="pool_branch",
    )(proj, proj, pool_w, pool_scale, w_br_pool)


def _conv_kernel(x_ref, w_ref, o_ref, *, seq):
    s = pl.program_id(1)
    x = x_ref[...]
    t_len = x.shape[0]
    w = w_ref[...]
    y = x * w[CONV_WIDTH - 1:CONV_WIDTH, :]
    for lag in range(1, CONV_WIDTH):
        y = y + pltpu.roll(x, lag, axis=0) * w[CONV_WIDTH - 1 - lag:CONV_WIDTH - lag, :]
    y = y * _sigmoid(y)
    ss = jnp.sum(y * y, axis=-1, keepdims=True)
    q_scale = jnp.where(s < DN_HEADS, DN_HEAD_DIM ** -0.5, 1.0).astype(F32)
    fac = jnp.where(s < 2 * DN_HEADS, lax.rsqrt(ss + L2_EPS) * q_scale, 1.0)
    row = lax.broadcasted_iota(I32, (t_len, 1), 0)
    o_ref[...] = _zero_pad_rows(y * fac, row, seq)


def _conv_qkv(proj, conv_w, *, batch, seq, t_len):
    rows = proj.shape[0]
    n_slabs = 3 * DN_HEADS
    return pl.pallas_call(
        functools.partial(_conv_kernel, seq=seq),
        grid=(batch, n_slabs),
        in_specs=[
            pl.BlockSpec((t_len, DN_HEAD_DIM), lambda b, s: (b, COL_QKV // DN_HEAD_DIM + s)),
            pl.BlockSpec((CONV_WIDTH, DN_HEAD_DIM), lambda b, s: (0, s)),
        ],
        out_specs=pl.BlockSpec((t_len, DN_HEAD_DIM), lambda b, s: (b, s)),
        out_shape=jax.ShapeDtypeStruct((rows, 3 * DN_WIDTH), F32),
        compiler_params=_params(("arbitrary", "arbitrary")),
        name="conv_qkv",
    )(proj, conv_w)


def _inv_unit_lower(a, masks, eye):
    m8, m16, m32, m64 = masks
    n = a * m8
    n2 = _dot(n, n)
    n4 = _dot(n2, n2)
    t = eye - n
    t = t + _dot(t, n2)
    t = t + _dot(t, n4)
    for m in (m16, m32, m64):
        t = t - _dot(t, _dot(a * m, t))
    return t


def _delta_kernel(q_ref, k_ref, v_ref, ba_ref, z_ref, nega_ref, dtb_ref, og_ref, o_ref, state_ref):
    c = pl.program_id(1)

    @pl.when(c == 0)
    def _():
        state_ref[...] = jnp.zeros_like(state_ref)

    r = lax.broadcasted_iota(I32, (CHUNK, CHUNK), 0)
    cc = lax.broadcasted_iota(I32, (CHUNK, CHUNK), 1)
    causal = r >= cc
    strict = r > cc
    eye = jnp.where(r == cc, 1.0, 0.0).astype(F32)
    tril = jnp.where(causal, 1.0, 0.0).astype(F32)

    def blk(x, s):
        return lax.shift_right_logical(x, s)

    lower = jnp.where(strict, 1.0, 0.0).astype(F32)
    m8 = jnp.where(blk(r, 3) == blk(cc, 3), lower, 0.0)
    m16 = jnp.where((blk(r, 4) == blk(cc, 4)) & (blk(r, 3) != blk(cc, 3)), lower, 0.0)
    m32 = jnp.where((blk(r, 5) == blk(cc, 5)) & (blk(r, 4) != blk(cc, 4)), lower, 0.0)
    m64 = jnp.where(blk(r, 5) != blk(cc, 5), lower, 0.0)
    masks = (m8, m16, m32, m64)

    ba = ba_ref[...]
    beta_all = _sigmoid(ba[:, :LANES])
    x = ba[:, LANES:] + dtb_ref[...]
    softplus = jnp.maximum(x, 0.0) + jnp.log(1.0 + jnp.exp(-jnp.abs(x)))
    g_all = nega_ref[...] * softplus
    gcum = lax.dot_general(tril, g_all, (((1,), (0,)), ((), ())), precision=lax.Precision.HIGHEST,
                           preferred_element_type=F32)
    gcum_t = gcum.T

    for h in range(DN_HEADS):
        sl = slice(h * DN_HEAD_DIM, (h + 1) * DN_HEAD_DIM)
        qh = q_ref[:, sl]
        kh = k_ref[:, sl]
        vh = v_ref[:, sl]
        beta = beta_all[:, h:h + 1]
        gc = gcum[:, h:h + 1]
        gr = gcum_t[h:h + 1, :]
        g_last = gcum[CHUNK - 1:CHUNK, h:h + 1]
        decay = jnp.where(causal, jnp.exp(jnp.where(causal, gc - gr, 0.0)), 0.0)
        eg = jnp.exp(gc)
        kb = kh * beta
        a_mat = jnp.where(strict, _dot_nt(kb, kh) * decay, 0.0)
        t_inv = _inv_unit_lower(a_mat, masks, eye)
        u = _dot(t_inv, vh * beta)
        w = _dot(t_inv, kb * eg)
        attn = _dot_nt(qh, kh) * decay
        q_dec = qh * eg
        k_dec = kh * jnp.exp(g_last - gc)
        s_prev = state_ref[h]
        v_new = u - _dot(w, s_prev)
        out = _dot(q_dec, s_prev) + _dot(attn, v_new)
        state_ref[h] = s_prev * jnp.exp(g_last) + _dot(k_dec.T, v_new)
        out = out * lax.rsqrt(jnp.mean(out * out, axis=-1, keepdims=True) + RMS_EPS) * og_ref[...]
        zh = z_ref[:, sl]
        o_ref[:, sl] = (out * (zh * _sigmoid(zh))).astype(o_ref.dtype)


def _delta_rule(qkvn, ba, proj, neg_a, dt_bias, onorm_g, *, batch, t_len):
    rows = qkvn.shape[0]
    n_chunks = t_len // CHUNK

    def mem_chunk(b, c):
        return b * n_chunks + lax.rem(c + n_chunks - 1, n_chunks)

    return pl.pallas_call(
        _delta_kernel,
        grid=(batch, n_chunks),
        in_specs=[
            pl.BlockSpec((CHUNK, DN_WIDTH), lambda b, c: (mem_chunk(b, c), 0)),
            pl.BlockSpec((CHUNK, DN_WIDTH), lambda b, c: (mem_chunk(b, c), 1)),
            pl.BlockSpec((CHUNK, DN_WIDTH), lambda b, c: (mem_chunk(b, c), 2)),
            pl.BlockSpec((CHUNK, BA_WIDTH), lambda b, c: (mem_chunk(b, c), 0)),
            pl.BlockSpec((CHUNK, DN_WIDTH), lambda b, c: (mem_chunk(b, c), COL_Z // DN_WIDTH)),
            pl.BlockSpec((1, LANES), lambda b, c: (0, 0)),
            pl.BlockSpec((1, LANES), lambda b, c: (0, 0)),
            pl.BlockSpec((1, DN_HEAD_DIM), lambda b, c: (0, 0)),
        ],
        out_specs=pl.BlockSpec((CHUNK, DN_WIDTH), lambda b, c: (mem_chunk(b, c), 0)),
        out_shape=jax.ShapeDtypeStruct((rows, DN_WIDTH), BF16),
        scratch_shapes=[pltpu.VMEM((DN_HEADS, DN_HEAD_DIM, DN_HEAD_DIM), F32)],
        compiler_params=_params(("arbitrary", "arbitrary")),
        name="delta_rule",
    )(qkvn, qkvn, qkvn, ba, proj, neg_a, dt_bias, onorm_g)


def _merge_kernel(o_ref, ya_ref, gp_ref, gd_ref, h0_ref, wd_ref, wo_ref, g0_ref, b0_ref, g1_ref, b1_ref, h1_ref):
    y_b = _dot(o_ref[...], wd_ref[...])
    merged = _sigmoid(gp_ref[...]) * ya_ref[...] + _sigmoid(gd_ref[...]) * y_b
    mix = _dot(merged.astype(BF16), wo_ref[...])
    h = _layer_norm(h0_ref[...], g0_ref[...], b0_ref[...])
    h1_ref[...] = _layer_norm(ALPHA * h + mix, g1_ref[...], b1_ref[...])


def _merge(o, ya, proj, h0, w_br_delta, w_out, g0, b0, g1, b1, *, batch, seq, t_len):
    bf16_rows = 2 * SUBLANES
    tiles_per_batch = next(n for n in (10, 8, 4, 2, 1) if t_len % (n * bf16_rows) == 0)
    tm = t_len // tiles_per_batch
    row = lambda i: (i, 0)
    const = lambda i: (0, 0)
    return pl.pallas_call(
        _merge_kernel,
        grid=(batch * tiles_per_batch,),
        in_specs=[
            pl.BlockSpec((tm, DN_WIDTH), row),
            pl.BlockSpec((tm, D_MODEL), row),
            pl.BlockSpec((tm, D_MODEL), lambda i: (i, COL_GP // D_MODEL)),
            pl.BlockSpec((tm, D_MODEL), lambda i: (i, COL_GD // D_MODEL)),
            pl.BlockSpec((tm, D_MODEL), row),
            pl.BlockSpec((DN_WIDTH, D_MODEL), const),
            pl.BlockSpec((D_MODEL, D_MODEL), const),
            pl.BlockSpec((1, D_MODEL), const),
            pl.BlockSpec((1, D_MODEL), const),
            pl.BlockSpec((1, D_MODEL), const),
            pl.BlockSpec((1, D_MODEL), const),
        ],
        out_specs=pl.BlockSpec((tm, D_MODEL), row),
        out_shape=jax.ShapeDtypeStruct((batch * t_len, D_MODEL), F32),
        compiler_params=_params(("arbitrary",)),
        name="merge",
    )(o, ya, proj, proj, h0, w_br_delta, w_out, g0, b0, g1, b1)


def _tile(ref, k):
    return ref[k * SUBLANES:(k + 1) * SUBLANES, :]


def _argmax_tournament(vals, ids):
    nodes = list(zip(vals, ids))
    while len(nodes) > 1:
        nxt = []
        for p in range(0, len(nodes) - 1, 2):
            (va, ia), (vb, ib) = nodes[p], nodes[p + 1]
            first = va >= vb
            nxt.append((jnp.maximum(va, vb), jnp.where(first, ia, ib)))
        if len(nodes) % 2:
            nxt.append(nodes[-1])
        nodes = nxt
    return nodes[0]


def _query_kernel(x_ref, wq_ref, k1_ref, k2_ref, idx_ref, gate_ref, s1_ref, s2_ref, m1_ref, i1_ref, m2_ref, i2_ref,
                  c_ref, e_ref, sc_ref):
    xb = x_ref[...].astype(BF16)
    q = _dot(xb, wq_ref[...])
    for h in range(PEER_HEADS):
        q1 = q[:, (2 * h) * PEER_HALF:(2 * h + 1) * PEER_HALF].astype(BF16)
        q2 = q[:, (2 * h + 1) * PEER_HALF:(2 * h + 2) * PEER_HALF].astype(BF16)
        s1_ref[pl.ds(h, N_KEYS, stride=SUBLANES), :] = _dot_nt(k1_ref[h], q1)
        s2_ref[pl.ds(h, N_KEYS, stride=SUBLANES), :] = _dot_nt(k2_ref[h], q2)

    neg_inf = jnp.float32(-jnp.inf)

    def sub_round(r, carry):
        for s_ref, m_ref, i_ref in ((s1_ref, m1_ref, i1_ref), (s2_ref, m2_ref, i2_ref)):
            vals = [_tile(s_ref, k) for k in range(N_KEYS)]
            best, arg = _argmax_tournament(vals, list(range(N_KEYS)))
            m_ref[r] = best
            i_ref[r] = arg
            for k in range(N_KEYS):
                s_ref[k * SUBLANES:(k + 1) * SUBLANES, :] = jnp.where(arg == k, neg_inf, vals[k])
        return carry

    lax.fori_loop(0, PEER_TOPK, sub_round, 0)

    for j, (a, b) in enumerate(_CAND):
        c_ref[j * SUBLANES:(j + 1) * SUBLANES, :] = m1_ref[a] + m2_ref[b]
        e_ref[j * SUBLANES:(j + 1) * SUBLANES, :] = i1_ref[a] * N_KEYS + i2_ref[b]

    def pair_round(r, carry):
        vals = [_tile(c_ref, j) for j in range(len(_CAND))]
        ids = [_tile(e_ref, j) for j in range(len(_CAND))]
        best, arg = _argmax_tournament(vals, ids)
        sc_ref[r] = best
        idx_ref[r] = arg
        for j in range(len(_CAND)):
            c_ref[j * SUBLANES:(j + 1) * SUBLANES, :] = jnp.where(ids[j] == arg, neg_inf, vals[j])
        return carry

    lax.fori_loop(0, PEER_TOPK, pair_round, 0)

    sc = sc_ref[...]
    ex = jnp.exp(sc - sc[0:1])
    gate_ref[...] = ex / jnp.sum(ex, axis=0, keepdims=True)


def _token_tile_map(seq, tile):
    per_batch = seq // tile
    return lambda i, *_: (i // per_batch, lax.rem(i, per_batch), 0)


def _peer_query(h1, wq, k1, k2, *, seq, tq):
    n_tok = h1.shape[0] * seq
    slot_shape = (PEER_TOPK, PEER_HEADS, tq)
    return pl.pallas_call(
        _query_kernel,
        grid=(n_tok // tq,),
        in_specs=[
            pl.BlockSpec((None, tq, D_MODEL), _token_tile_map(seq, tq)),
            pl.BlockSpec((D_MODEL, 2 * PEER_HALF * PEER_HEADS), lambda i: (0, 0)),
            pl.BlockSpec((PEER_HEADS, N_KEYS, PEER_HALF), lambda i: (0, 0, 0)),
            pl.BlockSpec((PEER_HEADS, N_KEYS, PEER_HALF), lambda i: (0, 0, 0)),
        ],
        out_specs=[
            pl.BlockSpec(slot_shape, lambda i: (0, 0, i)),
            pl.BlockSpec(slot_shape, lambda i: (0, 0, i)),
        ],
        out_shape=[
            jax.ShapeDtypeStruct((PEER_TOPK, PEER_HEADS, n_tok), I32),
            jax.ShapeDtypeStruct((PEER_TOPK, PEER_HEADS, n_tok), F32),
        ],
        scratch_shapes=[
            pltpu.VMEM((N_KEYS * SUBLANES, tq), F32),
            pltpu.VMEM((N_KEYS * SUBLANES, tq), F32),
            pltpu.VMEM(slot_shape, F32),
            pltpu.VMEM(slot_shape, I32),
            pltpu.VMEM(slot_shape, F32),
            pltpu.VMEM(slot_shape, I32),
            pltpu.VMEM((len(_CAND) * SUBLANES, tq), F32),
            pltpu.VMEM((len(_CAND) * SUBLANES, tq), I32),
            pltpu.VMEM(slot_shape, F32),
        ],
        compiler_params=_params(("arbitrary",)),
        name="peer_query",
    )(h1, wq, k1, k2)


IB_PER_STEP = 8
EXPERTS_PER_STEP = IB_PER_STEP * N_KEYS


def _one_hot_rows(idx_row):
    iota = lax.broadcasted_iota(I32, (N_KEYS, N_SLOTS), 0)
    return jnp.where(iota == idx_row, 1.0, 0.0).astype(BF16)


def _hidden_kernel(x_ref, u_ref, ia_ref, ib_ref, gate_ref, act_ref, xb_ref, h_ref, hs_ref, *, tb, tbp):
    j = pl.program_id(1)

    @pl.when(j == 0)
    def _():
        xb_ref[...] = x_ref[...].astype(BF16)

    hc = _dot_nt(xb_ref[...], u_ref[...])
    for i in range(IB_PER_STEP):
        start = pl.multiple_of((j * IB_PER_STEP + i) * tbp, SUBLANES)
        h_ref[pl.ds(start, tb), :] = hc[:, i * N_KEYS:(i + 1) * N_KEYS]

    @pl.when(j == pl.num_programs(1) - 1)
    def _():
        iota = lax.broadcasted_iota(I32, (N_KEYS, N_SLOTS), 0)

        def per_token(t, carry):
            ia_row = ia_ref[pl.ds(t, 1), :]
            ib_row = ib_ref[pl.ds(t, 1), :]
            sel_a = _one_hot_rows(ia_row)
            h_t = h_ref[pl.ds(t, N_KEYS, stride=tbp), :]
            hi = h_t.astype(BF16)
            lo = (h_t - hi.astype(F32)).astype(BF16)
            by_slot = _dot(hi, sel_a) + _dot(lo, sel_a)
            picked = jnp.where(iota == ib_row, by_slot, 0.0)
            hs_ref[pl.ds(t, 1), :] = jnp.sum(picked, axis=0, keepdims=True)
            return carry

        lax.fori_loop(0, tb, per_token, 0)
        hid = hs_ref[...]
        gelu = 0.5 * hid * (1.0 + lax.erf(hid * (2.0 ** -0.5)))
        act_ref[...] = gelu * gate_ref[...]


def _peer_hidden(h1, u_perm, ia, ib, gate, *, seq, tb):
    n_tok = h1.shape[0] * seq
    tbp = tb + SUBLANES
    n_steps = (N_KEYS * N_KEYS) // EXPERTS_PER_STEP
    tok = lambda i, j: (i, 0)
    return pl.pallas_call(
        functools.partial(_hidden_kernel, tb=tb, tbp=tbp),
        grid=(n_tok // tb, n_steps),
        in_specs=[
            pl.BlockSpec((None, tb, D_MODEL), _token_tile_map(seq, tb)),
            pl.BlockSpec((EXPERTS_PER_STEP, D_MODEL), lambda i, j: (j, 0)),
            pl.BlockSpec((tb, N_SLOTS), tok),
            pl.BlockSpec((tb, N_SLOTS), tok),
            pl.BlockSpec((tb, N_SLOTS), tok),
        ],
        out_specs=pl.BlockSpec((tb, N_SLOTS), tok),
        out_shape=jax.ShapeDtypeStruct((n_tok, N_SLOTS), F32),
        scratch_shapes=[
            pltpu.VMEM((tb, D_MODEL), BF16),
            pltpu.VMEM((N_KEYS * tbp, N_KEYS), F32),
            pltpu.VMEM((tb, N_SLOTS), F32),
        ],
        compiler_params=_params(("arbitrary", "arbitrary")),
        name="peer_hidden",
    )(h1, u_perm, ia, ib, gate)


def _output_kernel(act_ref, ia_ref, ib_ref, v_ref, h1_ref, g_ref, b_ref, out_ref, a_ref, acc_ref, *, tb, tbp):
    j = pl.program_id(1)

    @pl.when(j == 0)
    def _():
        iota = lax.broadcasted_iota(I32, (N_KEYS, N_SLOTS), 0)

        def per_token(t, carry):
            ia_row = ia_ref[pl.ds(t, 1), :]
            ib_row = ib_ref[pl.ds(t, 1), :]
            act_row = act_ref[pl.ds(t, 1), :]
            sel_a = _one_hot_rows(ia_row)
            act_b = jnp.where(iota == ib_row, act_row, 0.0).astype(BF16)
            a_ref[pl.ds(t, N_KEYS, stride=tbp), :] = _dot_nt(act_b, sel_a)
            return carry

        lax.fori_loop(0, tb, per_token, 0)
        acc_ref[...] = jnp.zeros_like(acc_ref)

    tiles = []
    for i in range(IB_PER_STEP):
        start = pl.multiple_of((j * IB_PER_STEP + i) * tbp, SUBLANES)
        tiles.append(a_ref[pl.ds(start, tb), :].astype(BF16))
    acc_ref[...] += _dot(jnp.concatenate(tiles, axis=1), v_ref[...])

    @pl.when(j == pl.num_programs(1) - 1)
    def _():
        out_ref[...] = _layer_norm(ALPHA * h1_ref[...] + acc_ref[...], g_ref[...], b_ref[...])


def _peer_output(act, ia, ib, v_perm, h1, g2, b2, *, seq, tb):
    n_tok = h1.shape[0] * seq
    tbp = tb + SUBLANES
    n_steps = (N_KEYS * N_KEYS) // EXPERTS_PER_STEP
    tok = lambda i, j: (i, 0)
    const = lambda i, j: (0, 0)
    return pl.pallas_call(
        functools.partial(_output_kernel, tb=tb, tbp=tbp),
        grid=(n_tok // tb, n_steps),
        in_specs=[
            pl.BlockSpec((tb, N_SLOTS), tok),
            pl.BlockSpec((tb, N_SLOTS), tok),
            pl.BlockSpec((tb, N_SLOTS), tok),
            pl.BlockSpec((EXPERTS_PER_STEP, D_MODEL), lambda i, j: (j, 0)),
            pl.BlockSpec((None, tb, D_MODEL), _token_tile_map(seq, tb)),
            pl.BlockSpec((1, D_MODEL), const),
            pl.BlockSpec((1, D_MODEL), const),
        ],
        out_specs=pl.BlockSpec((tb, D_MODEL), tok),
        out_shape=jax.ShapeDtypeStruct((n_tok, D_MODEL), F32),
        scratch_shapes=[
            pltpu.VMEM((N_KEYS * tbp, N_KEYS), F32),
            pltpu.VMEM((tb, D_MODEL), F32),
        ],
        compiler_params=_params(("arbitrary", "arbitrary")),
        name="peer_output",
    )(act, ia, ib, v_perm, h1, g2, b2)


def _pick_tile(n, candidates):
    for c in candidates:
        if n % c == 0:
            return c
    raise ValueError(f"no tile in {candidates} divides {n}")


def _expert_rows_ib_major(tab):
    d = tab.shape[-1]
    return tab.reshape(N_KEYS, N_KEYS, d).transpose(1, 0, 2).reshape(N_KEYS * N_KEYS, d).astype(BF16)


def kernel(x, meta, ln0_g, ln0_b, w_in, pool_w, pool_scale, w_br_pool, conv_w, a_log, dt_bias, onorm_g, w_br_delta, w_out, ln1_g, ln1_b, peer_wq, peer_k1, peer_k2, peer_u, peer_v, ln2_g, ln2_b):
    batch, seq, d = x.shape
    assert d == D_MODEL and seq % CHUNK == 0 and w_in.shape[0] == 1
    t_len = seq + CHUNK
    n_tok = batch * seq
    row = lambda p: p.reshape(1, -1).astype(F32)

    meta_b = jnp.broadcast_to(meta[None].astype(x.dtype), (batch, N_META, d))
    h0 = jnp.concatenate([x, jnp.zeros((batch, PAD, d), x.dtype), meta_b], axis=1).reshape(batch * t_len, d)

    w = w_in[0]
    c_pool, c_qkv, c_z = POOL_WIDTH, POOL_WIDTH + 3 * DN_WIDTH, POOL_WIDTH + 4 * DN_WIDTH
    c_b, c_a = c_z + DN_HEADS, c_z + 2 * DN_HEADS
    w_main = jnp.concatenate([w[:, c_pool:c_z], w[:, c_a:], w[:, :c_pool]], axis=1).astype(BF16)
    w_ba = jnp.zeros((d, BA_WIDTH), F32)
    w_ba = w_ba.at[:, :DN_HEADS].set(w[:, c_z:c_b]).at[:, LANES:LANES + DN_HEADS].set(w[:, c_b:c_a]).astype(BF16)
    lane_pad = lambda p: jnp.zeros((1, LANES), F32).at[0, :DN_HEADS].set(p.astype(F32))

    proj, ba = _inproj(h0, row(ln0_g), row(ln0_b), w_main, w_ba, seq=seq, t_len=t_len)
    ya = _pool_branch(proj, pool_w[0].astype(BF16), row(pool_scale[0]), w_br_pool[0].astype(BF16),
                      batch=batch, seq=seq, t_len=t_len)
    qkvn = _conv_qkv(proj, conv_w[0].astype(F32), batch=batch, seq=seq, t_len=t_len)
    o = _delta_rule(qkvn, ba, proj, lane_pad(-jnp.exp(a_log[0].astype(F32))), lane_pad(dt_bias[0]),
                    row(onorm_g[0]), batch=batch, t_len=t_len)
    h1 = _merge(o, ya, proj, h0, w_br_delta[0].astype(BF16), w_out[0].astype(BF16), row(ln0_g), row(ln0_b),
                row(ln1_g[0]), row(ln1_b[0]), batch=batch, seq=seq, t_len=t_len)
    h1 = h1.reshape(batch, t_len, d)

    idx, gate = _peer_query(h1, peer_wq[0].astype(BF16), peer_k1[0].astype(BF16), peer_k2[0].astype(BF16),
                            seq=seq, tq=LANES)
    idx = idx.reshape(N_SLOTS, n_tok).T
    gate = gate.reshape(N_SLOTS, n_tok).T
    ia = lax.shift_right_logical(idx, 7)
    ib = lax.bitwise_and(idx, N_KEYS - 1)

    tb = _pick_tile(seq, (256, 128))
    act = _peer_hidden(h1, _expert_rows_ib_major(peer_u[0]), ia, ib, gate, seq=seq, tb=tb)
    out = _peer_output(act, ia, ib, _expert_rows_ib_major(peer_v[0]), h1, row(ln2_g[0]), row(ln2_b[0]),
                       seq=seq, tb=tb)
    return out.reshape(batch, seq, d)
```

```python
import functools
import math

import jax
import jax.numpy as jnp
from jax import lax
from jax.experimental import pallas as pl
from jax.experimental.pallas import tpu as pltpu

F32 = jnp.float32
BF16 = jnp.bfloat16
I32 = jnp.int32

D_MODEL = 1024
N_META = 16
CHUNK = 64
PAD = CHUNK - N_META
POOL_GROUPS = 4
POOL_GROUP_DIM = 128
POOL_WIDTH = POOL_GROUPS * POOL_GROUP_DIM
POOL_WINDOWS = (2, 4, 8, 16)
POOL_HALO = 16
DN_HEADS = 8
DN_HEAD_DIM = 128
DN_WIDTH = DN_HEADS * DN_HEAD_DIM
CONV_WIDTH = 4
PEER_HEADS = 8
PEER_HALF = 128
N_KEYS = 128
PEER_TOPK = 16
N_SLOTS = PEER_HEADS * PEER_TOPK
LN_EPS = 1e-5
RMS_EPS = 1e-6
L2_EPS = 1e-6
ALPHA = 2.0 ** 0.25

COL_QKV = 0
COL_Z = 3 * DN_WIDTH
COL_GP = COL_Z + DN_WIDTH
COL_GD = COL_GP + D_MODEL
COL_POOL = COL_GD + D_MODEL
N_PROJ = COL_POOL + POOL_WIDTH
BA_WIDTH = 256

SUBLANES = 8
LANES = 128
VMEM_LIMIT = 56 * 1024 * 1024

_CAND = [(a, b) for a in range(PEER_TOPK) for b in range(PEER_TOPK) if (a + 1) * (b + 1) <= PEER_TOPK]


def _params(sem):
    return pltpu.CompilerParams(dimension_semantics=sem, vmem_limit_bytes=VMEM_LIMIT)


def _layer_norm(x, g, b):
    mu = jnp.mean(x, axis=-1, keepdims=True)
    xc = x - mu
    var = jnp.mean(xc * xc, axis=-1, keepdims=True)
    return xc * lax.rsqrt(var + LN_EPS) * g + b


def _zero_pad_rows(y, row, seq):
    return jnp.where(row >= seq, jnp.where(row < seq + PAD, 0.0, y), y)


def _sigmoid(x):
    return 1.0 / (1.0 + jnp.exp(-x))


def _dot(a, b):
    return jnp.dot(a, b, preferred_element_type=F32)


def _dot_nt(a, b):
    return lax.dot_general(a, b, (((1,), (1,)), ((), ())), preferred_element_type=F32)


def _inproj_kernel(x_ref, g_ref, b_ref, w_ref, wba_ref, proj_ref, ba_ref, xn_ref, *, tiles_per_batch, seq):
    i = pl.program_id(0)
    j = pl.program_id(1)
    tm = x_ref.shape[0]

    @pl.when(j == 0)
    def _():
        y = _layer_norm(x_ref[...], g_ref[...], b_ref[...])
        row = lax.broadcasted_iota(I32, (tm, 1), 0) + lax.rem(i, tiles_per_batch) * tm
        xn = _zero_pad_rows(y, row, seq).astype(BF16)
        xn_ref[...] = xn
        ba_ref[...] = _dot(xn, wba_ref[...])

    proj_ref[...] = _dot(xn_ref[...], w_ref[...])


def _inproj(h0, ln_g, ln_b, w_main, w_ba, *, seq, t_len):
    rows = h0.shape[0]
    tiles_per_batch = 4
    tm = t_len // tiles_per_batch
    tn = 512
    kern = functools.partial(_inproj_kernel, tiles_per_batch=tiles_per_batch, seq=seq)
    return pl.pallas_call(
        kern,
        grid=(rows // tm, N_PROJ // tn),
        in_specs=[
            pl.BlockSpec((tm, D_MODEL), lambda i, j: (i, 0)),
            pl.BlockSpec((1, D_MODEL), lambda i, j: (0, 0)),
            pl.BlockSpec((1, D_MODEL), lambda i, j: (0, 0)),
            pl.BlockSpec((D_MODEL, tn), lambda i, j: (0, j)),
            pl.BlockSpec((D_MODEL, BA_WIDTH), lambda i, j: (0, 0)),
        ],
        out_specs=[
            pl.BlockSpec((tm, tn), lambda i, j: (i, j)),
            pl.BlockSpec((tm, BA_WIDTH), lambda i, j: (i, 0)),
        ],
        out_shape=[
            jax.ShapeDtypeStruct((rows, N_PROJ), F32),
            jax.ShapeDtypeStruct((rows, BA_WIDTH), F32),
        ],
        scratch_shapes=[pltpu.VMEM((tm, D_MODEL), BF16)],
        compiler_params=_params(("arbitrary", "arbitrary")),
        name="inproj",
    )(h0, ln_g, ln_b, w_main, w_ba)


def _pool_kernel(xp_ref, halo_ref, pw_ref, ps_ref, wbp_ref, ya_ref, *, seq):
    i = pl.program_id(1)
    tm = xp_ref.shape[0]
    v = jnp.concatenate([halo_ref[...], xp_ref[...]], axis=0)
    row = lax.broadcasted_iota(I32, (tm, POOL_GROUP_DIM), 0) + i * tm
    meta_pos = row - (seq + PAD)
    outs = []
    for gi, w in enumerate(POOL_WINDOWS):
        vg = v[:, gi * POOL_GROUP_DIM:(gi + 1) * POOL_GROUP_DIM]
        s = vg
        shift = 1
        while shift < w:
            s = s + pltpu.roll(s, shift, axis=0)
            shift *= 2
        s = s[POOL_HALO:]
        xg = vg[POOL_HALO:]
        count = jnp.where(meta_pos >= 0, jnp.minimum(meta_pos + 1, w), w).astype(F32)
        pooled = s / count - xg
        mixed = _dot(pooled.astype(BF16), pw_ref[gi])
        outs.append(mixed)
    y_pool = jnp.concatenate(outs, axis=1) * ps_ref[...]
    ya_ref[...] = _dot(y_pool.astype(BF16), wbp_ref[...])


def _pool_branch(proj, pool_w, pool_scale, w_br_pool, *, batch, seq, t_len):
    rows = proj.shape[0]
    tiles_per_batch = 4
    tm = t_len // tiles_per_batch
    halo_blocks_per_batch = t_len // POOL_HALO
    halo_blocks_per_tile = tm // POOL_HALO
    pool_col = COL_POOL // POOL_WIDTH

    def halo_map(b, i):
        prev = lax.rem(i * halo_blocks_per_tile + halo_blocks_per_batch - 1, halo_blocks_per_batch)
        return (b * halo_blocks_per_batch + prev, pool_col)

    return pl.pallas_call(
        functools.partial(_pool_kernel, seq=seq),
        grid=(batch, tiles_per_batch),
        in_specs=[
            pl.BlockSpec((tm, POOL_WIDTH), lambda b, i: (b * tiles_per_batch + i, pool_col)),
            pl.BlockSpec((POOL_HALO, POOL_WIDTH), halo_map),
            pl.BlockSpec((POOL_GROUPS, POOL_GROUP_DIM, POOL_GROUP_DIM), lambda b, i: (0, 0, 0)),
            pl.BlockSpec((1, POOL_WIDTH), lambda b, i: (0, 0)),
            pl.BlockSpec((POOL_WIDTH, D_MODEL), lambda b, i: (0, 0)),
        ],
        out_specs=pl.BlockSpec((tm, D_MODEL), lambda b, i: (b * tiles_per_batch + i, 0)),
        out_shape=jax.ShapeDtypeStruct((rows, D_MODEL), F32),
        compiler_params=_params(("arbitrary", "arbitrary")),
        name="pool_branch",
    )(proj, proj, pool_w, pool_scale, w_br_pool)


def _conv_kernel(x_ref, w_ref, o_ref, *, seq):
    s = pl.program_id(1)
    x = x_ref[...]
    t_len = x.shape[0]
    w = w_ref[...]
    y = x * w[CONV_WIDTH - 1:CONV_WIDTH, :]
    for lag in range(1, CONV_WIDTH):
        y = y + pltpu.roll(x, lag, axis=0) * w[CONV_WIDTH - 1 - lag:CONV_WIDTH - lag, :]
    y = y * _sigmoid(y)
    ss = jnp.sum(y * y, axis=-1, keepdims=True)
    q_scale = jnp.where(s < DN_HEADS, DN_HEAD_DIM ** -0.5, 1.0).astype(F32)
    fac = jnp.where(s < 2 * DN_HEADS, lax.rsqrt(ss + L2_EPS) * q_scale, 1.0)
    row = lax.broadcasted_iota(I32, (t_len, 1), 0)
    o_ref[...] = _zero_pad_rows(y * fac, row, seq)


def _conv_qkv(proj, conv_w, *, batch, seq, t_len):
    rows = proj.shape[0]
    n_slabs = 3 * DN_HEADS
    return pl.pallas_call(
        functools.partial(_conv_kernel, seq=seq),
        grid=(batch, n_slabs),
        in_specs=[
            pl.BlockSpec((t_len, DN_HEAD_DIM), lambda b, s: (b, COL_QKV // DN_HEAD_DIM + s)),
            pl.BlockSpec((CONV_WIDTH, DN_HEAD_DIM), lambda b, s: (0, s)),
        ],
        out_specs=pl.BlockSpec((t_len, DN_HEAD_DIM), lambda b, s: (b, s)),
        out_shape=jax.ShapeDtypeStruct((rows, 3 * DN_WIDTH), F32),
        compiler_params=_params(("arbitrary", "arbitrary")),
        name="conv_qkv",
    )(proj, conv_w)


def _inv_unit_lower(a, masks, eye):
    m8, m16, m32, m64 = masks
    n = [x * m8 for x in a]
    n2 = [_dot(x, x) for x in n]
    n4 = [_dot(x, x) for x in n2]
    t = [eye - x for x in n]
    t = [x + _dot(x, y) for x, y in zip(t, n2)]
    t = [x + _dot(x, y) for x, y in zip(t, n4)]
    for m in (m16, m32, m64):
        at = [_dot(x * m, y) for x, y in zip(a, t)]
        t = [x - _dot(x, y) for x, y in zip(t, at)]
    return t


def _delta_kernel(q_ref, k_ref, v_ref, ba_ref, z_ref, nega_ref, dtb_ref, og_ref, o_ref, state_ref):
    c = pl.program_id(1)

    @pl.when(c == 0)
    def _():
        state_ref[...] = jnp.zeros_like(state_ref)

    r = lax.broadcasted_iota(I32, (CHUNK, CHUNK), 0)
    cc = lax.broadcasted_iota(I32, (CHUNK, CHUNK), 1)
    causal = r >= cc
    strict = r > cc
    eye = jnp.where(r == cc, 1.0, 0.0).astype(F32)
    tril = jnp.where(causal, 1.0, 0.0).astype(F32)

    def blk(x, s):
        return lax.shift_right_logical(x, s)

    lower = jnp.where(strict, 1.0, 0.0).astype(F32)
    m8 = jnp.where(blk(r, 3) == blk(cc, 3), lower, 0.0)
    m16 = jnp.where((blk(r, 4) == blk(cc, 4)) & (blk(r, 3) != blk(cc, 3)), lower, 0.0)
    m32 = jnp.where((blk(r, 5) == blk(cc, 5)) & (blk(r, 4) != blk(cc, 4)), lower, 0.0)
    m64 = jnp.where(blk(r, 5) != blk(cc, 5), lower, 0.0)
    masks = (m8, m16, m32, m64)

    ba = ba_ref[...]
    beta_all = _sigmoid(ba[:, :LANES])
    x = ba[:, LANES:] + dtb_ref[...]
    softplus = jnp.maximum(x, 0.0) + jnp.log(1.0 + jnp.exp(-jnp.abs(x)))
    g_all = nega_ref[...] * softplus
    gcum = lax.dot_general(tril, g_all, (((1,), (0,)), ((), ())), precision=lax.Precision.HIGHEST,
                           preferred_element_type=F32)
    gcum_t = gcum.T

    heads = range(DN_HEADS)
    sl = [slice(h * DN_HEAD_DIM, (h + 1) * DN_HEAD_DIM) for h in heads]
    q = [q_ref[:, s] for s in sl]
    k = [k_ref[:, s] for s in sl]
    v = [v_ref[:, s] for s in sl]
    beta = [beta_all[:, h:h + 1] for h in heads]
    gc = [gcum[:, h:h + 1] for h in heads]
    gr = [gcum_t[h:h + 1, :] for h in heads]
    g_last = [gcum[CHUNK - 1:CHUNK, h:h + 1] for h in heads]
    decay = [jnp.where(causal, jnp.exp(jnp.where(causal, gc[h] - gr[h], 0.0)), 0.0) for h in heads]
    eg = [jnp.exp(gc[h]) for h in heads]
    kb = [k[h] * beta[h] for h in heads]
    kk = [_dot_nt(kb[h], k[h]) for h in heads]
    qk = [_dot_nt(q[h], k[h]) for h in heads]
    a_mat = [jnp.where(strict, kk[h] * decay[h], 0.0) for h in heads]
    t_inv = _inv_unit_lower(a_mat, masks, eye)
    rhs = [jnp.concatenate([v[h] * beta[h], kb[h] * eg[h]], axis=1) for h in heads]
    uw = [_dot(t_inv[h], rhs[h]) for h in heads]
    s_prev = [state_ref[h] for h in heads]
    wq = [jnp.concatenate([uw[h][:, DN_HEAD_DIM:], q[h] * eg[h]], axis=0) for h in heads]
    ws = [_dot(wq[h], s_prev[h]) for h in heads]
    v_new = [uw[h][:, :DN_HEAD_DIM] - ws[h][:CHUNK] for h in heads]
    attn = [qk[h] * decay[h] for h in heads]
    intra = [_dot(attn[h], v_new[h]) for h in heads]
    k_dec_t = [(k[h] * jnp.exp(g_last[h] - gc[h])).T for h in heads]
    kv = [_dot(k_dec_t[h], v_new[h]) for h in heads]
    for h in heads:
        state_ref[h] = s_prev[h] * jnp.exp(g_last[h]) + kv[h]
        out = ws[h][CHUNK:] + intra[h]
        out = out * lax.rsqrt(jnp.mean(out * out, axis=-1, keepdims=True) + RMS_EPS) * og_ref[...]
        zh = z_ref[:, sl[h]]
        o_ref[:, sl[h]] = (out * (zh * _sigmoid(zh))).astype(o_ref.dtype)


def _delta_rule(qkvn, ba, proj, neg_a, dt_bias, onorm_g, *, batch, t_len):
    rows = qkvn.shape[0]
    n_chunks = t_len // CHUNK

    def mem_chunk(b, c):
        return b * n_chunks + lax.rem(c + n_chunks - 1, n_chunks)

    return pl.pallas_call(
        _delta_kernel,
        grid=(batch, n_chunks),
        in_specs=[
            pl.BlockSpec((CHUNK, DN_WIDTH), lambda b, c: (mem_chunk(b, c), 0)),
            pl.BlockSpec((CHUNK, DN_WIDTH), lambda b, c: (mem_chunk(b, c), 1)),
            pl.BlockSpec((CHUNK, DN_WIDTH), lambda b, c: (mem_chunk(b, c), 2)),
            pl.BlockSpec((CHUNK, BA_WIDTH), lambda b, c: (mem_chunk(b, c), 0)),
            pl.BlockSpec((CHUNK, DN_WIDTH), lambda b, c: (mem_chunk(b, c), COL_Z // DN_WIDTH)),
            pl.BlockSpec((1, LANES), lambda b, c: (0, 0)),
            pl.BlockSpec((1, LANES), lambda b, c: (0, 0)),
            pl.BlockSpec((1, DN_HEAD_DIM), lambda b, c: (0, 0)),
        ],
        out_specs=pl.BlockSpec((CHUNK, DN_WIDTH), lambda b, c: (mem_chunk(b, c), 0)),
        out_shape=jax.ShapeDtypeStruct((rows, DN_WIDTH), BF16),
        scratch_shapes=[pltpu.VMEM((DN_HEADS, DN_HEAD_DIM, DN_HEAD_DIM), F32)],
        compiler_params=_params(("arbitrary", "arbitrary")),
        name="delta_rule",
    )(qkvn, qkvn, qkvn, ba, proj, neg_a, dt_bias, onorm_g)


def _merge_kernel(o_ref, ya_ref, gp_ref, gd_ref, h0_ref, wd_ref, wo_ref, g0_ref, b0_ref, g1_ref, b1_ref, h1_ref):
    y_b = _dot(o_ref[...], wd_ref[...])
    merged = _sigmoid(gp_ref[...]) * ya_ref[...] + _sigmoid(gd_ref[...]) * y_b
    mix = _dot(merged.astype(BF16), wo_ref[...])
    h = _layer_norm(h0_ref[...], g0_ref[...], b0_ref[...])
    h1_ref[...] = _layer_norm(ALPHA * h + mix, g1_ref[...], b1_ref[...])


def _merge(o, ya, proj, h0, w_br_delta, w_out, g0, b0, g1, b1, *, batch, seq, t_len):
    bf16_rows = 2 * SUBLANES
    tiles_per_batch = next(n for n in (10, 8, 4, 2, 1) if t_len % (n * bf16_rows) == 0)
    tm = t_len // tiles_per_batch
    row = lambda i: (i, 0)
    const = lambda i: (0, 0)
    return pl.pallas_call(
        _merge_kernel,
        grid=(batch * tiles_per_batch,),
        in_specs=[
            pl.BlockSpec((tm, DN_WIDTH), row),
            pl.BlockSpec((tm, D_MODEL), row),
            pl.BlockSpec((tm, D_MODEL), lambda i: (i, COL_GP // D_MODEL)),
            pl.BlockSpec((tm, D_MODEL), lambda i: (i, COL_GD // D_MODEL)),
            pl.BlockSpec((tm, D_MODEL), row),
            pl.BlockSpec((DN_WIDTH, D_MODEL), const),
            pl.BlockSpec((D_MODEL, D_MODEL), const),
            pl.BlockSpec((1, D_MODEL), const),
            pl.BlockSpec((1, D_MODEL), const),
            pl.BlockSpec((1, D_MODEL), const),
            pl.BlockSpec((1, D_MODEL), const),
        ],
        out_specs=pl.BlockSpec((tm, D_MODEL), row),
        out_shape=jax.ShapeDtypeStruct((batch * t_len, D_MODEL), F32),
        compiler_params=_params(("arbitrary",)),
        name="merge",
    )(o, ya, proj, proj, h0, w_br_delta, w_out, g0, b0, g1, b1)


def _tile(ref, k):
    return ref[k * SUBLANES:(k + 1) * SUBLANES, :]


def _argmax_tournament(vals, ids):
    nodes = list(zip(vals, ids))
    while len(nodes) > 1:
        nxt = []
        for p in range(0, len(nodes) - 1, 2):
            (va, ia), (vb, ib) = nodes[p], nodes[p + 1]
            first = va >= vb
            nxt.append((jnp.maximum(va, vb), jnp.where(first, ia, ib)))
        if len(nodes) % 2:
            nxt.append(nodes[-1])
        nodes = nxt
    return nodes[0]


def _query_kernel(x_ref, wq_ref, k1_ref, k2_ref, idx_ref, gate_ref, s1_ref, s2_ref, m1_ref, i1_ref, m2_ref, i2_ref,
                  c_ref, e_ref, sc_ref):
    xb = x_ref[...].astype(BF16)
    q = _dot(xb, wq_ref[...])
    for h in range(PEER_HEADS):
        q1 = q[:, (2 * h) * PEER_HALF:(2 * h + 1) * PEER_HALF].astype(BF16)
        q2 = q[:, (2 * h + 1) * PEER_HALF:(2 * h + 2) * PEER_HALF].astype(BF16)
        s1_ref[pl.ds(h, N_KEYS, stride=SUBLANES), :] = _dot_nt(k1_ref[h], q1)
        s2_ref[pl.ds(h, N_KEYS, stride=SUBLANES), :] = _dot_nt(k2_ref[h], q2)

    neg_inf = jnp.float32(-jnp.inf)

    def sub_round(r, carry):
        for s_ref, m_ref, i_ref in ((s1_ref, m1_ref, i1_ref), (s2_ref, m2_ref, i2_ref)):
            vals = [_tile(s_ref, k) for k in range(N_KEYS)]
            best, arg = _argmax_tournament(vals, list(range(N_KEYS)))
            m_ref[r] = best
            i_ref[r] = arg
            for k in range(N_KEYS):
                s_ref[k * SUBLANES:(k + 1) * SUBLANES, :] = jnp.where(arg == k, neg_inf, vals[k])
        return carry

    lax.fori_loop(0, PEER_TOPK, sub_round, 0)

    for j, (a, b) in enumerate(_CAND):
        c_ref[j * SUBLANES:(j + 1) * SUBLANES, :] = m1_ref[a] + m2_ref[b]
        e_ref[j * SUBLANES:(j + 1) * SUBLANES, :] = i1_ref[a] * N_KEYS + i2_ref[b]

    def pair_round(r, carry):
        vals = [_tile(c_ref, j) for j in range(len(_CAND))]
        ids = [_tile(e_ref, j) for j in range(len(_CAND))]
        best, arg = _argmax_tournament(vals, ids)
        sc_ref[r] = best
        idx_ref[r] = arg
        for j in range(len(_CAND)):
            c_ref[j * SUBLANES:(j + 1) * SUBLANES, :] = jnp.where(ids[j] == arg, neg_inf, vals[j])
        return carry

    lax.fori_loop(0, PEER_TOPK, pair_round, 0)

    sc = sc_ref[...]
    ex = jnp.exp(sc - sc[0:1])
    gate_ref[...] = ex / jnp.sum(ex, axis=0, keepdims=True)


def _token_tile_map(seq, tile):
    per_batch = seq // tile
    return lambda i, *_: (i // per_batch, lax.rem(i, per_batch), 0)


def _peer_query(h1, wq, k1, k2, *, seq, tq):
    n_tok = h1.shape[0] * seq
    slot_shape = (PEER_TOPK, PEER_HEADS, tq)
    return pl.pallas_call(
        _query_kernel,
        grid=(n_tok // tq,),
        in_specs=[
            pl.BlockSpec((None, tq, D_MODEL), _token_tile_map(seq, tq)),
            pl.BlockSpec((D_MODEL, 2 * PEER_HALF * PEER_HEADS), lambda i: (0, 0)),
            pl.BlockSpec((PEER_HEADS, N_KEYS, PEER_HALF), lambda i: (0, 0, 0)),
            pl.BlockSpec((PEER_HEADS, N_KEYS, PEER_HALF), lambda i: (0, 0, 0)),
        ],
        out_specs=[
            pl.BlockSpec(slot_shape, lambda i: (0, 0, i)),
            pl.BlockSpec(slot_shape, lambda i: (0, 0, i)),
        ],
        out_shape=[
            jax.ShapeDtypeStruct((PEER_TOPK, PEER_HEADS, n_tok), I32),
            jax.ShapeDtypeStruct((PEER_TOPK, PEER_HEADS, n_tok), F32),
        ],
        scratch_shapes=[
            pltpu.VMEM((N_KEYS * SUBLANES, tq), F32),
            pltpu.VMEM((N_KEYS * SUBLANES, tq), F32),
            pltpu.VMEM(slot_shape, F32),
            pltpu.VMEM(slot_shape, I32),
            pltpu.VMEM(slot_shape, F32),
            pltpu.VMEM(slot_shape, I32),
            pltpu.VMEM((len(_CAND) * SUBLANES, tq), F32),
            pltpu.VMEM((len(_CAND) * SUBLANES, tq), I32),
            pltpu.VMEM(slot_shape, F32),
        ],
        compiler_params=_params(("arbitrary",)),
        name="peer_query",
    )(h1, wq, k1, k2)


IB_PER_STEP = 8
EXPERTS_PER_STEP = IB_PER_STEP * N_KEYS
TOKENS_PER_ITER = SUBLANES


def _one_hot_rows(idx_row):
    iota = lax.broadcasted_iota(I32, (N_KEYS, N_SLOTS), 0)
    return jnp.where(iota == idx_row, 1.0, 0.0).astype(BF16)


def _hidden_kernel(x_ref, u_ref, ia_ref, ib_ref, gate_ref, act_ref, xb_ref, h_ref, hs_ref, *, tb, tbp):
    j = pl.program_id(1)

    @pl.when(j == 0)
    def _():
        xb_ref[...] = x_ref[...].astype(BF16)

    hc = _dot_nt(xb_ref[...], u_ref[...])
    for i in range(IB_PER_STEP):
        start = pl.multiple_of((j * IB_PER_STEP + i) * tbp, SUBLANES)
        h_ref[pl.ds(start, tb), :] = hc[:, i * N_KEYS:(i + 1) * N_KEYS]

    @pl.when(j == pl.num_programs(1) - 1)
    def _():
        iota = lax.broadcasted_iota(I32, (N_KEYS, N_SLOTS), 0)

        def token_group(g, carry):
            base = pl.multiple_of(g * TOKENS_PER_ITER, TOKENS_PER_ITER)
            ia_rows = ia_ref[pl.ds(base, TOKENS_PER_ITER), :]
            ib_rows = ib_ref[pl.ds(base, TOKENS_PER_ITER), :]
            lhs, sel = [], []
            for u in range(TOKENS_PER_ITER):
                sel_a = _one_hot_rows(ia_rows[u:u + 1, :])
                sel.append(jnp.concatenate([sel_a, sel_a], axis=0))
                h_t = h_ref[pl.ds(base + u, N_KEYS, stride=tbp), :]
                hi = h_t.astype(BF16)
                lo = (h_t - hi.astype(F32)).astype(BF16)
                lhs.append(jnp.concatenate([hi, lo], axis=1))
            by_slot = [_dot(lhs[u], sel[u]) for u in range(TOKENS_PER_ITER)]
            rows = []
            for u in range(TOKENS_PER_ITER):
                picked = jnp.where(iota == ib_rows[u:u + 1, :], by_slot[u], 0.0)
                rows.append(jnp.sum(picked, axis=0, keepdims=True))
            hs_ref[pl.ds(base, TOKENS_PER_ITER), :] = jnp.concatenate(rows, axis=0)
            return carry

        lax.fori_loop(0, tb // TOKENS_PER_ITER, token_group, 0)
        hid = hs_ref[...]
        gelu = 0.5 * hid * (1.0 + lax.erf(hid * (2.0 ** -0.5)))
        act_ref[...] = gelu * gate_ref[...]


def _peer_hidden(h1, u_perm, ia, ib, gate, *, seq, tb):
    n_tok = h1.shape[0] * seq
    tbp = tb + SUBLANES
    n_steps = (N_KEYS * N_KEYS) // EXPERTS_PER_STEP
    tok = lambda i, j: (i, 0)
    return pl.pallas_call(
        functools.partial(_hidden_kernel, tb=tb, tbp=tbp),
        grid=(n_tok // tb, n_steps),
        in_specs=[
            pl.BlockSpec((None, tb, D_MODEL), _token_tile_map(seq, tb)),
            pl.BlockSpec((EXPERTS_PER_STEP, D_MODEL), lambda i, j: (j, 0)),
            pl.BlockSpec((tb, N_SLOTS), tok),
            pl.BlockSpec((tb, N_SLOTS), tok),
            pl.BlockSpec((tb, N_SLOTS), tok),
        ],
        out_specs=pl.BlockSpec((tb, N_SLOTS), tok),
        out_shape=jax.ShapeDtypeStruct((n_tok, N_SLOTS), F32),
        scratch_shapes=[
            pltpu.VMEM((tb, D_MODEL), BF16),
            pltpu.VMEM((N_KEYS * tbp, N_KEYS), F32),
            pltpu.VMEM((tb, N_SLOTS), F32),
        ],
        compiler_params=_params(("arbitrary", "arbitrary")),
        name="peer_hidden",
    )(h1, u_perm, ia, ib, gate)


def _output_kernel(act_ref, ia_ref, ib_ref, v_ref, h1_ref, g_ref, b_ref, out_ref, a_ref, acc_ref, *, tb, tbp):
    j = pl.program_id(1)

    @pl.when(j == 0)
    def _():
        iota = lax.broadcasted_iota(I32, (N_KEYS, N_SLOTS), 0)

        def token_group(g, carry):
            base = pl.multiple_of(g * TOKENS_PER_ITER, TOKENS_PER_ITER)
            ia_rows = ia_ref[pl.ds(base, TOKENS_PER_ITER), :]
            ib_rows = ib_ref[pl.ds(base, TOKENS_PER_ITER), :]
            act_rows = act_ref[pl.ds(base, TOKENS_PER_ITER), :]
            sel, act_b = [], []
            for u in range(TOKENS_PER_ITER):
                sel.append(_one_hot_rows(ia_rows[u:u + 1, :]))
                act_b.append(jnp.where(iota == ib_rows[u:u + 1, :], act_rows[u:u + 1, :], 0.0).astype(BF16))
            dense = [_dot_nt(act_b[u], sel[u]) for u in range(TOKENS_PER_ITER)]
            for u in range(TOKENS_PER_ITER):
                a_ref[pl.ds(base + u, N_KEYS, stride=tbp), :] = dense[u]
            return carry

        lax.fori_loop(0, tb // TOKENS_PER_ITER, token_group, 0)
        acc_ref[...] = jnp.zeros_like(acc_ref)

    tiles = []
    for i in range(IB_PER_STEP):
        start = pl.multiple_of((j * IB_PER_STEP + i) * tbp, SUBLANES)
        tiles.append(a_ref[pl.ds(start, tb), :].astype(BF16))
    acc_ref[...] += _dot(jnp.concatenate(tiles, axis=1), v_ref[...])

    @pl.when(j == pl.num_programs(1) - 1)
    def _():
        out_ref[...] = _layer_norm(ALPHA * h1_ref[...] + acc_ref[...], g_ref[...], b_ref[...])


def _peer_output(act, ia, ib, v_perm, h1, g2, b2, *, seq, tb):
    n_tok = h1.shape[0] * seq
    tbp = tb + SUBLANES
    n_steps = (N_KEYS * N_KEYS) // EXPERTS_PER_STEP
    tok = lambda i, j: (i, 0)
    const = lambda i, j: (0, 0)
    return pl.pallas_call(
        functools.partial(_output_kernel, tb=tb, tbp=tbp),
        grid=(n_tok // tb, n_steps),
        in_specs=[
            pl.BlockSpec((tb, N_SLOTS), tok),
            pl.BlockSpec((tb, N_SLOTS), tok),
            pl.BlockSpec((tb, N_SLOTS), tok),
            pl.BlockSpec((EXPERTS_PER_STEP, D_MODEL), lambda i, j: (j, 0)),
            pl.BlockSpec((None, tb, D_MODEL), _token_tile_map(seq, tb)),
            pl.BlockSpec((1, D_MODEL), const),
            pl.BlockSpec((1, D_MODEL), const),
        ],
        out_specs=pl.BlockSpec((tb, D_MODEL), tok),
        out_shape=jax.ShapeDtypeStruct((n_tok, D_MODEL), F32),
        scratch_shapes=[
            pltpu.VMEM((N_KEYS * tbp, N_KEYS), F32),
            pltpu.VMEM((tb, D_MODEL), F32),
        ],
        compiler_params=_params(("arbitrary", "arbitrary")),
        name="peer_output",
    )(act, ia, ib, v_perm, h1, g2, b2)


def _pick_tile(n, candidates):
    for c in candidates:
        if n % c == 0:
            return c
    raise ValueError(f"no tile in {candidates} divides {n}")


def _expert_rows_ib_major(tab):
    d = tab.shape[-1]
    return tab.reshape(N_KEYS, N_KEYS, d).transpose(1, 0, 2).reshape(N_KEYS * N_KEYS, d).astype(BF16)


def kernel(x, meta, ln0_g, ln0_b, w_in, pool_w, pool_scale, w_br_pool, conv_w, a_log, dt_bias, onorm_g, w_br_delta, w_out, ln1_g, ln1_b, peer_wq, peer_k1, peer_k2, peer_u, peer_v, ln2_g, ln2_b):
    batch, seq, d = x.shape
    assert d == D_MODEL and seq % CHUNK == 0 and w_in.shape[0] == 1
    t_len = seq + CHUNK
    n_tok = batch * seq
    row = lambda p: p.reshape(1, -1).astype(F32)

    meta_b = jnp.broadcast_to(meta[None].astype(x.dtype), (batch, N_META, d))
    h0 = jnp.concatenate([x, jnp.zeros((batch, PAD, d), x.dtype), meta_b], axis=1).reshape(batch * t_len, d)

    w = w_in[0]
    c_pool, c_qkv, c_z = POOL_WIDTH, POOL_WIDTH + 3 * DN_WIDTH, POOL_WIDTH + 4 * DN_WIDTH
    c_b, c_a = c_z + DN_HEADS, c_z + 2 * DN_HEADS
    w_main = jnp.concatenate([w[:, c_pool:c_z], w[:, c_a:], w[:, :c_pool]], axis=1).astype(BF16)
    w_ba = jnp.zeros((d, BA_WIDTH), F32)
    w_ba = w_ba.at[:, :DN_HEADS].set(w[:, c_z:c_b]).at[:, LANES:LANES + DN_HEADS].set(w[:, c_b:c_a]).astype(BF16)
    lane_pad = lambda p: jnp.zeros((1, LANES), F32).at[0, :DN_HEADS].set(p.astype(F32))

    proj, ba = _inproj(h0, row(ln0_g), row(ln0_b), w_main, w_ba, seq=seq, t_len=t_len)
    ya = _pool_branch(proj, pool_w[0].astype(BF16), row(pool_scale[0]), w_br_pool[0].astype(BF16),
                      batch=batch, seq=seq, t_len=t_len)
    qkvn = _conv_qkv(proj, conv_w[0].astype(F32), batch=batch, seq=seq, t_len=t_len)
    o = _delta_rule(qkvn, ba, proj, lane_pad(-jnp.exp(a_log[0].astype(F32))), lane_pad(dt_bias[0]),
                    row(onorm_g[0]), batch=batch, t_len=t_len)
    h1 = _merge(o, ya, proj, h0, w_br_delta[0].astype(BF16), w_out[0].astype(BF16), row(ln0_g), row(ln0_b),
                row(ln1_g[0]), row(ln1_b[0]), batch=batch, seq=seq, t_len=t_len)
    h1 = h1.reshape(batch, t_len, d)

    idx, gate = _peer_query(h1, peer_wq[0].astype(BF16), peer_k1[0].astype(BF16), peer_k2[0].astype(BF16),
                            seq=seq, tq=LANES)
    idx = idx.reshape(N_SLOTS, n_tok).T
    gate = gate.reshape(N_SLOTS, n_tok).T
    ia = lax.shift_right_logical(idx, 7)
    ib = lax.bitwise_and(idx, N_KEYS - 1)

    tb = _pick_tile(seq, (512, 256, 128))
    act = _peer_hidden(h1, _expert_rows_ib_major(peer_u[0]), ia, ib, gate, seq=seq, tb=tb)
    out = _peer_output(act, ia, ib, _expert_rows_ib_major(peer_v[0]), h1, row(ln2_g[0]), row(ln2_b[0]),
                       seq=seq, tb=tb)
    return out.reshape(batch, seq, d)
```

```python
import functools
import math

import jax
import jax.numpy as jnp
from jax import lax
from jax.experimental import pallas as pl
from jax.experimental.pallas import tpu as pltpu

F32 = jnp.float32
BF16 = jnp.bfloat16
I32 = jnp.int32

D_MODEL = 1024
N_META = 16
CHUNK = 64
PAD = CHUNK - N_META
POOL_GROUPS = 4
POOL_GROUP_DIM = 128
POOL_WIDTH = POOL_GROUPS * POOL_GROUP_DIM
POOL_WINDOWS = (2, 4, 8, 16)
POOL_HALO = 16
DN_HEADS = 8
DN_HEAD_DIM = 128
DN_WIDTH = DN_HEADS * DN_HEAD_DIM
CONV_WIDTH = 4
PEER_HEADS = 8
PEER_HALF = 128
N_KEYS = 128
PEER_TOPK = 16
N_SLOTS = PEER_HEADS * PEER_TOPK
LN_EPS = 1e-5
RMS_EPS = 1e-6
L2_EPS = 1e-6
ALPHA = 2.0 ** 0.25

COL_QKV = 0
COL_Z = 3 * DN_WIDTH
COL_GP = COL_Z + DN_WIDTH
COL_GD = COL_GP + D_MODEL
COL_POOL = COL_GD + D_MODEL
N_PROJ = COL_POOL + POOL_WIDTH
BA_WIDTH = 256

SUBLANES = 8
LANES = 128
VMEM_LIMIT = 56 * 1024 * 1024

_CAND = [(a, b) for a in range(PEER_TOPK) for b in range(PEER_TOPK) if (a + 1) * (b + 1) <= PEER_TOPK]


def _params(sem):
    return pltpu.CompilerParams(dimension_semantics=sem, vmem_limit_bytes=VMEM_LIMIT)


def _layer_norm(x, g, b):
    mu = jnp.mean(x, axis=-1, keepdims=True)
    xc = x - mu
    var = jnp.mean(xc * xc, axis=-1, keepdims=True)
    return xc * lax.rsqrt(var + LN_EPS) * g + b


def _zero_pad_rows(y, row, seq):
    return jnp.where(row >= seq, jnp.where(row < seq + PAD, 0.0, y), y)


def _sigmoid(x):
    return 1.0 / (1.0 + jnp.exp(-x))


def _dot(a, b):
    return jnp.dot(a, b, preferred_element_type=F32)


def _dot_nt(a, b):
    return lax.dot_general(a, b, (((1,), (1,)), ((), ())), preferred_element_type=F32)


def _inproj_kernel(x_ref, g_ref, b_ref, w_ref, wba_ref, proj_ref, ba_ref, xn_ref, *, tiles_per_batch, seq):
    i = pl.program_id(0)
    j = pl.program_id(1)
    tm = x_ref.shape[0]

    @pl.when(j == 0)
    def _():
        y = _layer_norm(x_ref[...], g_ref[...], b_ref[...])
        row = lax.broadcasted_iota(I32, (tm, 1), 0) + lax.rem(i, tiles_per_batch) * tm
        xn = _zero_pad_rows(y, row, seq).astype(BF16)
        xn_ref[...] = xn
        ba_ref[...] = _dot(xn, wba_ref[...])

    proj_ref[...] = _dot(xn_ref[...], w_ref[...])


def _inproj(h0, ln_g, ln_b, w_main, w_ba, *, seq, t_len):
    rows = h0.shape[0]
    tiles_per_batch = 4
    tm = t_len // tiles_per_batch
    tn = N_PROJ // 4
    kern = functools.partial(_inproj_kernel, tiles_per_batch=tiles_per_batch, seq=seq)
    return pl.pallas_call(
        kern,
        grid=(rows // tm, N_PROJ // tn),
        in_specs=[
            pl.BlockSpec((tm, D_MODEL), lambda i, j: (i, 0)),
            pl.BlockSpec((1, D_MODEL), lambda i, j: (0, 0)),
            pl.BlockSpec((1, D_MODEL), lambda i, j: (0, 0)),
            pl.BlockSpec((D_MODEL, tn), lambda i, j: (0, j)),
            pl.BlockSpec((D_MODEL, BA_WIDTH), lambda i, j: (0, 0)),
        ],
        out_specs=[
            pl.BlockSpec((tm, tn), lambda i, j: (i, j)),
            pl.BlockSpec((tm, BA_WIDTH), lambda i, j: (i, 0)),
        ],
        out_shape=[
            jax.ShapeDtypeStruct((rows, N_PROJ), F32),
            jax.ShapeDtypeStruct((rows, BA_WIDTH), F32),
        ],
        scratch_shapes=[pltpu.VMEM((tm, D_MODEL), BF16)],
        compiler_params=_params(("arbitrary", "arbitrary")),
        name="inproj",
    )(h0, ln_g, ln_b, w_main, w_ba)


def _pool_kernel(xp_ref, halo_ref, pw_ref, ps_ref, wbp_ref, ya_ref, *, seq):
    i = pl.program_id(1)
    tm = xp_ref.shape[0]
    v = jnp.concatenate([halo_ref[...], xp_ref[...]], axis=0)
    row = lax.broadcasted_iota(I32, (tm, POOL_GROUP_DIM), 0) + i * tm
    meta_pos = row - (seq + PAD)
    outs = []
    for gi, w in enumerate(POOL_WINDOWS):
        vg = v[:, gi * POOL_GROUP_DIM:(gi + 1) * POOL_GROUP_DIM]
        s = vg
        shift = 1
        while shift < w:
            s = s + pltpu.roll(s, shift, axis=0)
            shift *= 2
        s = s[POOL_HALO:]
        xg = vg[POOL_HALO:]
        count = jnp.where(meta_pos >= 0, jnp.minimum(meta_pos + 1, w), w).astype(F32)
        pooled = s / count - xg
        mixed = _dot(pooled.astype(BF16), pw_ref[gi])
        outs.append(mixed)
    y_pool = jnp.concatenate(outs, axis=1) * ps_ref[...]
    ya_ref[...] = _dot(y_pool.astype(BF16), wbp_ref[...])


def _pool_branch(proj, pool_w, pool_scale, w_br_pool, *, batch, seq, t_len):
    rows = proj.shape[0]
    tiles_per_batch = 4
    tm = t_len // tiles_per_batch
    halo_blocks_per_batch = t_len // POOL_HALO
    halo_blocks_per_tile = tm // POOL_HALO
    pool_col = COL_POOL // POOL_WIDTH

    def halo_map(b, i):
        prev = lax.rem(i * halo_blocks_per_tile + halo_blocks_per_batch - 1, halo_blocks_per_batch)
        return (b * halo_blocks_per_batch + prev, pool_col)

    return pl.pallas_call(
        functools.partial(_pool_kernel, seq=seq),
        grid=(batch, tiles_per_batch),
        in_specs=[
            pl.BlockSpec((tm, POOL_WIDTH), lambda b, i: (b * tiles_per_batch + i, pool_col)),
            pl.BlockSpec((POOL_HALO, POOL_WIDTH), halo_map),
            pl.BlockSpec((POOL_GROUPS, POOL_GROUP_DIM, POOL_GROUP_DIM), lambda b, i: (0, 0, 0)),
            pl.BlockSpec((1, POOL_WIDTH), lambda b, i: (0, 0)),
            pl.BlockSpec((POOL_WIDTH, D_MODEL), lambda b, i: (0, 0)),
        ],
        out_specs=pl.BlockSpec((tm, D_MODEL), lambda b, i: (b * tiles_per_batch + i, 0)),
        out_shape=jax.ShapeDtypeStruct((rows, D_MODEL), F32),
        compiler_params=_params(("arbitrary", "arbitrary")),
        name="pool_branch",
    )(proj, proj, pool_w, pool_scale, w_br_pool)


def _conv_kernel(x_ref, w_ref, o_ref, *, seq):
    s = pl.program_id(1)
    x = x_ref[...]
    t_len = x.shape[0]
    w = w_ref[...]
    y = x * w[CONV_WIDTH - 1:CONV_WIDTH, :]
    for lag in range(1, CONV_WIDTH):
        y = y + pltpu.roll(x, lag, axis=0) * w[CONV_WIDTH - 1 - lag:CONV_WIDTH - lag, :]
    y = y * _sigmoid(y)
    ss = jnp.sum(y * y, axis=-1, keepdims=True)
    q_scale = jnp.where(s < DN_HEADS, DN_HEAD_DIM ** -0.5, 1.0).astype(F32)
    fac = jnp.where(s < 2 * DN_HEADS, lax.rsqrt(ss + L2_EPS) * q_scale, 1.0)
    row = lax.broadcasted_iota(I32, (t_len, 1), 0)
    o_ref[...] = _zero_pad_rows(y * fac, row, seq)


def _conv_qkv(proj, conv_w, *, batch, seq, t_len):
    rows = proj.shape[0]
    n_slabs = 3 * DN_HEADS
    return pl.pallas_call(
        functools.partial(_conv_kernel, seq=seq),
        grid=(batch, n_slabs),
        in_specs=[
            pl.BlockSpec((t_len, DN_HEAD_DIM), lambda b, s: (b, COL_QKV // DN_HEAD_DIM + s)),
            pl.BlockSpec((CONV_WIDTH, DN_HEAD_DIM), lambda b, s: (0, s)),
        ],
        out_specs=pl.BlockSpec((t_len, DN_HEAD_DIM), lambda b, s: (b, s)),
        out_shape=jax.ShapeDtypeStruct((rows, 3 * DN_WIDTH), F32),
        compiler_params=_params(("arbitrary", "arbitrary")),
        name="conv_qkv",
    )(proj, conv_w)


def _inv_unit_lower(a, masks, eye):
    m8, m16, m32, m64 = masks
    n = [x * m8 for x in a]
    n2 = [_dot(x, x) for x in n]
    n4 = [_dot(x, x) for x in n2]
    t = [eye - x for x in n]
    t = [x + _dot(x, y) for x, y in zip(t, n2)]
    t = [x + _dot(x, y) for x, y in zip(t, n4)]
    for m in (m16, m32, m64):
        at = [_dot(x * m, y) for x, y in zip(a, t)]
        t = [x - _dot(x, y) for x, y in zip(t, at)]
    return t


DELTA_STREAMS = 2


def _delta_kernel(q_ref, k_ref, v_ref, ba_ref, z_ref, nega_ref, dtb_ref, og_ref, o_ref, state_ref):
    n = DELTA_STREAMS
    c = pl.program_id(1)

    @pl.when(c == 0)
    def _():
        state_ref[...] = jnp.zeros_like(state_ref)

    r = lax.broadcasted_iota(I32, (CHUNK, CHUNK), 0)
    cc = lax.broadcasted_iota(I32, (CHUNK, CHUNK), 1)
    causal = r >= cc
    strict = r > cc
    eye = jnp.where(r == cc, 1.0, 0.0).astype(F32)
    tril = jnp.where(causal, 1.0, 0.0).astype(F32)

    def blk(x, s):
        return lax.shift_right_logical(x, s)

    lower = jnp.where(strict, 1.0, 0.0).astype(F32)
    m8 = jnp.where(blk(r, 3) == blk(cc, 3), lower, 0.0)
    m16 = jnp.where((blk(r, 4) == blk(cc, 4)) & (blk(r, 3) != blk(cc, 3)), lower, 0.0)
    m32 = jnp.where((blk(r, 5) == blk(cc, 5)) & (blk(r, 4) != blk(cc, 4)), lower, 0.0)
    m64 = jnp.where(blk(r, 5) != blk(cc, 5), lower, 0.0)
    masks = (m8, m16, m32, m64)

    beta_all, gcum, gcum_t = [], [], []
    for s in range(n):
        ba = ba_ref[s]
        beta_all.append(_sigmoid(ba[:, :LANES]))
        x = ba[:, LANES:] + dtb_ref[...]
        softplus = jnp.maximum(x, 0.0) + jnp.log(1.0 + jnp.exp(-jnp.abs(x)))
        g_all = nega_ref[...] * softplus
        gcum.append(lax.dot_general(tril, g_all, (((1,), (0,)), ((), ())), precision=lax.Precision.HIGHEST,
                                    preferred_element_type=F32))
        gcum_t.append(gcum[s].T)

    chains = [(s, h) for s in range(n) for h in range(DN_HEADS)]
    ids = range(len(chains))
    sl = [slice(h * DN_HEAD_DIM, (h + 1) * DN_HEAD_DIM) for _, h in chains]
    q = [q_ref[s, :, sl[i]] for i, (s, h) in enumerate(chains)]
    k = [k_ref[s, :, sl[i]] for i, (s, h) in enumerate(chains)]
    v = [v_ref[s, :, sl[i]] for i, (s, h) in enumerate(chains)]
    beta = [beta_all[s][:, h:h + 1] for s, h in chains]
    gc = [gcum[s][:, h:h + 1] for s, h in chains]
    gr = [gcum_t[s][h:h + 1, :] for s, h in chains]
    g_last = [gcum[s][CHUNK - 1:CHUNK, h:h + 1] for s, h in chains]
    decay = [jnp.where(causal, jnp.exp(jnp.where(causal, gc[i] - gr[i], 0.0)), 0.0) for i in ids]
    eg = [jnp.exp(gc[i]) for i in ids]
    kb = [k[i] * beta[i] for i in ids]
    kk = [_dot_nt(kb[i], k[i]) for i in ids]
    qk = [_dot_nt(q[i], k[i]) for i in ids]
    a_mat = [jnp.where(strict, kk[i] * decay[i], 0.0) for i in ids]
    t_inv = _inv_unit_lower(a_mat, masks, eye)
    rhs = [jnp.concatenate([v[i] * beta[i], kb[i] * eg[i]], axis=1) for i in ids]
    uw = [_dot(t_inv[i], rhs[i]) for i in ids]
    s_prev = [state_ref[i] for i in ids]
    wq = [jnp.concatenate([uw[i][:, DN_HEAD_DIM:], q[i] * eg[i]], axis=0) for i in ids]
    ws = [_dot(wq[i], s_prev[i]) for i in ids]
    v_new = [uw[i][:, :DN_HEAD_DIM] - ws[i][:CHUNK] for i in ids]
    attn = [qk[i] * decay[i] for i in ids]
    intra = [_dot(attn[i], v_new[i]) for i in ids]
    k_dec_t = [(k[i] * jnp.exp(g_last[i] - gc[i])).T for i in ids]
    kv = [_dot(k_dec_t[i], v_new[i]) for i in ids]
    for i, (s, h) in enumerate(chains):
        state_ref[i] = s_prev[i] * jnp.exp(g_last[i]) + kv[i]
        out = ws[i][CHUNK:] + intra[i]
        out = out * lax.rsqrt(jnp.mean(out * out, axis=-1, keepdims=True) + RMS_EPS) * og_ref[...]
        zh = z_ref[s, :, sl[i]]
        o_ref[s, :, sl[i]] = (out * (zh * _sigmoid(zh))).astype(o_ref.dtype)


def _delta_rule(qkvn, ba, proj, neg_a, dt_bias, onorm_g, *, batch, t_len):
    rows = qkvn.shape[0]
    n_chunks = t_len // CHUNK

    n = DELTA_STREAMS
    assert batch % n == 0
    groups = batch // n
    by_stream = lambda a: a.reshape(groups, n, t_len, a.shape[-1])

    def chunk_block(col):
        return lambda g, c: (g, 0, lax.rem(c + n_chunks - 1, n_chunks), col)

    const = lambda g, c: (0, 0)
    out = pl.pallas_call(
        _delta_kernel,
        grid=(groups, n_chunks),
        in_specs=[
            pl.BlockSpec((None, n, CHUNK, DN_WIDTH), chunk_block(0)),
            pl.BlockSpec((None, n, CHUNK, DN_WIDTH), chunk_block(1)),
            pl.BlockSpec((None, n, CHUNK, DN_WIDTH), chunk_block(2)),
            pl.BlockSpec((None, n, CHUNK, BA_WIDTH), chunk_block(0)),
            pl.BlockSpec((None, n, CHUNK, DN_WIDTH), chunk_block(COL_Z // DN_WIDTH)),
            pl.BlockSpec((1, LANES), const),
            pl.BlockSpec((1, LANES), const),
            pl.BlockSpec((1, DN_HEAD_DIM), const),
        ],
        out_specs=pl.BlockSpec((None, n, CHUNK, DN_WIDTH), chunk_block(0)),
        out_shape=jax.ShapeDtypeStruct((groups, n, t_len, DN_WIDTH), BF16),
        scratch_shapes=[pltpu.VMEM((n * DN_HEADS, DN_HEAD_DIM, DN_HEAD_DIM), F32)],
        compiler_params=_params(("arbitrary", "arbitrary")),
        name="delta_rule",
    )(by_stream(qkvn), by_stream(qkvn), by_stream(qkvn), by_stream(ba), by_stream(proj), neg_a, dt_bias, onorm_g)
    return out.reshape(rows, DN_WIDTH)


def _merge_kernel(o_ref, ya_ref, gp_ref, gd_ref, h0_ref, wd_ref, wo_ref, g0_ref, b0_ref, g1_ref, b1_ref, h1_ref):
    y_b = _dot(o_ref[...], wd_ref[...])
    merged = _sigmoid(gp_ref[...]) * ya_ref[...] + _sigmoid(gd_ref[...]) * y_b
    mix = _dot(merged.astype(BF16), wo_ref[...])
    h = _layer_norm(h0_ref[...], g0_ref[...], b0_ref[...])
    h1_ref[...] = _layer_norm(ALPHA * h + mix, g1_ref[...], b1_ref[...])


def _merge(o, ya, proj, h0, w_br_delta, w_out, g0, b0, g1, b1, *, batch, seq, t_len):
    bf16_rows = 2 * SUBLANES
    tiles_per_batch = next(n for n in (10, 8, 4, 2, 1) if t_len % (n * bf16_rows) == 0)
    tm = t_len // tiles_per_batch
    row = lambda i: (i, 0)
    const = lambda i: (0, 0)
    return pl.pallas_call(
        _merge_kernel,
        grid=(batch * tiles_per_batch,),
        in_specs=[
            pl.BlockSpec((tm, DN_WIDTH), row),
            pl.BlockSpec((tm, D_MODEL), row),
            pl.BlockSpec((tm, D_MODEL), lambda i: (i, COL_GP // D_MODEL)),
            pl.BlockSpec((tm, D_MODEL), lambda i: (i, COL_GD // D_MODEL)),
            pl.BlockSpec((tm, D_MODEL), row),
            pl.BlockSpec((DN_WIDTH, D_MODEL), const),
            pl.BlockSpec((D_MODEL, D_MODEL), const),
            pl.BlockSpec((1, D_MODEL), const),
            pl.BlockSpec((1, D_MODEL), const),
            pl.BlockSpec((1, D_MODEL), const),
            pl.BlockSpec((1, D_MODEL), const),
        ],
        out_specs=pl.BlockSpec((tm, D_MODEL), row),
        out_shape=jax.ShapeDtypeStruct((batch * t_len, D_MODEL), F32),
        compiler_params=_params(("arbitrary",)),
        name="merge",
    )(o, ya, proj, proj, h0, w_br_delta, w_out, g0, b0, g1, b1)


def _tile(ref, k):
    return ref[k * SUBLANES:(k + 1) * SUBLANES, :]


def _argmax_tournament(vals, ids):
    nodes = list(zip(vals, ids))
    while len(nodes) > 1:
        nxt = []
        for p in range(0, len(nodes) - 1, 2):
            (va, ia), (vb, ib) = nodes[p], nodes[p + 1]
            first = va >= vb
            nxt.append((jnp.maximum(va, vb), jnp.where(first, ia, ib)))
        if len(nodes) % 2:
            nxt.append(nodes[-1])
        nodes = nxt
    return nodes[0]


def _query_kernel(x_ref, wq_ref, k1_ref, k2_ref, idx_ref, gate_ref, s1_ref, s2_ref, m1_ref, i1_ref, m2_ref, i2_ref,
                  c_ref, e_ref, sc_ref):
    xb = x_ref[...].astype(BF16)
    q = _dot(xb, wq_ref[...])
    for h in range(PEER_HEADS):
        q1 = q[:, (2 * h) * PEER_HALF:(2 * h + 1) * PEER_HALF].astype(BF16)
        q2 = q[:, (2 * h + 1) * PEER_HALF:(2 * h + 2) * PEER_HALF].astype(BF16)
        s1_ref[pl.ds(h, N_KEYS, stride=SUBLANES), :] = _dot_nt(k1_ref[h], q1)
        s2_ref[pl.ds(h, N_KEYS, stride=SUBLANES), :] = _dot_nt(k2_ref[h], q2)

    neg_inf = jnp.float32(-jnp.inf)

    def sub_round(r, carry):
        for s_ref, m_ref, i_ref in ((s1_ref, m1_ref, i1_ref), (s2_ref, m2_ref, i2_ref)):
            vals = [_tile(s_ref, k) for k in range(N_KEYS)]
            best, arg = _argmax_tournament(vals, list(range(N_KEYS)))
            m_ref[r] = best
            i_ref[r] = arg
            for k in range(N_KEYS):
                s_ref[k * SUBLANES:(k + 1) * SUBLANES, :] = jnp.where(arg == k, neg_inf, vals[k])
        return carry

    lax.fori_loop(0, PEER_TOPK, sub_round, 0)

    for j, (a, b) in enumerate(_CAND):
        c_ref[j * SUBLANES:(j + 1) * SUBLANES, :] = m1_ref[a] + m2_ref[b]
        e_ref[j * SUBLANES:(j + 1) * SUBLANES, :] = i1_ref[a] * N_KEYS + i2_ref[b]

    def pair_round(r, carry):
        vals = [_tile(c_ref, j) for j in range(len(_CAND))]
        ids = [_tile(e_ref, j) for j in range(len(_CAND))]
        best, arg = _argmax_tournament(vals, ids)
        sc_ref[r] = best
        idx_ref[r] = arg
        for j in range(len(_CAND)):
            c_ref[j * SUBLANES:(j + 1) * SUBLANES, :] = jnp.where(ids[j] == arg, neg_inf, vals[j])
        return carry

    lax.fori_loop(0, PEER_TOPK, pair_round, 0)

    sc = sc_ref[...]
    ex = jnp.exp(sc - sc[0:1])
    gate_ref[...] = ex / jnp.sum(ex, axis=0, keepdims=True)


def _token_tile_map(seq, tile):
    per_batch = seq // tile
    return lambda i, *_: (i // per_batch, lax.rem(i, per_batch), 0)


def _peer_query(h1, wq, k1, k2, *, seq, tq):
    n_tok = h1.shape[0] * seq
    slot_shape = (PEER_TOPK, PEER_HEADS, tq)
    return pl.pallas_call(
        _query_kernel,
        grid=(n_tok // tq,),
        in_specs=[
            pl.BlockSpec((None, tq, D_MODEL), _token_tile_map(seq, tq)),
            pl.BlockSpec((D_MODEL, 2 * PEER_HALF * PEER_HEADS), lambda i: (0, 0)),
            pl.BlockSpec((PEER_HEADS, N_KEYS, PEER_HALF), lambda i: (0, 0, 0)),
            pl.BlockSpec((PEER_HEADS, N_KEYS, PEER_HALF), lambda i: (0, 0, 0)),
        ],
        out_specs=[
            pl.BlockSpec(slot_shape, lambda i: (0, 0, i)),
            pl.BlockSpec(slot_shape, lambda i: (0, 0, i)),
        ],
        out_shape=[
            jax.ShapeDtypeStruct((PEER_TOPK, PEER_HEADS, n_tok), I32),
            jax.ShapeDtypeStruct((PEER_TOPK, PEER_HEADS, n_tok), F32),
        ],
        scratch_shapes=[
            pltpu.VMEM((N_KEYS * SUBLANES, tq), F32),
            pltpu.VMEM((N_KEYS * SUBLANES, tq), F32),
            pltpu.VMEM(slot_shape, F32),
            pltpu.VMEM(slot_shape, I32),
            pltpu.VMEM(slot_shape, F32),
            pltpu.VMEM(slot_shape, I32),
            pltpu.VMEM((len(_CAND) * SUBLANES, tq), F32),
            pltpu.VMEM((len(_CAND) * SUBLANES, tq), I32),
            pltpu.VMEM(slot_shape, F32),
        ],
        compiler_params=_params(("arbitrary",)),
        name="peer_query",
    )(h1, wq, k1, k2)


IA_PER_STEP = 8
EXPERTS_PER_STEP = IA_PER_STEP * N_KEYS
PEER_STEPS = (N_KEYS * N_KEYS) // EXPERTS_PER_STEP
HIDDEN_CHUNKS = 2
OUTPUT_CHUNKS = 4


def _one_hot_rows(idx_row):
    iota = lax.broadcasted_iota(I32, (N_KEYS, N_SLOTS), 0)
    return jnp.where(iota == idx_row, 1.0, 0.0).astype(BF16)


def _hidden_kernel(x_ref, u_ref, ia_ref, ib_ref, gate_ref, act_ref, xb_ref, h0_ref, h1_ref, hs_ref, *, tb, hp):
    i = pl.program_id(0)
    j = pl.program_id(1)
    group = tb // PEER_STEPS // HIDDEN_CHUNKS
    ia_per_chunk = IA_PER_STEP // HIDDEN_CHUNKS

    @pl.when(jnp.logical_and(i == 0, j == 0))
    def _():
        h1_ref[...] = jnp.zeros_like(h1_ref)

    @pl.when(j == 0)
    def _():
        xb_ref[...] = x_ref[...].astype(BF16)

    def step(fill_ref, drain_ref):
        iota = lax.broadcasted_iota(I32, (N_KEYS, N_SLOTS), 0)
        for c in range(HIDDEN_CHUNKS):
            base = pl.multiple_of((j * HIDDEN_CHUNKS + c) * group, group)
            ia_rows = ia_ref[pl.ds(base, group), :]
            ib_rows = ib_ref[pl.ds(base, group), :]
            lhs, sel = [], []
            for u in range(group):
                sel.append(_one_hot_rows(ib_rows[u:u + 1, :]))
                start = pl.multiple_of((base + u) * hp, SUBLANES)
                lhs.append(drain_ref[pl.ds(start, N_KEYS), :].astype(BF16))
            by_slot = [_dot(lhs[u], sel[u]) for u in range(group)]
            rows = []
            for u in range(group):
                picked = jnp.where(iota == ia_rows[u:u + 1, :], by_slot[u], 0.0)
                rows.append(jnp.sum(picked, axis=0, keepdims=True))
            hs_ref[pl.ds(base, group), :] = jnp.concatenate(rows, axis=0)

            experts = u_ref[c * ia_per_chunk * N_KEYS:(c + 1) * ia_per_chunk * N_KEYS, :]
            hc = _dot_nt(xb_ref[...], experts)
            for t in range(ia_per_chunk):
                ia = j * IA_PER_STEP + c * ia_per_chunk + t
                fill_ref[pl.ds(ia, tb, stride=hp), :] = hc[:, t * N_KEYS:(t + 1) * N_KEYS]

    parity = lax.rem(i, 2)

    @pl.when(parity == 0)
    def _():
        step(h0_ref, h1_ref)

    @pl.when(parity == 1)
    def _():
        step(h1_ref, h0_ref)

    @pl.when(j == PEER_STEPS - 1)
    def _():
        hid = hs_ref[...]
        gelu = 0.5 * hid * (1.0 + lax.erf(hid * (2.0 ** -0.5)))
        act_ref[...] = gelu * gate_ref[...]


def _peer_hidden(h1, u_tab, ia, ib, gate, *, seq, tb):
    n_tiles = h1.shape[0] * seq // tb
    hp = N_KEYS + SUBLANES
    x_map = _token_tile_map(seq, tb)
    ahead = lambda i, j: x_map(jnp.minimum(i, n_tiles - 1))
    behind = lambda i, j: (jnp.maximum(i - 1, 0), 0)
    return pl.pallas_call(
        functools.partial(_hidden_kernel, tb=tb, hp=hp),
        grid=(n_tiles + 1, PEER_STEPS),
        in_specs=[
            pl.BlockSpec((None, tb, D_MODEL), ahead),
            pl.BlockSpec((EXPERTS_PER_STEP, D_MODEL), lambda i, j: (j, 0)),
            pl.BlockSpec((tb, N_SLOTS), behind),
            pl.BlockSpec((tb, N_SLOTS), behind),
            pl.BlockSpec((tb, N_SLOTS), behind),
        ],
        out_specs=pl.BlockSpec((tb, N_SLOTS), behind),
        out_shape=jax.ShapeDtypeStruct((n_tiles * tb, N_SLOTS), F32),
        scratch_shapes=[
            pltpu.VMEM((tb, D_MODEL), BF16),
            pltpu.VMEM((tb * hp, N_KEYS), F32),
            pltpu.VMEM((tb * hp, N_KEYS), F32),
            pltpu.VMEM((tb, N_SLOTS), F32),
        ],
        compiler_params=_params(("arbitrary", "arbitrary")),
        name="peer_hidden",
    )(h1, u_tab, ia, ib, gate)


def _output_kernel(act_ref, ia_ref, ib_ref, v_ref, h1_ref, g_ref, b_ref, out_ref, a0_ref, a1_ref, acc_ref, *, tb, hp):
    i = pl.program_id(0)
    j = pl.program_id(1)
    group = tb // PEER_STEPS // OUTPUT_CHUNKS
    ia_per_chunk = IA_PER_STEP // OUTPUT_CHUNKS

    @pl.when(jnp.logical_and(i == 0, j == 0))
    def _():
        a1_ref[...] = jnp.zeros_like(a1_ref)

    @pl.when(j == 0)
    def _():
        acc_ref[...] = jnp.zeros_like(acc_ref)

    def step(fill_ref, drain_ref):
        iota = lax.broadcasted_iota(I32, (N_KEYS, N_SLOTS), 0)
        acc = acc_ref[...]
        for c in range(OUTPUT_CHUNKS):
            base = pl.multiple_of((j * OUTPUT_CHUNKS + c) * group, group)
            ia_rows = ia_ref[pl.ds(base, group), :]
            ib_rows = ib_ref[pl.ds(base, group), :]
            act_rows = act_ref[pl.ds(base, group), :]
            sel, act_a = [], []
            for u in range(group):
                sel.append(_one_hot_rows(ib_rows[u:u + 1, :]))
                act_a.append(jnp.where(iota == ia_rows[u:u + 1, :], act_rows[u:u + 1, :], 0.0).astype(BF16))
            dense = [_dot_nt(act_a[u], sel[u]) for u in range(group)]
            pair_base = (j * OUTPUT_CHUNKS + c) * (group // 2)
            for p in range(group // 2):
                packed = pltpu.pack_elementwise([dense[2 * p], dense[2 * p + 1]], packed_dtype=BF16)
                fill_ref[pl.ds(pair_base + p, N_KEYS, stride=hp), :] = lax.bitcast_convert_type(packed, I32)

            tiles = []
            for t in range(ia_per_chunk):
                ia = j * IA_PER_STEP + c * ia_per_chunk + t
                start = pl.multiple_of(ia * hp, SUBLANES)
                tiles.append(pltpu.bitcast(drain_ref[pl.ds(start, tb // 2), :], BF16))
            rows = slice(c * ia_per_chunk * N_KEYS, (c + 1) * ia_per_chunk * N_KEYS)
            acc = acc + _dot(jnp.concatenate(tiles, axis=1), v_ref[rows, :])
        acc_ref[...] = acc

    parity = lax.rem(i, 2)

    @pl.when(parity == 0)
    def _():
        step(a0_ref, a1_ref)

    @pl.when(parity == 1)
    def _():
        step(a1_ref, a0_ref)

    @pl.when(j == PEER_STEPS - 1)
    def _():
        out_ref[...] = _layer_norm(ALPHA * h1_ref[...] + acc_ref[...], g_ref[...], b_ref[...])


def _peer_output(act, ia, ib, v_tab, h1, g2, b2, *, seq, tb):
    n_tiles = h1.shape[0] * seq // tb
    hp = tb // 2 + SUBLANES
    x_map = _token_tile_map(seq, tb)
    ahead = lambda i, j: (jnp.minimum(i, n_tiles - 1), 0)
    behind = lambda i, j: (jnp.maximum(i - 1, 0), 0)
    const = lambda i, j: (0, 0)
    return pl.pallas_call(
        functools.partial(_output_kernel, tb=tb, hp=hp),
        grid=(n_tiles + 1, PEER_STEPS),
        in_specs=[
            pl.BlockSpec((tb, N_SLOTS), ahead),
            pl.BlockSpec((tb, N_SLOTS), ahead),
            pl.BlockSpec((tb, N_SLOTS), ahead),
            pl.BlockSpec((EXPERTS_PER_STEP, D_MODEL), lambda i, j: (j, 0)),
            pl.BlockSpec((None, tb, D_MODEL), lambda i, j: x_map(jnp.maximum(i - 1, 0))),
            pl.BlockSpec((1, D_MODEL), const),
            pl.BlockSpec((1, D_MODEL), const),
        ],
        out_specs=pl.BlockSpec((tb, D_MODEL), behind),
        out_shape=jax.ShapeDtypeStruct((n_tiles * tb, D_MODEL), F32),
        scratch_shapes=[
            pltpu.VMEM((N_KEYS * hp, N_KEYS), I32),
            pltpu.VMEM((N_KEYS * hp, N_KEYS), I32),
            pltpu.VMEM((tb, D_MODEL), F32),
        ],
        compiler_params=_params(("arbitrary", "arbitrary")),
        name="peer_output",
    )(act, ia, ib, v_tab, h1, g2, b2)


def _pick_tile(n, candidates):
    for c in candidates:
        if n % c == 0:
            return c
    raise ValueError(f"no tile in {candidates} divides {n}")


def kernel(x, meta, ln0_g, ln0_b, w_in, pool_w, pool_scale, w_br_pool, conv_w, a_log, dt_bias, onorm_g, w_br_delta, w_out, ln1_g, ln1_b, peer_wq, peer_k1, peer_k2, peer_u, peer_v, ln2_g, ln2_b):
    batch, seq, d = x.shape
    assert d == D_MODEL and seq % CHUNK == 0 and w_in.shape[0] == 1
    t_len = seq + CHUNK
    n_tok = batch * seq
    row = lambda p: p.reshape(1, -1).astype(F32)

    meta_b = jnp.broadcast_to(meta[None].astype(x.dtype), (batch, N_META, d))
    h0 = jnp.concatenate([x, jnp.zeros((batch, PAD, d), x.dtype), meta_b], axis=1).reshape(batch * t_len, d)

    w = w_in[0]
    c_pool, c_qkv, c_z = POOL_WIDTH, POOL_WIDTH + 3 * DN_WIDTH, POOL_WIDTH + 4 * DN_WIDTH
    c_b, c_a = c_z + DN_HEADS, c_z + 2 * DN_HEADS
    w_main = jnp.concatenate([w[:, c_pool:c_z], w[:, c_a:], w[:, :c_pool]], axis=1).astype(BF16)
    w_ba = jnp.zeros((d, BA_WIDTH), F32)
    w_ba = w_ba.at[:, :DN_HEADS].set(w[:, c_z:c_b]).at[:, LANES:LANES + DN_HEADS].set(w[:, c_b:c_a]).astype(BF16)
    lane_pad = lambda p: jnp.zeros((1, LANES), F32).at[0, :DN_HEADS].set(p.astype(F32))

    proj, ba = _inproj(h0, row(ln0_g), row(ln0_b), w_main, w_ba, seq=seq, t_len=t_len)
    ya = _pool_branch(proj, pool_w[0].astype(BF16), row(pool_scale[0]), w_br_pool[0].astype(BF16),
                      batch=batch, seq=seq, t_len=t_len)
    qkvn = _conv_qkv(proj, conv_w[0].astype(F32), batch=batch, seq=seq, t_len=t_len)
    o = _delta_rule(qkvn, ba, proj, lane_pad(-jnp.exp(a_log[0].astype(F32))), lane_pad(dt_bias[0]),
                    row(onorm_g[0]), batch=batch, t_len=t_len)
    h1 = _merge(o, ya, proj, h0, w_br_delta[0].astype(BF16), w_out[0].astype(BF16), row(ln0_g), row(ln0_b),
                row(ln1_g[0]), row(ln1_b[0]), batch=batch, seq=seq, t_len=t_len)
    h1 = h1.reshape(batch, t_len, d)

    idx, gate = _peer_query(h1, peer_wq[0].astype(BF16), peer_k1[0].astype(BF16), peer_k2[0].astype(BF16),
                            seq=seq, tq=LANES)
    idx = idx.reshape(N_SLOTS, n_tok).T
    gate = gate.reshape(N_SLOTS, n_tok).T
    ia = lax.shift_right_logical(idx, 7)
    ib = lax.bitwise_and(idx, N_KEYS - 1)

    act = _peer_hidden(h1, peer_u[0].astype(BF16), ia, ib, gate, seq=seq, tb=_pick_tile(seq, (256,)))
    out = _peer_output(act, ia, ib, peer_v[0].astype(BF16), h1, row(ln2_g[0]), row(ln2_b[0]),
                       seq=seq, tb=_pick_tile(seq, (512,)))
    return out.reshape(batch, seq, d)
```

```python
import functools
import math

import jax
import jax.numpy as jnp
from jax import lax
from jax.experimental import pallas as pl
from jax.experimental.pallas import tpu as pltpu

F32 = jnp.float32
BF16 = jnp.bfloat16
I32 = jnp.int32

D_MODEL = 1024
N_META = 16
CHUNK = 64
PAD = CHUNK - N_META
POOL_GROUPS = 4
POOL_GROUP_DIM = 128
POOL_WIDTH = POOL_GROUPS * POOL_GROUP_DIM
POOL_WINDOWS = (2, 4, 8, 16)
POOL_HALO = 16
DN_HEADS = 8
DN_HEAD_DIM = 128
DN_WIDTH = DN_HEADS * DN_HEAD_DIM
CONV_WIDTH = 4
PEER_HEADS = 8
PEER_HALF = 128
N_KEYS = 128
PEER_TOPK = 16
N_SLOTS = PEER_HEADS * PEER_TOPK
LN_EPS = 1e-5
RMS_EPS = 1e-6
L2_EPS = 1e-6
ALPHA = 2.0 ** 0.25

COL_QKV = 0
COL_Z = 3 * DN_WIDTH
COL_GP = COL_Z + DN_WIDTH
COL_GD = COL_GP + D_MODEL
COL_POOL = COL_GD + D_MODEL
N_PROJ = COL_POOL + POOL_WIDTH
BA_WIDTH = 256

SUBLANES = 8
LANES = 128
VMEM_LIMIT = 56 * 1024 * 1024

_CAND = [(a, b) for a in range(PEER_TOPK) for b in range(PEER_TOPK) if (a + 1) * (b + 1) <= PEER_TOPK]


def _params(sem):
    return pltpu.CompilerParams(dimension_semantics=sem, vmem_limit_bytes=VMEM_LIMIT)


def _layer_norm(x, g, b):
    mu = jnp.mean(x, axis=-1, keepdims=True)
    xc = x - mu
    var = jnp.mean(xc * xc, axis=-1, keepdims=True)
    return xc * lax.rsqrt(var + LN_EPS) * g + b


def _zero_pad_rows(y, row, seq):
    return jnp.where(row >= seq, jnp.where(row < seq + PAD, 0.0, y), y)


def _sigmoid(x):
    return 1.0 / (1.0 + jnp.exp(-x))


def _dot(a, b):
    return jnp.dot(a, b, preferred_element_type=F32)


def _dot_nt(a, b):
    return lax.dot_general(a, b, (((1,), (1,)), ((), ())), preferred_element_type=F32)


def _inproj_kernel(x_ref, g_ref, b_ref, w_ref, wba_ref, proj_ref, ba_ref, xn_ref, *, tiles_per_batch, seq):
    i = pl.program_id(0)
    j = pl.program_id(1)
    tm = x_ref.shape[0]

    @pl.when(j == 0)
    def _():
        y = _layer_norm(x_ref[...], g_ref[...], b_ref[...])
        row = lax.broadcasted_iota(I32, (tm, 1), 0) + lax.rem(i, tiles_per_batch) * tm
        xn = _zero_pad_rows(y, row, seq).astype(BF16)
        xn_ref[...] = xn
        ba_ref[...] = _dot(xn, wba_ref[...])

    proj_ref[...] = _dot(xn_ref[...], w_ref[...]).astype(proj_ref.dtype)


def _inproj(h0, ln_g, ln_b, w_main, w_ba, *, seq, t_len):
    rows = h0.shape[0]
    tiles_per_batch = 4
    tm = t_len // tiles_per_batch
    tn = N_PROJ // 4
    kern = functools.partial(_inproj_kernel, tiles_per_batch=tiles_per_batch, seq=seq)
    return pl.pallas_call(
        kern,
        grid=(rows // tm, N_PROJ // tn),
        in_specs=[
            pl.BlockSpec((tm, D_MODEL), lambda i, j: (i, 0)),
            pl.BlockSpec((1, D_MODEL), lambda i, j: (0, 0)),
            pl.BlockSpec((1, D_MODEL), lambda i, j: (0, 0)),
            pl.BlockSpec((D_MODEL, tn), lambda i, j: (0, j)),
            pl.BlockSpec((D_MODEL, BA_WIDTH), lambda i, j: (0, 0)),
        ],
        out_specs=[
            pl.BlockSpec((tm, tn), lambda i, j: (i, j)),
            pl.BlockSpec((tm, BA_WIDTH), lambda i, j: (i, 0)),
        ],
        out_shape=[
            jax.ShapeDtypeStruct((rows, N_PROJ), BF16),
            jax.ShapeDtypeStruct((rows, BA_WIDTH), F32),
        ],
        scratch_shapes=[pltpu.VMEM((tm, D_MODEL), BF16)],
        compiler_params=_params(("arbitrary", "arbitrary")),
        name="inproj",
    )(h0, ln_g, ln_b, w_main, w_ba)


def _pool_kernel(xp_ref, halo_ref, pw_ref, ps_ref, wbp_ref, ya_ref, *, seq):
    i = pl.program_id(1)
    tm = xp_ref.shape[0]
    v = jnp.concatenate([halo_ref[...].astype(F32), xp_ref[...].astype(F32)], axis=0)
    row = lax.broadcasted_iota(I32, (tm, POOL_GROUP_DIM), 0) + i * tm
    meta_pos = row - (seq + PAD)
    outs = []
    for gi, w in enumerate(POOL_WINDOWS):
        vg = v[:, gi * POOL_GROUP_DIM:(gi + 1) * POOL_GROUP_DIM]
        s = vg
        shift = 1
        while shift < w:
            s = s + pltpu.roll(s, shift, axis=0)
            shift *= 2
        s = s[POOL_HALO:]
        xg = vg[POOL_HALO:]
        count = jnp.where(meta_pos >= 0, jnp.minimum(meta_pos + 1, w), w).astype(F32)
        pooled = s / count - xg
        mixed = _dot(pooled.astype(BF16), pw_ref[gi])
        outs.append(mixed)
    y_pool = jnp.concatenate(outs, axis=1) * ps_ref[...]
    ya_ref[...] = _dot(y_pool.astype(BF16), wbp_ref[...])


def _pool_branch(proj, pool_w, pool_scale, w_br_pool, *, batch, seq, t_len):
    rows = proj.shape[0]
    tiles_per_batch = 4
    tm = t_len // tiles_per_batch
    halo_blocks_per_batch = t_len // POOL_HALO
    halo_blocks_per_tile = tm // POOL_HALO
    pool_col = COL_POOL // POOL_WIDTH

    def halo_map(b, i):
        prev = lax.rem(i * halo_blocks_per_tile + halo_blocks_per_batch - 1, halo_blocks_per_batch)
        return (b * halo_blocks_per_batch + prev, pool_col)

    return pl.pallas_call(
        functools.partial(_pool_kernel, seq=seq),
        grid=(batch, tiles_per_batch),
        in_specs=[
            pl.BlockSpec((tm, POOL_WIDTH), lambda b, i: (b * tiles_per_batch + i, pool_col)),
            pl.BlockSpec((POOL_HALO, POOL_WIDTH), halo_map),
            pl.BlockSpec((POOL_GROUPS, POOL_GROUP_DIM, POOL_GROUP_DIM), lambda b, i: (0, 0, 0)),
            pl.BlockSpec((1, POOL_WIDTH), lambda b, i: (0, 0)),
            pl.BlockSpec((POOL_WIDTH, D_MODEL), lambda b, i: (0, 0)),
        ],
        out_specs=pl.BlockSpec((tm, D_MODEL), lambda b, i: (b * tiles_per_batch + i, 0)),
        out_shape=jax.ShapeDtypeStruct((rows, D_MODEL), F32),
        compiler_params=_params(("arbitrary", "arbitrary")),
        name="pool_branch",
    )(proj, proj, pool_w, pool_scale, w_br_pool)


def _conv_kernel(x_ref, w_ref, o_ref, *, seq):
    s = pl.program_id(1)
    x = x_ref[...].astype(F32)
    t_len = x.shape[0]
    w = w_ref[...]
    y = x * w[CONV_WIDTH - 1:CONV_WIDTH, :]
    for lag in range(1, CONV_WIDTH):
        y = y + pltpu.roll(x, lag, axis=0) * w[CONV_WIDTH - 1 - lag:CONV_WIDTH - lag, :]
    y = y * _sigmoid(y)
    ss = jnp.sum(y * y, axis=-1, keepdims=True)
    q_scale = jnp.where(s < DN_HEADS, DN_HEAD_DIM ** -0.5, 1.0).astype(F32)
    fac = jnp.where(s < 2 * DN_HEADS, lax.rsqrt(ss + L2_EPS) * q_scale, 1.0)
    row = lax.broadcasted_iota(I32, (t_len, 1), 0)
    o_ref[...] = _zero_pad_rows(y * fac, row, seq).astype(o_ref.dtype)


def _conv_qkv(proj, conv_w, *, batch, seq, t_len):
    rows = proj.shape[0]
    n_slabs = 3 * DN_HEADS
    return pl.pallas_call(
        functools.partial(_conv_kernel, seq=seq),
        grid=(batch, n_slabs),
        in_specs=[
            pl.BlockSpec((t_len, DN_HEAD_DIM), lambda b, s: (b, COL_QKV // DN_HEAD_DIM + s)),
            pl.BlockSpec((CONV_WIDTH, DN_HEAD_DIM), lambda b, s: (0, s)),
        ],
        out_specs=pl.BlockSpec((t_len, DN_HEAD_DIM), lambda b, s: (b, s)),
        out_shape=jax.ShapeDtypeStruct((rows, 3 * DN_WIDTH), BF16),
        compiler_params=_params(("arbitrary", "arbitrary")),
        name="conv_qkv",
    )(proj, conv_w)


def _inv_unit_lower(a, masks, eye):
    m8, m16, m32, m64 = masks
    n = [x * m8 for x in a]
    n2 = [_dot(x, x) for x in n]
    n4 = [_dot(x, x) for x in n2]
    t = [eye - x for x in n]
    t = [x + _dot(x, y) for x, y in zip(t, n2)]
    t = [x + _dot(x, y) for x, y in zip(t, n4)]
    for m in (m16, m32, m64):
        at = [_dot(x * m, y) for x, y in zip(a, t)]
        t = [x - _dot(x, y) for x, y in zip(t, at)]
    return t


DELTA_STREAMS = 2


def _delta_kernel(q_ref, k_ref, v_ref, ba_ref, z_ref, nega_ref, dtb_ref, og_ref, o_ref, state_ref):
    n = DELTA_STREAMS
    c = pl.program_id(1)

    @pl.when(c == 0)
    def _():
        state_ref[...] = jnp.zeros_like(state_ref)

    r = lax.broadcasted_iota(I32, (CHUNK, CHUNK), 0)
    cc = lax.broadcasted_iota(I32, (CHUNK, CHUNK), 1)
    causal = r >= cc
    strict = r > cc
    eye = jnp.where(r == cc, 1.0, 0.0).astype(F32)
    tril = jnp.where(causal, 1.0, 0.0).astype(F32)

    def blk(x, s):
        return lax.shift_right_logical(x, s)

    lower = jnp.where(strict, 1.0, 0.0).astype(F32)
    m8 = jnp.where(blk(r, 3) == blk(cc, 3), lower, 0.0)
    m16 = jnp.where((blk(r, 4) == blk(cc, 4)) & (blk(r, 3) != blk(cc, 3)), lower, 0.0)
    m32 = jnp.where((blk(r, 5) == blk(cc, 5)) & (blk(r, 4) != blk(cc, 4)), lower, 0.0)
    m64 = jnp.where(blk(r, 5) != blk(cc, 5), lower, 0.0)
    masks = (m8, m16, m32, m64)

    beta_all, gcum, gcum_t = [], [], []
    for s in range(n):
        ba = ba_ref[s]
        beta_all.append(_sigmoid(ba[:, :LANES]))
        x = ba[:, LANES:] + dtb_ref[...]
        softplus = jnp.maximum(x, 0.0) + jnp.log(1.0 + jnp.exp(-jnp.abs(x)))
        g_all = nega_ref[...] * softplus
        gcum.append(lax.dot_general(tril, g_all, (((1,), (0,)), ((), ())), precision=lax.Precision.HIGHEST,
                                    preferred_element_type=F32))
        gcum_t.append(gcum[s].T)

    chains = [(s, h) for s in range(n) for h in range(DN_HEADS)]
    ids = range(len(chains))
    sl = [slice(h * DN_HEAD_DIM, (h + 1) * DN_HEAD_DIM) for _, h in chains]
    q = [q_ref[s, :, sl[i]].astype(F32) for i, (s, h) in enumerate(chains)]
    k = [k_ref[s, :, sl[i]].astype(F32) for i, (s, h) in enumerate(chains)]
    v = [v_ref[s, :, sl[i]].astype(F32) for i, (s, h) in enumerate(chains)]
    beta = [beta_all[s][:, h:h + 1] for s, h in chains]
    gc = [gcum[s][:, h:h + 1] for s, h in chains]
    gr = [gcum_t[s][h:h + 1, :] for s, h in chains]
    g_last = [gcum[s][CHUNK - 1:CHUNK, h:h + 1] for s, h in chains]
    decay = [jnp.where(causal, jnp.exp(jnp.where(causal, gc[i] - gr[i], 0.0)), 0.0) for i in ids]
    eg = [jnp.exp(gc[i]) for i in ids]
    kb = [k[i] * beta[i] for i in ids]
    kk = [_dot_nt(kb[i], k[i]) for i in ids]
    qk = [_dot_nt(q[i], k[i]) for i in ids]
    a_mat = [jnp.where(strict, kk[i] * decay[i], 0.0) for i in ids]
    t_inv = _inv_unit_lower(a_mat, masks, eye)
    rhs = [jnp.concatenate([v[i] * beta[i], kb[i] * eg[i]], axis=1) for i in ids]
    uw = [_dot(t_inv[i], rhs[i]) for i in ids]
    s_prev = [state_ref[i] for i in ids]
    wq = [jnp.concatenate([uw[i][:, DN_HEAD_DIM:], q[i] * eg[i]], axis=0) for i in ids]
    ws = [_dot(wq[i], s_prev[i]) for i in ids]
    v_new = [uw[i][:, :DN_HEAD_DIM] - ws[i][:CHUNK] for i in ids]
    attn = [qk[i] * decay[i] for i in ids]
    intra = [_dot(attn[i], v_new[i]) for i in ids]
    k_dec_t = [(k[i] * jnp.exp(g_last[i] - gc[i])).T for i in ids]
    kv = [_dot(k_dec_t[i], v_new[i]) for i in ids]
    for i, (s, h) in enumerate(chains):
        state_ref[i] = s_prev[i] * jnp.exp(g_last[i]) + kv[i]
        out = ws[i][CHUNK:] + intra[i]
        out = out * lax.rsqrt(jnp.mean(out * out, axis=-1, keepdims=True) + RMS_EPS) * og_ref[...]
        zh = z_ref[s, :, sl[i]].astype(F32)
        o_ref[s, :, sl[i]] = (out * (zh * _sigmoid(zh))).astype(o_ref.dtype)


def _delta_rule(qkvn, ba, proj, neg_a, dt_bias, onorm_g, *, batch, t_len):
    rows = qkvn.shape[0]
    n_chunks = t_len // CHUNK

    n = DELTA_STREAMS
    assert batch % n == 0
    groups = batch // n
    by_stream = lambda a: a.reshape(groups, n, t_len, a.shape[-1])

    def chunk_block(col):
        return lambda g, c: (g, 0, lax.rem(c + n_chunks - 1, n_chunks), col)

    const = lambda g, c: (0, 0)
    out = pl.pallas_call(
        _delta_kernel,
        grid=(groups, n_chunks),
        in_specs=[
            pl.BlockSpec((None, n, CHUNK, DN_WIDTH), chunk_block(0)),
            pl.BlockSpec((None, n, CHUNK, DN_WIDTH), chunk_block(1)),
            pl.BlockSpec((None, n, CHUNK, DN_WIDTH), chunk_block(2)),
            pl.BlockSpec((None, n, CHUNK, BA_WIDTH), chunk_block(0)),
            pl.BlockSpec((None, n, CHUNK, DN_WIDTH), chunk_block(COL_Z // DN_WIDTH)),
            pl.BlockSpec((1, LANES), const),
            pl.BlockSpec((1, LANES), const),
            pl.BlockSpec((1, DN_HEAD_DIM), const),
        ],
        out_specs=pl.BlockSpec((None, n, CHUNK, DN_WIDTH), chunk_block(0)),
        out_shape=jax.ShapeDtypeStruct((groups, n, t_len, DN_WIDTH), BF16),
        scratch_shapes=[pltpu.VMEM((n * DN_HEADS, DN_HEAD_DIM, DN_HEAD_DIM), F32)],
        compiler_params=_params(("arbitrary", "arbitrary")),
        name="delta_rule",
    )(by_stream(qkvn), by_stream(qkvn), by_stream(qkvn), by_stream(ba), by_stream(proj), neg_a, dt_bias, onorm_g)
    return out.reshape(rows, DN_WIDTH)


def _merge_kernel(o_ref, ya_ref, gp_ref, gd_ref, h0_ref, wd_ref, wo_ref, g0_ref, b0_ref, g1_ref, b1_ref, h1_ref):
    y_b = _dot(o_ref[...], wd_ref[...])
    merged = _sigmoid(gp_ref[...].astype(F32)) * ya_ref[...] + _sigmoid(gd_ref[...].astype(F32)) * y_b
    mix = _dot(merged.astype(BF16), wo_ref[...])
    h = _layer_norm(h0_ref[...], g0_ref[...], b0_ref[...])
    h1_ref[...] = _layer_norm(ALPHA * h + mix, g1_ref[...], b1_ref[...])


def _merge(o, ya, proj, h0, w_br_delta, w_out, g0, b0, g1, b1, *, batch, seq, t_len):
    bf16_rows = 2 * SUBLANES
    tiles_per_batch = next(n for n in (10, 8, 4, 2, 1) if t_len % (n * bf16_rows) == 0)
    tm = t_len // tiles_per_batch
    row = lambda i: (i, 0)
    const = lambda i: (0, 0)
    return pl.pallas_call(
        _merge_kernel,
        grid=(batch * tiles_per_batch,),
        in_specs=[
            pl.BlockSpec((tm, DN_WIDTH), row),
            pl.BlockSpec((tm, D_MODEL), row),
            pl.BlockSpec((tm, D_MODEL), lambda i: (i, COL_GP // D_MODEL)),
            pl.BlockSpec((tm, D_MODEL), lambda i: (i, COL_GD // D_MODEL)),
            pl.BlockSpec((tm, D_MODEL), row),
            pl.BlockSpec((DN_WIDTH, D_MODEL), const),
            pl.BlockSpec((D_MODEL, D_MODEL), const),
            pl.BlockSpec((1, D_MODEL), const),
            pl.BlockSpec((1, D_MODEL), const),
            pl.BlockSpec((1, D_MODEL), const),
            pl.BlockSpec((1, D_MODEL), const),
        ],
        out_specs=pl.BlockSpec((tm, D_MODEL), row),
        out_shape=jax.ShapeDtypeStruct((batch * t_len, D_MODEL), F32),
        compiler_params=_params(("arbitrary",)),
        name="merge",
    )(o, ya, proj, proj, h0, w_br_delta, w_out, g0, b0, g1, b1)


def _tile(ref, k):
    return ref[k * SUBLANES:(k + 1) * SUBLANES, :]


def _argmax_tournament(vals, ids):
    nodes = list(zip(vals, ids))
    while len(nodes) > 1:
        nxt = []
        for p in range(0, len(nodes) - 1, 2):
            (va, ia), (vb, ib) = nodes[p], nodes[p + 1]
            first = va >= vb
            nxt.append((jnp.maximum(va, vb), jnp.where(first, ia, ib)))
        if len(nodes) % 2:
            nxt.append(nodes[-1])
        nodes = nxt
    return nodes[0]


def _query_kernel(x_ref, wq_ref, k1_ref, k2_ref, idx_ref, gate_ref, s1_ref, s2_ref, m1_ref, i1_ref, m2_ref, i2_ref,
                  c_ref, e_ref, sc_ref):
    xb = x_ref[...].astype(BF16)
    q = _dot(xb, wq_ref[...])
    for h in range(PEER_HEADS):
        q1 = q[:, (2 * h) * PEER_HALF:(2 * h + 1) * PEER_HALF].astype(BF16)
        q2 = q[:, (2 * h + 1) * PEER_HALF:(2 * h + 2) * PEER_HALF].astype(BF16)
        s1_ref[pl.ds(h, N_KEYS, stride=SUBLANES), :] = _dot_nt(k1_ref[h], q1)
        s2_ref[pl.ds(h, N_KEYS, stride=SUBLANES), :] = _dot_nt(k2_ref[h], q2)

    neg_inf = jnp.float32(-jnp.inf)

    def sub_round(r, carry):
        for s_ref, m_ref, i_ref in ((s1_ref, m1_ref, i1_ref), (s2_ref, m2_ref, i2_ref)):
            vals = [_tile(s_ref, k) for k in range(N_KEYS)]
            best, arg = _argmax_tournament(vals, list(range(N_KEYS)))
            m_ref[r] = best
            i_ref[r] = arg
            for k in range(N_KEYS):
                s_ref[k * SUBLANES:(k + 1) * SUBLANES, :] = jnp.where(arg == k, neg_inf, vals[k])
        return carry

    lax.fori_loop(0, PEER_TOPK, sub_round, 0)

    for j, (a, b) in enumerate(_CAND):
        c_ref[j * SUBLANES:(j + 1) * SUBLANES, :] = m1_ref[a] + m2_ref[b]
        e_ref[j * SUBLANES:(j + 1) * SUBLANES, :] = i1_ref[a] * N_KEYS + i2_ref[b]

    def pair_round(r, carry):
        vals = [_tile(c_ref, j) for j in range(len(_CAND))]
        ids = [_tile(e_ref, j) for j in range(len(_CAND))]
        best, arg = _argmax_tournament(vals, ids)
        sc_ref[r] = best
        idx_ref[r] = arg
        for j in range(len(_CAND)):
            c_ref[j * SUBLANES:(j + 1) * SUBLANES, :] = jnp.where(ids[j] == arg, neg_inf, vals[j])
        return carry

    lax.fori_loop(0, PEER_TOPK, pair_round, 0)

    sc = sc_ref[...]
    ex = jnp.exp(sc - sc[0:1])
    gate_ref[...] = ex / jnp.sum(ex, axis=0, keepdims=True)


def _token_tile_map(seq, tile):
    per_batch = seq // tile
    return lambda i, *_: (i // per_batch, lax.rem(i, per_batch), 0)


def _peer_query(h1, wq, k1, k2, *, seq, tq):
    n_tok = h1.shape[0] * seq
    slot_shape = (PEER_TOPK, PEER_HEADS, tq)
    return pl.pallas_call(
        _query_kernel,
        grid=(n_tok // tq,),
        in_specs=[
            pl.BlockSpec((None, tq, D_MODEL), _token_tile_map(seq, tq)),
            pl.BlockSpec((D_MODEL, 2 * PEER_HALF * PEER_HEADS), lambda i: (0, 0)),
            pl.BlockSpec((PEER_HEADS, N_KEYS, PEER_HALF), lambda i: (0, 0, 0)),
            pl.BlockSpec((PEER_HEADS, N_KEYS, PEER_HALF), lambda i: (0, 0, 0)),
        ],
        out_specs=[
            pl.BlockSpec(slot_shape, lambda i: (0, 0, i)),
            pl.BlockSpec(slot_shape, lambda i: (0, 0, i)),
        ],
        out_shape=[
            jax.ShapeDtypeStruct((PEER_TOPK, PEER_HEADS, n_tok), I32),
            jax.ShapeDtypeStruct((PEER_TOPK, PEER_HEADS, n_tok), F32),
        ],
        scratch_shapes=[
            pltpu.VMEM((N_KEYS * SUBLANES, tq), F32),
            pltpu.VMEM((N_KEYS * SUBLANES, tq), F32),
            pltpu.VMEM(slot_shape, F32),
            pltpu.VMEM(slot_shape, I32),
            pltpu.VMEM(slot_shape, F32),
            pltpu.VMEM(slot_shape, I32),
            pltpu.VMEM((len(_CAND) * SUBLANES, tq), F32),
            pltpu.VMEM((len(_CAND) * SUBLANES, tq), I32),
            pltpu.VMEM(slot_shape, F32),
        ],
        compiler_params=_params(("arbitrary",)),
        name="peer_query",
    )(h1, wq, k1, k2)


IA_PER_STEP = 8
EXPERTS_PER_STEP = IA_PER_STEP * N_KEYS
PEER_STEPS = (N_KEYS * N_KEYS) // EXPERTS_PER_STEP
HIDDEN_CHUNKS = 2
OUTPUT_CHUNKS = 4
TABLE_STREAMS = 4


def _table_specs():
    rows = EXPERTS_PER_STEP // TABLE_STREAMS
    return [pl.BlockSpec((rows, D_MODEL), lambda i, j, p=p: (j * TABLE_STREAMS + p, 0)) for p in range(TABLE_STREAMS)]


def _one_hot_rows(idx_row):
    iota = lax.broadcasted_iota(I32, (N_KEYS, N_SLOTS), 0)
    return jnp.where(iota == idx_row, 1.0, 0.0).astype(BF16)


def _hidden_kernel(x_ref, *refs, tb, hp):
    u_refs = refs[:TABLE_STREAMS]
    ia_ref, ib_ref, gate_ref, act_ref, xb_ref, h0_ref, h1_ref, hs_ref = refs[TABLE_STREAMS:]
    parts_per_chunk = TABLE_STREAMS // HIDDEN_CHUNKS
    i = pl.program_id(0)
    j = pl.program_id(1)
    group = tb // PEER_STEPS // HIDDEN_CHUNKS
    ia_per_chunk = IA_PER_STEP // HIDDEN_CHUNKS

    @pl.when(jnp.logical_and(i == 0, j == 0))
    def _():
        h1_ref[...] = jnp.zeros_like(h1_ref)

    @pl.when(j == 0)
    def _():
        xb_ref[...] = x_ref[...].astype(BF16)

    def step(fill_ref, drain_ref):
        iota = lax.broadcasted_iota(I32, (N_KEYS, N_SLOTS), 0)
        for c in range(HIDDEN_CHUNKS):
            base = pl.multiple_of((j * HIDDEN_CHUNKS + c) * group, group)
            ia_rows = ia_ref[pl.ds(base, group), :]
            ib_rows = ib_ref[pl.ds(base, group), :]
            lhs, sel = [], []
            for u in range(group):
                sel.append(_one_hot_rows(ib_rows[u:u + 1, :]))
                start = pl.multiple_of((base + u) * hp, SUBLANES)
                lhs.append(drain_ref[pl.ds(start, N_KEYS), :].astype(BF16))
            by_slot = [_dot(lhs[u], sel[u]) for u in range(group)]
            rows = []
            for u in range(group):
                picked = jnp.where(iota == ia_rows[u:u + 1, :], by_slot[u], 0.0)
                rows.append(jnp.sum(picked, axis=0, keepdims=True))
            hs_ref[pl.ds(base, group), :] = jnp.concatenate(rows, axis=0)

            experts = jnp.concatenate(
                [u_refs[c * parts_per_chunk + p][...] for p in range(parts_per_chunk)], axis=0)
            hc = _dot_nt(xb_ref[...], experts)
            for t in range(ia_per_chunk):
                ia = j * IA_PER_STEP + c * ia_per_chunk + t
                fill_ref[pl.ds(ia, tb, stride=hp), :] = hc[:, t * N_KEYS:(t + 1) * N_KEYS]

    parity = lax.rem(i, 2)

    @pl.when(parity == 0)
    def _():
        step(h0_ref, h1_ref)

    @pl.when(parity == 1)
    def _():
        step(h1_ref, h0_ref)

    @pl.when(j == PEER_STEPS - 1)
    def _():
        hid = hs_ref[...]
        gelu = 0.5 * hid * (1.0 + lax.erf(hid * (2.0 ** -0.5)))
        act_ref[...] = gelu * gate_ref[...]


def _peer_hidden(h1, u_tab, ia, ib, gate, *, seq, tb):
    n_tiles = h1.shape[0] * seq // tb
    hp = N_KEYS + SUBLANES
    x_map = _token_tile_map(seq, tb)
    ahead = lambda i, j: x_map(jnp.minimum(i, n_tiles - 1))
    behind = lambda i, j: (jnp.maximum(i - 1, 0), 0)
    return pl.pallas_call(
        functools.partial(_hidden_kernel, tb=tb, hp=hp),
        grid=(n_tiles + 1, PEER_STEPS),
        in_specs=[
            pl.BlockSpec((None, tb, D_MODEL), ahead),
            *_table_specs(),
            pl.BlockSpec((tb, N_SLOTS), behind),
            pl.BlockSpec((tb, N_SLOTS), behind),
            pl.BlockSpec((tb, N_SLOTS), behind),
        ],
        out_specs=pl.BlockSpec((tb, N_SLOTS), behind),
        out_shape=jax.ShapeDtypeStruct((n_tiles * tb, N_SLOTS), F32),
        scratch_shapes=[
            pltpu.VMEM((tb, D_MODEL), BF16),
            pltpu.VMEM((tb * hp, N_KEYS), F32),
            pltpu.VMEM((tb * hp, N_KEYS), F32),
            pltpu.VMEM((tb, N_SLOTS), F32),
        ],
        compiler_params=_params(("arbitrary", "arbitrary")),
        name="peer_hidden",
    )(h1, *([u_tab] * TABLE_STREAMS), ia, ib, gate)


def _output_kernel(act_ref, ia_ref, ib_ref, *refs, tb, hp):
    assert TABLE_STREAMS == OUTPUT_CHUNKS
    v_refs = refs[:TABLE_STREAMS]
    h1_ref, g_ref, b_ref, out_ref, a0_ref, a1_ref, acc_ref = refs[TABLE_STREAMS:]
    i = pl.program_id(0)
    j = pl.program_id(1)
    group = tb // PEER_STEPS // OUTPUT_CHUNKS
    ia_per_chunk = IA_PER_STEP // OUTPUT_CHUNKS

    @pl.when(jnp.logical_and(i == 0, j == 0))
    def _():
        a1_ref[...] = jnp.zeros_like(a1_ref)

    @pl.when(j == 0)
    def _():
        acc_ref[...] = jnp.zeros_like(acc_ref)

    def step(fill_ref, drain_ref):
        iota = lax.broadcasted_iota(I32, (N_KEYS, N_SLOTS), 0)
        acc = acc_ref[...]
        for c in range(OUTPUT_CHUNKS):
            base = pl.multiple_of((j * OUTPUT_CHUNKS + c) * group, group)
            ia_rows = ia_ref[pl.ds(base, group), :]
            ib_rows = ib_ref[pl.ds(base, group), :]
            act_rows = act_ref[pl.ds(base, group), :]
            sel, act_a = [], []
            for u in range(group):
                sel.append(_one_hot_rows(ib_rows[u:u + 1, :]))
                act_a.append(jnp.where(iota == ia_rows[u:u + 1, :], act_rows[u:u + 1, :], 0.0).astype(BF16))
            dense = [_dot_nt(act_a[u], sel[u]) for u in range(group)]
            pair_base = (j * OUTPUT_CHUNKS + c) * (group // 2)
            for p in range(group // 2):
                packed = pltpu.pack_elementwise([dense[2 * p], dense[2 * p + 1]], packed_dtype=BF16)
                fill_ref[pl.ds(pair_base + p, N_KEYS, stride=hp), :] = lax.bitcast_convert_type(packed, I32)

            tiles = []
            for t in range(ia_per_chunk):
                ia = j * IA_PER_STEP + c * ia_per_chunk + t
                start = pl.multiple_of(ia * hp, SUBLANES)
                tiles.append(pltpu.bitcast(drain_ref[pl.ds(start, tb // 2), :], BF16))
            acc = acc + _dot(jnp.concatenate(tiles, axis=1), v_refs[c][...])
        acc_ref[...] = acc

    parity = lax.rem(i, 2)

    @pl.when(parity == 0)
    def _():
        step(a0_ref, a1_ref)

    @pl.when(parity == 1)
    def _():
        step(a1_ref, a0_ref)

    @pl.when(j == PEER_STEPS - 1)
    def _():
        out_ref[...] = _layer_norm(ALPHA * h1_ref[...] + acc_ref[...], g_ref[...], b_ref[...])


def _peer_output(act, ia, ib, v_tab, h1, g2, b2, *, seq, tb):
    n_tiles = h1.shape[0] * seq // tb
    hp = tb // 2 + SUBLANES
    x_map = _token_tile_map(seq, tb)
    ahead = lambda i, j: (jnp.minimum(i, n_tiles - 1), 0)
    behind = lambda i, j: (jnp.maximum(i - 1, 0), 0)
    const = lambda i, j: (0, 0)
    return pl.pallas_call(
        functools.partial(_output_kernel, tb=tb, hp=hp),
        grid=(n_tiles + 1, PEER_STEPS),
        in_specs=[
            pl.BlockSpec((tb, N_SLOTS), ahead),
            pl.BlockSpec((tb, N_SLOTS), ahead),
            pl.BlockSpec((tb, N_SLOTS), ahead),
            *_table_specs(),
            pl.BlockSpec((None, tb, D_MODEL), lambda i, j: x_map(jnp.maximum(i - 1, 0))),
            pl.BlockSpec((1, D_MODEL), const),
            pl.BlockSpec((1, D_MODEL), const),
        ],
        out_specs=pl.BlockSpec((tb, D_MODEL), behind),
        out_shape=jax.ShapeDtypeStruct((n_tiles * tb, D_MODEL), F32),
        scratch_shapes=[
            pltpu.VMEM((N_KEYS * hp, N_KEYS), I32),
            pltpu.VMEM((N_KEYS * hp, N_KEYS), I32),
            pltpu.VMEM((tb, D_MODEL), F32),
        ],
        compiler_params=_params(("arbitrary", "arbitrary")),
        name="peer_output",
    )(act, ia, ib, *([v_tab] * TABLE_STREAMS), h1, g2, b2)


def _pick_tile(n, candidates):
    for c in candidates:
        if n % c == 0:
            return c
    raise ValueError(f"no tile in {candidates} divides {n}")


def kernel(x, meta, ln0_g, ln0_b, w_in, pool_w, pool_scale, w_br_pool, conv_w, a_log, dt_bias, onorm_g, w_br_delta, w_out, ln1_g, ln1_b, peer_wq, peer_k1, peer_k2, peer_u, peer_v, ln2_g, ln2_b):
    batch, seq, d = x.shape
    assert d == D_MODEL and seq % CHUNK == 0 and w_in.shape[0] == 1
    t_len = seq + CHUNK
    n_tok = batch * seq
    row = lambda p: p.reshape(1, -1).astype(F32)

    meta_b = jnp.broadcast_to(meta[None].astype(x.dtype), (batch, N_META, d))
    h0 = jnp.concatenate([x, jnp.zeros((batch, PAD, d), x.dtype), meta_b], axis=1).reshape(batch * t_len, d)

    w = w_in[0]
    c_pool, c_qkv, c_z = POOL_WIDTH, POOL_WIDTH + 3 * DN_WIDTH, POOL_WIDTH + 4 * DN_WIDTH
    c_b, c_a = c_z + DN_HEADS, c_z + 2 * DN_HEADS
    w_main = jnp.concatenate([w[:, c_pool:c_z], w[:, c_a:], w[:, :c_pool]], axis=1).astype(BF16)
    w_ba = jnp.zeros((d, BA_WIDTH), F32)
    w_ba = w_ba.at[:, :DN_HEADS].set(w[:, c_z:c_b]).at[:, LANES:LANES + DN_HEADS].set(w[:, c_b:c_a]).astype(BF16)
    lane_pad = lambda p: jnp.zeros((1, LANES), F32).at[0, :DN_HEADS].set(p.astype(F32))

    proj, ba = _inproj(h0, row(ln0_g), row(ln0_b), w_main, w_ba, seq=seq, t_len=t_len)
    ya = _pool_branch(proj, pool_w[0].astype(BF16), row(pool_scale[0]), w_br_pool[0].astype(BF16),
                      batch=batch, seq=seq, t_len=t_len)
    qkvn = _conv_qkv(proj, conv_w[0].astype(F32), batch=batch, seq=seq, t_len=t_len)
    o = _delta_rule(qkvn, ba, proj, lane_pad(-jnp.exp(a_log[0].astype(F32))), lane_pad(dt_bias[0]),
                    row(onorm_g[0]), batch=batch, t_len=t_len)
    h1 = _merge(o, ya, proj, h0, w_br_delta[0].astype(BF16), w_out[0].astype(BF16), row(ln0_g), row(ln0_b),
                row(ln1_g[0]), row(ln1_b[0]), batch=batch, seq=seq, t_len=t_len)
    h1 = h1.reshape(batch, t_len, d)

    idx, gate = _peer_query(h1, peer_wq[0].astype(BF16), peer_k1[0].astype(BF16), peer_k2[0].astype(BF16),
                            seq=seq, tq=LANES)
    idx = idx.reshape(N_SLOTS, n_tok).T
    gate = gate.reshape(N_SLOTS, n_tok).T
    ia = lax.shift_right_logical(idx, 7)
    ib = lax.bitwise_and(idx, N_KEYS - 1)

    act = _peer_hidden(h1, peer_u[0].astype(BF16), ia, ib, gate, seq=seq, tb=_pick_tile(seq, (256,)))
    out = _peer_output(act, ia, ib, peer_v[0].astype(BF16), h1, row(ln2_g[0]), row(ln2_b[0]),
                       seq=seq, tb=_pick_tile(seq, (512,)))
    return out.reshape(batch, seq, d)
```

```python
import functools
import math

import jax
import jax.numpy as jnp
from jax import lax
from jax.experimental import pallas as pl
from jax.experimental.pallas import tpu as pltpu

F32 = jnp.float32
BF16 = jnp.bfloat16
I32 = jnp.int32

D_MODEL = 1024
N_META = 16
CHUNK = 64
PAD = CHUNK - N_META
POOL_GROUPS = 4
POOL_GROUP_DIM = 128
POOL_WIDTH = POOL_GROUPS * POOL_GROUP_DIM
POOL_WINDOWS = (2, 4, 8, 16)
POOL_HALO = 16
DN_HEADS = 8
DN_HEAD_DIM = 128
DN_WIDTH = DN_HEADS * DN_HEAD_DIM
CONV_WIDTH = 4
PEER_HEADS = 8
PEER_HALF = 128
N_KEYS = 128
PEER_TOPK = 16
N_SLOTS = PEER_HEADS * PEER_TOPK
LN_EPS = 1e-5
RMS_EPS = 1e-6
L2_EPS = 1e-6
ALPHA = 2.0 ** 0.25

COL_QKV = 0
COL_Z = 3 * DN_WIDTH
COL_GP = COL_Z + DN_WIDTH
COL_GD = COL_GP + D_MODEL
COL_POOL = COL_GD + D_MODEL
N_PROJ = COL_POOL + POOL_WIDTH
BA_WIDTH = 256

SUBLANES = 8
LANES = 128
VMEM_LIMIT = 60 * 1024 * 1024

_CAND = [(a, b) for a in range(PEER_TOPK) for b in range(PEER_TOPK) if (a + 1) * (b + 1) <= PEER_TOPK]


def _params(sem):
    return pltpu.CompilerParams(dimension_semantics=sem, vmem_limit_bytes=VMEM_LIMIT)


def _layer_norm(x, g, b):
    mu = jnp.mean(x, axis=-1, keepdims=True)
    xc = x - mu
    var = jnp.mean(xc * xc, axis=-1, keepdims=True)
    return xc * lax.rsqrt(var + LN_EPS) * g + b


def _zero_pad_rows(y, row, seq):
    return jnp.where(row >= seq, jnp.where(row < seq + PAD, 0.0, y), y)


def _sigmoid(x):
    return 1.0 / (1.0 + jnp.exp(-x))


def _dot(a, b):
    return jnp.dot(a, b, preferred_element_type=F32)


def _dot_nt(a, b):
    return lax.dot_general(a, b, (((1,), (1,)), ((), ())), preferred_element_type=F32)


def _inproj_kernel(x_ref, g_ref, b_ref, w_ref, wba_ref, proj_ref, ba_ref, xn_ref, *, tiles_per_batch, seq):
    i = pl.program_id(0)
    j = pl.program_id(1)
    tm = x_ref.shape[0]

    @pl.when(j == 0)
    def _():
        y = _layer_norm(x_ref[...], g_ref[...], b_ref[...])
        row = lax.broadcasted_iota(I32, (tm, 1), 0) + lax.rem(i, tiles_per_batch) * tm
        xn = _zero_pad_rows(y, row, seq).astype(BF16)
        xn_ref[...] = xn
        ba_ref[...] = _dot(xn, wba_ref[...])

    proj_ref[...] = _dot(xn_ref[...], w_ref[...]).astype(proj_ref.dtype)


def _inproj(h0, ln_g, ln_b, w_main, w_ba, *, seq, t_len):
    rows = h0.shape[0]
    tiles_per_batch = 4
    tm = t_len // tiles_per_batch
    tn = N_PROJ // 4
    kern = functools.partial(_inproj_kernel, tiles_per_batch=tiles_per_batch, seq=seq)
    return pl.pallas_call(
        kern,
        grid=(rows // tm, N_PROJ // tn),
        in_specs=[
            pl.BlockSpec((tm, D_MODEL), lambda i, j: (i, 0)),
            pl.BlockSpec((1, D_MODEL), lambda i, j: (0, 0)),
            pl.BlockSpec((1, D_MODEL), lambda i, j: (0, 0)),
            pl.BlockSpec((D_MODEL, tn), lambda i, j: (0, j)),
            pl.BlockSpec((D_MODEL, BA_WIDTH), lambda i, j: (0, 0)),
        ],
        out_specs=[
            pl.BlockSpec((tm, tn), lambda i, j: (i, j)),
            pl.BlockSpec((tm, BA_WIDTH), lambda i, j: (i, 0)),
        ],
        out_shape=[
            jax.ShapeDtypeStruct((rows, N_PROJ), BF16),
            jax.ShapeDtypeStruct((rows, BA_WIDTH), F32),
        ],
        scratch_shapes=[pltpu.VMEM((tm, D_MODEL), BF16)],
        compiler_params=_params(("arbitrary", "arbitrary")),
        name="inproj",
    )(h0, ln_g, ln_b, w_main, w_ba)


def _pool_kernel(xp_ref, halo_ref, pw_ref, ps_ref, wbp_ref, ya_ref, *, seq):
    i = pl.program_id(1)
    tm = xp_ref.shape[0]
    v = jnp.concatenate([halo_ref[...].astype(F32), xp_ref[...].astype(F32)], axis=0)
    row = lax.broadcasted_iota(I32, (tm, POOL_GROUP_DIM), 0) + i * tm
    meta_pos = row - (seq + PAD)
    outs = []
    for gi, w in enumerate(POOL_WINDOWS):
        vg = v[:, gi * POOL_GROUP_DIM:(gi + 1) * POOL_GROUP_DIM]
        s = vg
        shift = 1
        while shift < w:
            s = s + pltpu.roll(s, shift, axis=0)
            shift *= 2
        s = s[POOL_HALO:]
        xg = vg[POOL_HALO:]
        count = jnp.where(meta_pos >= 0, jnp.minimum(meta_pos + 1, w), w).astype(F32)
        pooled = s / count - xg
        mixed = _dot(pooled.astype(BF16), pw_ref[gi])
        outs.append(mixed)
    y_pool = jnp.concatenate(outs, axis=1) * ps_ref[...]
    ya_ref[...] = _dot(y_pool.astype(BF16), wbp_ref[...])


def _pool_branch(proj, pool_w, pool_scale, w_br_pool, *, batch, seq, t_len):
    rows = proj.shape[0]
    tiles_per_batch = 4
    tm = t_len // tiles_per_batch
    halo_blocks_per_batch = t_len // POOL_HALO
    halo_blocks_per_tile = tm // POOL_HALO
    pool_col = COL_POOL // POOL_WIDTH

    def halo_map(b, i):
        prev = lax.rem(i * halo_blocks_per_tile + halo_blocks_per_batch - 1, halo_blocks_per_batch)
        return (b * halo_blocks_per_batch + prev, pool_col)

    return pl.pallas_call(
        functools.partial(_pool_kernel, seq=seq),
        grid=(batch, tiles_per_batch),
        in_specs=[
            pl.BlockSpec((tm, POOL_WIDTH), lambda b, i: (b * tiles_per_batch + i, pool_col)),
            pl.BlockSpec((POOL_HALO, POOL_WIDTH), halo_map),
            pl.BlockSpec((POOL_GROUPS, POOL_GROUP_DIM, POOL_GROUP_DIM), lambda b, i: (0, 0, 0)),
            pl.BlockSpec((1, POOL_WIDTH), lambda b, i: (0, 0)),
            pl.BlockSpec((POOL_WIDTH, D_MODEL), lambda b, i: (0, 0)),
        ],
        out_specs=pl.BlockSpec((tm, D_MODEL), lambda b, i: (b * tiles_per_batch + i, 0)),
        out_shape=jax.ShapeDtypeStruct((rows, D_MODEL), F32),
        compiler_params=_params(("arbitrary", "arbitrary")),
        name="pool_branch",
    )(proj, proj, pool_w, pool_scale, w_br_pool)


def _conv_kernel(x_ref, w_ref, o_ref, *, seq):
    s = pl.program_id(1)
    x = x_ref[...].astype(F32)
    t_len = x.shape[0]
    w = w_ref[...]
    y = x * w[CONV_WIDTH - 1:CONV_WIDTH, :]
    for lag in range(1, CONV_WIDTH):
        y = y + pltpu.roll(x, lag, axis=0) * w[CONV_WIDTH - 1 - lag:CONV_WIDTH - lag, :]
    y = y * _sigmoid(y)
    ss = jnp.sum(y * y, axis=-1, keepdims=True)
    q_scale = jnp.where(s < DN_HEADS, DN_HEAD_DIM ** -0.5, 1.0).astype(F32)
    fac = jnp.where(s < 2 * DN_HEADS, lax.rsqrt(ss + L2_EPS) * q_scale, 1.0)
    row = lax.broadcasted_iota(I32, (t_len, 1), 0)
    o_ref[...] = _zero_pad_rows(y * fac, row, seq).astype(o_ref.dtype)


def _conv_qkv(proj, conv_w, *, batch, seq, t_len):
    rows = proj.shape[0]
    n_slabs = 3 * DN_HEADS
    return pl.pallas_call(
        functools.partial(_conv_kernel, seq=seq),
        grid=(batch, n_slabs),
        in_specs=[
            pl.BlockSpec((t_len, DN_HEAD_DIM), lambda b, s: (b, COL_QKV // DN_HEAD_DIM + s)),
            pl.BlockSpec((CONV_WIDTH, DN_HEAD_DIM), lambda b, s: (0, s)),
        ],
        out_specs=pl.BlockSpec((t_len, DN_HEAD_DIM), lambda b, s: (b, s)),
        out_shape=jax.ShapeDtypeStruct((rows, 3 * DN_WIDTH), BF16),
        compiler_params=_params(("arbitrary", "arbitrary")),
        name="conv_qkv",
    )(proj, conv_w)


def _inv_unit_lower(a, masks, eye):
    m8, m16, m32, m64 = masks
    n = [x * m8 for x in a]
    n2 = [_dot(x, x) for x in n]
    n4 = [_dot(x, x) for x in n2]
    t = [eye - x for x in n]
    t = [x + _dot(x, y) for x, y in zip(t, n2)]
    t = [x + _dot(x, y) for x, y in zip(t, n4)]
    for m in (m16, m32, m64):
        at = [_dot(x * m, y) for x, y in zip(a, t)]
        t = [x - _dot(x, y) for x, y in zip(t, at)]
    return t


DELTA_STREAMS = 2


def _delta_kernel(q_ref, k_ref, v_ref, ba_ref, z_ref, nega_ref, dtb_ref, og_ref, o_ref, state_ref):
    n = DELTA_STREAMS
    c = pl.program_id(1)

    @pl.when(c == 0)
    def _():
        state_ref[...] = jnp.zeros_like(state_ref)

    r = lax.broadcasted_iota(I32, (CHUNK, CHUNK), 0)
    cc = lax.broadcasted_iota(I32, (CHUNK, CHUNK), 1)
    causal = r >= cc
    strict = r > cc
    eye = jnp.where(r == cc, 1.0, 0.0).astype(F32)
    tril = jnp.where(causal, 1.0, 0.0).astype(F32)

    def blk(x, s):
        return lax.shift_right_logical(x, s)

    lower = jnp.where(strict, 1.0, 0.0).astype(F32)
    m8 = jnp.where(blk(r, 3) == blk(cc, 3), lower, 0.0)
    m16 = jnp.where((blk(r, 4) == blk(cc, 4)) & (blk(r, 3) != blk(cc, 3)), lower, 0.0)
    m32 = jnp.where((blk(r, 5) == blk(cc, 5)) & (blk(r, 4) != blk(cc, 4)), lower, 0.0)
    m64 = jnp.where(blk(r, 5) != blk(cc, 5), lower, 0.0)
    masks = (m8, m16, m32, m64)

    beta_all, gcum, gcum_t = [], [], []
    for s in range(n):
        ba = ba_ref[s]
        beta_all.append(_sigmoid(ba[:, :LANES]))
        x = ba[:, LANES:] + dtb_ref[...]
        softplus = jnp.maximum(x, 0.0) + jnp.log(1.0 + jnp.exp(-jnp.abs(x)))
        g_all = nega_ref[...] * softplus
        gcum.append(lax.dot_general(tril, g_all, (((1,), (0,)), ((), ())), precision=lax.Precision.HIGHEST,
                                    preferred_element_type=F32))
        gcum_t.append(gcum[s].T)

    chains = [(s, h) for s in range(n) for h in range(DN_HEADS)]
    ids = range(len(chains))
    sl = [slice(h * DN_HEAD_DIM, (h + 1) * DN_HEAD_DIM) for _, h in chains]
    q = [q_ref[s, :, sl[i]].astype(F32) for i, (s, h) in enumerate(chains)]
    k = [k_ref[s, :, sl[i]].astype(F32) for i, (s, h) in enumerate(chains)]
    v = [v_ref[s, :, sl[i]].astype(F32) for i, (s, h) in enumerate(chains)]
    beta = [beta_all[s][:, h:h + 1] for s, h in chains]
    gc = [gcum[s][:, h:h + 1] for s, h in chains]
    gr = [gcum_t[s][h:h + 1, :] for s, h in chains]
    g_last = [gcum[s][CHUNK - 1:CHUNK, h:h + 1] for s, h in chains]
    decay = [jnp.where(causal, jnp.exp(jnp.where(causal, gc[i] - gr[i], 0.0)), 0.0) for i in ids]
    eg = [jnp.exp(gc[i]) for i in ids]
    kb = [k[i] * beta[i] for i in ids]
    kk = [_dot_nt(kb[i], k[i]) for i in ids]
    qk = [_dot_nt(q[i], k[i]) for i in ids]
    a_mat = [jnp.where(strict, kk[i] * decay[i], 0.0) for i in ids]
    t_inv = _inv_unit_lower(a_mat, masks, eye)
    rhs = [jnp.concatenate([v[i] * beta[i], kb[i] * eg[i]], axis=1) for i in ids]
    uw = [_dot(t_inv[i], rhs[i]) for i in ids]
    s_prev = [state_ref[i] for i in ids]
    wq = [jnp.concatenate([uw[i][:, DN_HEAD_DIM:], q[i] * eg[i]], axis=0) for i in ids]
    ws = [_dot(wq[i], s_prev[i]) for i in ids]
    v_new = [uw[i][:, :DN_HEAD_DIM] - ws[i][:CHUNK] for i in ids]
    attn = [qk[i] * decay[i] for i in ids]
    intra = [_dot(attn[i], v_new[i]) for i in ids]
    k_dec_t = [(k[i] * jnp.exp(g_last[i] - gc[i])).T for i in ids]
    kv = [_dot(k_dec_t[i], v_new[i]) for i in ids]
    for i, (s, h) in enumerate(chains):
        state_ref[i] = s_prev[i] * jnp.exp(g_last[i]) + kv[i]
        out = ws[i][CHUNK:] + intra[i]
        out = out * lax.rsqrt(jnp.mean(out * out, axis=-1, keepdims=True) + RMS_EPS) * og_ref[...]
        zh = z_ref[s, :, sl[i]].astype(F32)
        o_ref[s, :, sl[i]] = (out * (zh * _sigmoid(zh))).astype(o_ref.dtype)


def _delta_rule(qkvn, ba, proj, neg_a, dt_bias, onorm_g, *, batch, t_len):
    rows = qkvn.shape[0]
    n_chunks = t_len // CHUNK

    n = DELTA_STREAMS
    assert batch % n == 0
    groups = batch // n
    by_stream = lambda a: a.reshape(groups, n, t_len, a.shape[-1])

    def chunk_block(col):
        return lambda g, c: (g, 0, lax.rem(c + n_chunks - 1, n_chunks), col)

    const = lambda g, c: (0, 0)
    out = pl.pallas_call(
        _delta_kernel,
        grid=(groups, n_chunks),
        in_specs=[
            pl.BlockSpec((None, n, CHUNK, DN_WIDTH), chunk_block(0)),
            pl.BlockSpec((None, n, CHUNK, DN_WIDTH), chunk_block(1)),
            pl.BlockSpec((None, n, CHUNK, DN_WIDTH), chunk_block(2)),
            pl.BlockSpec((None, n, CHUNK, BA_WIDTH), chunk_block(0)),
            pl.BlockSpec((None, n, CHUNK, DN_WIDTH), chunk_block(COL_Z // DN_WIDTH)),
            pl.BlockSpec((1, LANES), const),
            pl.BlockSpec((1, LANES), const),
            pl.BlockSpec((1, DN_HEAD_DIM), const),
        ],
        out_specs=pl.BlockSpec((None, n, CHUNK, DN_WIDTH), chunk_block(0)),
        out_shape=jax.ShapeDtypeStruct((groups, n, t_len, DN_WIDTH), BF16),
        scratch_shapes=[pltpu.VMEM((n * DN_HEADS, DN_HEAD_DIM, DN_HEAD_DIM), F32)],
        compiler_params=_params(("arbitrary", "arbitrary")),
        name="delta_rule",
    )(by_stream(qkvn), by_stream(qkvn), by_stream(qkvn), by_stream(ba), by_stream(proj), neg_a, dt_bias, onorm_g)
    return out.reshape(rows, DN_WIDTH)


def _merge_kernel(o_ref, ya_ref, gp_ref, gd_ref, h0_ref, wd_ref, wo_ref, g0_ref, b0_ref, g1_ref, b1_ref, h1_ref):
    y_b = _dot(o_ref[...], wd_ref[...])
    merged = _sigmoid(gp_ref[...].astype(F32)) * ya_ref[...] + _sigmoid(gd_ref[...].astype(F32)) * y_b
    mix = _dot(merged.astype(BF16), wo_ref[...])
    h = _layer_norm(h0_ref[...], g0_ref[...], b0_ref[...])
    h1_ref[...] = _layer_norm(ALPHA * h + mix, g1_ref[...], b1_ref[...])


def _merge(o, ya, proj, h0, w_br_delta, w_out, g0, b0, g1, b1, *, batch, seq, t_len):
    bf16_rows = 2 * SUBLANES
    tiles_per_batch = next(n for n in (10, 8, 4, 2, 1) if t_len % (n * bf16_rows) == 0)
    tm = t_len // tiles_per_batch
    row = lambda i: (i, 0)
    const = lambda i: (0, 0)
    return pl.pallas_call(
        _merge_kernel,
        grid=(batch * tiles_per_batch,),
        in_specs=[
            pl.BlockSpec((tm, DN_WIDTH), row),
            pl.BlockSpec((tm, D_MODEL), row),
            pl.BlockSpec((tm, D_MODEL), lambda i: (i, COL_GP // D_MODEL)),
            pl.BlockSpec((tm, D_MODEL), lambda i: (i, COL_GD // D_MODEL)),
            pl.BlockSpec((tm, D_MODEL), row),
            pl.BlockSpec((DN_WIDTH, D_MODEL), const),
            pl.BlockSpec((D_MODEL, D_MODEL), const),
            pl.BlockSpec((1, D_MODEL), const),
            pl.BlockSpec((1, D_MODEL), const),
            pl.BlockSpec((1, D_MODEL), const),
            pl.BlockSpec((1, D_MODEL), const),
        ],
        out_specs=pl.BlockSpec((tm, D_MODEL), row),
        out_shape=jax.ShapeDtypeStruct((batch * t_len, D_MODEL), F32),
        compiler_params=_params(("arbitrary",)),
        name="merge",
    )(o, ya, proj, proj, h0, w_br_delta, w_out, g0, b0, g1, b1)


def _tile(ref, k):
    return ref[k * SUBLANES:(k + 1) * SUBLANES, :]


def _argmax_tournament(vals, ids):
    nodes = list(zip(vals, ids))
    while len(nodes) > 1:
        nxt = []
        for p in range(0, len(nodes) - 1, 2):
            (va, ia), (vb, ib) = nodes[p], nodes[p + 1]
            first = va >= vb
            nxt.append((jnp.maximum(va, vb), jnp.where(first, ia, ib)))
        if len(nodes) % 2:
            nxt.append(nodes[-1])
        nodes = nxt
    return nodes[0]


def _query_kernel(x_ref, wq_ref, k1_ref, k2_ref, idx_ref, gate_ref, s1_ref, s2_ref, m1_ref, i1_ref, m2_ref, i2_ref,
                  c_ref, e_ref, sc_ref):
    xb = x_ref[...].astype(BF16)
    q = _dot(xb, wq_ref[...])
    for h in range(PEER_HEADS):
        q1 = q[:, (2 * h) * PEER_HALF:(2 * h + 1) * PEER_HALF].astype(BF16)
        q2 = q[:, (2 * h + 1) * PEER_HALF:(2 * h + 2) * PEER_HALF].astype(BF16)
        s1_ref[pl.ds(h, N_KEYS, stride=SUBLANES), :] = _dot_nt(k1_ref[h], q1)
        s2_ref[pl.ds(h, N_KEYS, stride=SUBLANES), :] = _dot_nt(k2_ref[h], q2)

    neg_inf = jnp.float32(-jnp.inf)

    def sub_round(r, carry):
        for s_ref, m_ref, i_ref in ((s1_ref, m1_ref, i1_ref), (s2_ref, m2_ref, i2_ref)):
            vals = [_tile(s_ref, k) for k in range(N_KEYS)]
            best, arg = _argmax_tournament(vals, list(range(N_KEYS)))
            m_ref[r] = best
            i_ref[r] = arg
            for k in range(N_KEYS):
                s_ref[k * SUBLANES:(k + 1) * SUBLANES, :] = jnp.where(arg == k, neg_inf, vals[k])
        return carry

    lax.fori_loop(0, PEER_TOPK, sub_round, 0)

    for j, (a, b) in enumerate(_CAND):
        c_ref[j * SUBLANES:(j + 1) * SUBLANES, :] = m1_ref[a] + m2_ref[b]
        e_ref[j * SUBLANES:(j + 1) * SUBLANES, :] = i1_ref[a] * N_KEYS + i2_ref[b]

    def pair_round(r, carry):
        vals = [_tile(c_ref, j) for j in range(len(_CAND))]
        ids = [_tile(e_ref, j) for j in range(len(_CAND))]
        best, arg = _argmax_tournament(vals, ids)
        sc_ref[r] = best
        idx_ref[r] = arg
        for j in range(len(_CAND)):
            c_ref[j * SUBLANES:(j + 1) * SUBLANES, :] = jnp.where(ids[j] == arg, neg_inf, vals[j])
        return carry

    lax.fori_loop(0, PEER_TOPK, pair_round, 0)

    sc = sc_ref[...]
    ex = jnp.exp(sc - sc[0:1])
    gate_ref[...] = ex / jnp.sum(ex, axis=0, keepdims=True)


def _token_tile_map(seq, tile):
    per_batch = seq // tile
    return lambda i, *_: (i // per_batch, lax.rem(i, per_batch), 0)


def _peer_query(h1, wq, k1, k2, *, seq, tq):
    n_tok = h1.shape[0] * seq
    slot_shape = (PEER_TOPK, PEER_HEADS, tq)
    return pl.pallas_call(
        _query_kernel,
        grid=(n_tok // tq,),
        in_specs=[
            pl.BlockSpec((None, tq, D_MODEL), _token_tile_map(seq, tq)),
            pl.BlockSpec((D_MODEL, 2 * PEER_HALF * PEER_HEADS), lambda i: (0, 0)),
            pl.BlockSpec((PEER_HEADS, N_KEYS, PEER_HALF), lambda i: (0, 0, 0)),
            pl.BlockSpec((PEER_HEADS, N_KEYS, PEER_HALF), lambda i: (0, 0, 0)),
        ],
        out_specs=[
            pl.BlockSpec(slot_shape, lambda i: (0, 0, i)),
            pl.BlockSpec(slot_shape, lambda i: (0, 0, i)),
        ],
        out_shape=[
            jax.ShapeDtypeStruct((PEER_TOPK, PEER_HEADS, n_tok), I32),
            jax.ShapeDtypeStruct((PEER_TOPK, PEER_HEADS, n_tok), F32),
        ],
        scratch_shapes=[
            pltpu.VMEM((N_KEYS * SUBLANES, tq), F32),
            pltpu.VMEM((N_KEYS * SUBLANES, tq), F32),
            pltpu.VMEM(slot_shape, F32),
            pltpu.VMEM(slot_shape, I32),
            pltpu.VMEM(slot_shape, F32),
            pltpu.VMEM(slot_shape, I32),
            pltpu.VMEM((len(_CAND) * SUBLANES, tq), F32),
            pltpu.VMEM((len(_CAND) * SUBLANES, tq), I32),
            pltpu.VMEM(slot_shape, F32),
        ],
        compiler_params=_params(("arbitrary",)),
        name="peer_query",
    )(h1, wq, k1, k2)


IA_PER_STEP = 16
EXPERTS_PER_STEP = IA_PER_STEP * N_KEYS
PEER_STEPS = (N_KEYS * N_KEYS) // EXPERTS_PER_STEP
HIDDEN_CHUNKS = 4
OUTPUT_CHUNKS = 8
TABLE_STREAMS = 4


def _table_specs():
    rows = EXPERTS_PER_STEP // TABLE_STREAMS
    return [pl.BlockSpec((rows, D_MODEL), lambda i, j, p=p: (j * TABLE_STREAMS + p, 0)) for p in range(TABLE_STREAMS)]


def _one_hot_rows(idx_row):
    iota = lax.broadcasted_iota(I32, (N_KEYS, N_SLOTS), 0)
    return jnp.where(iota == idx_row, 1.0, 0.0).astype(BF16)


def _hidden_kernel(x_ref, *refs, tb, hp):
    u_refs = refs[:TABLE_STREAMS]
    ia_ref, ib_ref, gate_ref, act_ref, xb_ref, h0_ref, h1_ref, hs_ref = refs[TABLE_STREAMS:]
    parts_per_chunk = TABLE_STREAMS // HIDDEN_CHUNKS
    i = pl.program_id(0)
    j = pl.program_id(1)
    group = tb // PEER_STEPS // HIDDEN_CHUNKS
    ia_per_chunk = IA_PER_STEP // HIDDEN_CHUNKS

    @pl.when(jnp.logical_and(i == 0, j == 0))
    def _():
        h1_ref[...] = jnp.zeros_like(h1_ref)

    @pl.when(j == 0)
    def _():
        xb_ref[...] = x_ref[...].astype(BF16)

    def step(fill_ref, drain_ref):
        iota = lax.broadcasted_iota(I32, (N_KEYS, N_SLOTS), 0)
        for c in range(HIDDEN_CHUNKS):
            base = pl.multiple_of((j * HIDDEN_CHUNKS + c) * group, group)
            ia_rows = ia_ref[pl.ds(base, group), :]
            ib_rows = ib_ref[pl.ds(base, group), :]
            lhs, sel = [], []
            for u in range(group):
                sel.append(_one_hot_rows(ib_rows[u:u + 1, :]))
                start = pl.multiple_of((base + u) * hp, SUBLANES)
                lhs.append(drain_ref[pl.ds(start, N_KEYS), :].astype(BF16))
            by_slot = [_dot(lhs[u], sel[u]) for u in range(group)]
            rows = []
            for u in range(group):
                picked = jnp.where(iota == ia_rows[u:u + 1, :], by_slot[u], 0.0)
                rows.append(jnp.sum(picked, axis=0, keepdims=True))
            hs_ref[pl.ds(base, group), :] = jnp.concatenate(rows, axis=0)

            experts = jnp.concatenate(
                [u_refs[c * parts_per_chunk + p][...] for p in range(parts_per_chunk)], axis=0)
            hc = _dot_nt(xb_ref[...], experts)
            for t in range(ia_per_chunk):
                ia = j * IA_PER_STEP + c * ia_per_chunk + t
                fill_ref[pl.ds(ia, tb, stride=hp), :] = hc[:, t * N_KEYS:(t + 1) * N_KEYS]

    parity = lax.rem(i, 2)

    @pl.when(parity == 0)
    def _():
        step(h0_ref, h1_ref)

    @pl.when(parity == 1)
    def _():
        step(h1_ref, h0_ref)

    @pl.when(j == PEER_STEPS - 1)
    def _():
        hid = hs_ref[...]
        gelu = 0.5 * hid * (1.0 + lax.erf(hid * (2.0 ** -0.5)))
        act_ref[...] = gelu * gate_ref[...]


def _peer_hidden(h1, u_tab, ia, ib, gate, *, seq, tb):
    n_tiles = h1.shape[0] * seq // tb
    hp = N_KEYS + SUBLANES
    x_map = _token_tile_map(seq, tb)
    ahead = lambda i, j: x_map(jnp.minimum(i, n_tiles - 1))
    behind = lambda i, j: (jnp.maximum(i - 1, 0), 0)
    return pl.pallas_call(
        functools.partial(_hidden_kernel, tb=tb, hp=hp),
        grid=(n_tiles + 1, PEER_STEPS),
        in_specs=[
            pl.BlockSpec((None, tb, D_MODEL), ahead),
            *_table_specs(),
            pl.BlockSpec((tb, N_SLOTS), behind),
            pl.BlockSpec((tb, N_SLOTS), behind),
            pl.BlockSpec((tb, N_SLOTS), behind),
        ],
        out_specs=pl.BlockSpec((tb, N_SLOTS), behind),
        out_shape=jax.ShapeDtypeStruct((n_tiles * tb, N_SLOTS), F32),
        scratch_shapes=[
            pltpu.VMEM((tb, D_MODEL), BF16),
            pltpu.VMEM((tb * hp, N_KEYS), F32),
            pltpu.VMEM((tb * hp, N_KEYS), F32),
            pltpu.VMEM((tb, N_SLOTS), F32),
        ],
        compiler_params=_params(("arbitrary", "arbitrary")),
        name="peer_hidden",
    )(h1, *([u_tab] * TABLE_STREAMS), ia, ib, gate)


def _output_kernel(act_ref, ia_ref, ib_ref, *refs, tb, hp):
    chunks_per_part = OUTPUT_CHUNKS // TABLE_STREAMS
    rows_per_chunk = EXPERTS_PER_STEP // OUTPUT_CHUNKS
    v_refs = refs[:TABLE_STREAMS]
    h1_ref, g_ref, b_ref, out_ref, a0_ref, a1_ref, acc_ref = refs[TABLE_STREAMS:]
    i = pl.program_id(0)
    j = pl.program_id(1)
    group = tb // PEER_STEPS // OUTPUT_CHUNKS
    ia_per_chunk = IA_PER_STEP // OUTPUT_CHUNKS

    @pl.when(jnp.logical_and(i == 0, j == 0))
    def _():
        a1_ref[...] = jnp.zeros_like(a1_ref)

    @pl.when(j == 0)
    def _():
        acc_ref[...] = jnp.zeros_like(acc_ref)

    def step(fill_ref, drain_ref):
        iota = lax.broadcasted_iota(I32, (N_KEYS, N_SLOTS), 0)
        acc = acc_ref[...]
        for c in range(OUTPUT_CHUNKS):
            base = pl.multiple_of((j * OUTPUT_CHUNKS + c) * group, group)
            ia_rows = ia_ref[pl.ds(base, group), :]
            ib_rows = ib_ref[pl.ds(base, group), :]
            act_rows = act_ref[pl.ds(base, group), :]
            sel, act_a = [], []
            for u in range(group):
                sel.append(_one_hot_rows(ib_rows[u:u + 1, :]))
                act_a.append(jnp.where(iota == ia_rows[u:u + 1, :], act_rows[u:u + 1, :], 0.0).astype(BF16))
            dense = [_dot_nt(act_a[u], sel[u]) for u in range(group)]
            pair_base = (j * OUTPUT_CHUNKS + c) * (group // 2)
            for p in range(group // 2):
                packed = pltpu.pack_elementwise([dense[2 * p], dense[2 * p + 1]], packed_dtype=BF16)
                fill_ref[pl.ds(pair_base + p, N_KEYS, stride=hp), :] = lax.bitcast_convert_type(packed, I32)

            tiles = []
            for t in range(ia_per_chunk):
                ia = j * IA_PER_STEP + c * ia_per_chunk + t
                start = pl.multiple_of(ia * hp, SUBLANES)
                tiles.append(pltpu.bitcast(drain_ref[pl.ds(start, tb // 2), :], BF16))
            first = (c % chunks_per_part) * rows_per_chunk
            table_rows = v_refs[c // chunks_per_part][first:first + rows_per_chunk, :]
            acc = acc + _dot(jnp.concatenate(tiles, axis=1), table_rows)
        acc_ref[...] = acc

    parity = lax.rem(i, 2)

    @pl.when(parity == 0)
    def _():
        step(a0_ref, a1_ref)

    @pl.when(parity == 1)
    def _():
        step(a1_ref, a0_ref)

    @pl.when(j == PEER_STEPS - 1)
    def _():
        out_ref[...] = _layer_norm(ALPHA * h1_ref[...] + acc_ref[...], g_ref[...], b_ref[...])


def _peer_output(act, ia, ib, v_tab, h1, g2, b2, *, seq, tb):
    n_tiles = h1.shape[0] * seq // tb
    hp = tb // 2 + SUBLANES
    x_map = _token_tile_map(seq, tb)
    ahead = lambda i, j: (jnp.minimum(i, n_tiles - 1), 0)
    behind = lambda i, j: (jnp.maximum(i - 1, 0), 0)
    const = lambda i, j: (0, 0)
    return pl.pallas_call(
        functools.partial(_output_kernel, tb=tb, hp=hp),
        grid=(n_tiles + 1, PEER_STEPS),
        in_specs=[
            pl.BlockSpec((tb, N_SLOTS), ahead),
            pl.BlockSpec((tb, N_SLOTS), ahead),
            pl.BlockSpec((tb, N_SLOTS), ahead),
            *_table_specs(),
            pl.BlockSpec((None, tb, D_MODEL), lambda i, j: x_map(jnp.maximum(i - 1, 0))),
            pl.BlockSpec((1, D_MODEL), const),
            pl.BlockSpec((1, D_MODEL), const),
        ],
        out_specs=pl.BlockSpec((tb, D_MODEL), behind),
        out_shape=jax.ShapeDtypeStruct((n_tiles * tb, D_MODEL), F32),
        scratch_shapes=[
            pltpu.VMEM((N_KEYS * hp, N_KEYS), I32),
            pltpu.VMEM((N_KEYS * hp, N_KEYS), I32),
            pltpu.VMEM((tb, D_MODEL), F32),
        ],
        compiler_params=_params(("arbitrary", "arbitrary")),
        name="peer_output",
    )(act, ia, ib, *([v_tab] * TABLE_STREAMS), h1, g2, b2)


def _pick_tile(n, candidates):
    for c in candidates:
        if n % c == 0:
            return c
    raise ValueError(f"no tile in {candidates} divides {n}")


def kernel(x, meta, ln0_g, ln0_b, w_in, pool_w, pool_scale, w_br_pool, conv_w, a_log, dt_bias, onorm_g, w_br_delta, w_out, ln1_g, ln1_b, peer_wq, peer_k1, peer_k2, peer_u, peer_v, ln2_g, ln2_b):
    batch, seq, d = x.shape
    assert d == D_MODEL and seq % CHUNK == 0 and w_in.shape[0] == 1
    t_len = seq + CHUNK
    n_tok = batch * seq
    row = lambda p: p.reshape(1, -1).astype(F32)

    meta_b = jnp.broadcast_to(meta[None].astype(x.dtype), (batch, N_META, d))
    h0 = jnp.concatenate([x, jnp.zeros((batch, PAD, d), x.dtype), meta_b], axis=1).reshape(batch * t_len, d)

    w = w_in[0]
    c_pool, c_qkv, c_z = POOL_WIDTH, POOL_WIDTH + 3 * DN_WIDTH, POOL_WIDTH + 4 * DN_WIDTH
    c_b, c_a = c_z + DN_HEADS, c_z + 2 * DN_HEADS
    w_main = jnp.concatenate([w[:, c_pool:c_z], w[:, c_a:], w[:, :c_pool]], axis=1).astype(BF16)
    w_ba = jnp.zeros((d, BA_WIDTH), F32)
    w_ba = w_ba.at[:, :DN_HEADS].set(w[:, c_z:c_b]).at[:, LANES:LANES + DN_HEADS].set(w[:, c_b:c_a]).astype(BF16)
    lane_pad = lambda p: jnp.zeros((1, LANES), F32).at[0, :DN_HEADS].set(p.astype(F32))

    proj, ba = _inproj(h0, row(ln0_g), row(ln0_b), w_main, w_ba, seq=seq, t_len=t_len)
    ya = _pool_branch(proj, pool_w[0].astype(BF16), row(pool_scale[0]), w_br_pool[0].astype(BF16),
                      batch=batch, seq=seq, t_len=t_len)
    qkvn = _conv_qkv(proj, conv_w[0].astype(F32), batch=batch, seq=seq, t_len=t_len)
    o = _delta_rule(qkvn, ba, proj, lane_pad(-jnp.exp(a_log[0].astype(F32))), lane_pad(dt_bias[0]),
                    row(onorm_g[0]), batch=batch, t_len=t_len)
    h1 = _merge(o, ya, proj, h0, w_br_delta[0].astype(BF16), w_out[0].astype(BF16), row(ln0_g), row(ln0_b),
                row(ln1_g[0]), row(ln1_b[0]), batch=batch, seq=seq, t_len=t_len)
    h1 = h1.reshape(batch, t_len, d)

    idx, gate = _peer_query(h1, peer_wq[0].astype(BF16), peer_k1[0].astype(BF16), peer_k2[0].astype(BF16),
                            seq=seq, tq=LANES)
    idx = idx.reshape(N_SLOTS, n_tok).T
    gate = gate.reshape(N_SLOTS, n_tok).T
    ia = lax.shift_right_logical(idx, 7)
    ib = lax.bitwise_and(idx, N_KEYS - 1)

    act = _peer_hidden(h1, peer_u[0].astype(BF16), ia, ib, gate, seq=seq, tb=_pick_tile(seq, (256,)))
    out = _peer_output(act, ia, ib, peer_v[0].astype(BF16), h1, row(ln2_g[0]), row(ln2_b[0]),
                       seq=seq, tb=_pick_tile(seq, (512,)))
    return out.reshape(batch, seq, d)
```

```python
import functools
import math

import jax
import jax.numpy as jnp
from jax import lax
from jax.experimental import pallas as pl
from jax.experimental.pallas import tpu as pltpu

F32 = jnp.float32
BF16 = jnp.bfloat16
I32 = jnp.int32

D_MODEL = 1024
N_META = 16
CHUNK = 64
PAD = CHUNK - N_META
POOL_GROUPS = 4
POOL_GROUP_DIM = 128
POOL_WIDTH = POOL_GROUPS * POOL_GROUP_DIM
POOL_WINDOWS = (2, 4, 8, 16)
POOL_HALO = 16
DN_HEADS = 8
DN_HEAD_DIM = 128
DN_WIDTH = DN_HEADS * DN_HEAD_DIM
CONV_WIDTH = 4
PEER_HEADS = 8
PEER_HALF = 128
N_KEYS = 128
PEER_TOPK = 16
N_SLOTS = PEER_HEADS * PEER_TOPK
LN_EPS = 1e-5
RMS_EPS = 1e-6
L2_EPS = 1e-6
ALPHA = 2.0 ** 0.25

COL_QKV = 0
COL_Z = 3 * DN_WIDTH
COL_GP = COL_Z + DN_WIDTH
COL_GD = COL_GP + D_MODEL
COL_POOL = COL_GD + D_MODEL
N_PROJ = COL_POOL + POOL_WIDTH
BA_WIDTH = 256

SUBLANES = 8
LANES = 128
VMEM_LIMIT = 60 * 1024 * 1024

_CAND = [(a, b) for a in range(PEER_TOPK) for b in range(PEER_TOPK) if (a + 1) * (b + 1) <= PEER_TOPK]


def _params(sem):
    return pltpu.CompilerParams(dimension_semantics=sem, vmem_limit_bytes=VMEM_LIMIT)


def _layer_norm(x, g, b):
    mu = jnp.mean(x, axis=-1, keepdims=True)
    xc = x - mu
    var = jnp.mean(xc * xc, axis=-1, keepdims=True)
    return xc * lax.rsqrt(var + LN_EPS) * g + b


def _zero_pad_rows(y, row, seq):
    return jnp.where(row >= seq, jnp.where(row < seq + PAD, 0.0, y), y)


def _sigmoid(x):
    return 1.0 / (1.0 + jnp.exp(-x))


def _dot(a, b):
    return jnp.dot(a, b, preferred_element_type=F32)


def _dot_nt(a, b):
    return lax.dot_general(a, b, (((1,), (1,)), ((), ())), preferred_element_type=F32)


def _inproj_kernel(x_ref, g_ref, b_ref, w_ref, wba_ref, proj_ref, ba_ref, xn_ref, *, tiles_per_batch, seq):
    i = pl.program_id(0)
    j = pl.program_id(1)
    tm = x_ref.shape[0]

    @pl.when(j == 0)
    def _():
        y = _layer_norm(x_ref[...], g_ref[...], b_ref[...])
        row = lax.broadcasted_iota(I32, (tm, 1), 0) + lax.rem(i, tiles_per_batch) * tm
        xn = _zero_pad_rows(y, row, seq).astype(BF16)
        xn_ref[...] = xn
        ba_ref[...] = _dot(xn, wba_ref[...])

    proj_ref[...] = _dot(xn_ref[...], w_ref[...]).astype(proj_ref.dtype)


def _inproj(h0, ln_g, ln_b, w_main, w_ba, *, seq, t_len):
    rows = h0.shape[0]
    tiles_per_batch = 4
    tm = t_len // tiles_per_batch
    tn = N_PROJ // 4
    kern = functools.partial(_inproj_kernel, tiles_per_batch=tiles_per_batch, seq=seq)
    return pl.pallas_call(
        kern,
        grid=(rows // tm, N_PROJ // tn),
        in_specs=[
            pl.BlockSpec((tm, D_MODEL), lambda i, j: (i, 0)),
            pl.BlockSpec((1, D_MODEL), lambda i, j: (0, 0)),
            pl.BlockSpec((1, D_MODEL), lambda i, j: (0, 0)),
            pl.BlockSpec((D_MODEL, tn), lambda i, j: (0, j)),
            pl.BlockSpec((D_MODEL, BA_WIDTH), lambda i, j: (0, 0)),
        ],
        out_specs=[
            pl.BlockSpec((tm, tn), lambda i, j: (i, j)),
            pl.BlockSpec((tm, BA_WIDTH), lambda i, j: (i, 0)),
        ],
        out_shape=[
            jax.ShapeDtypeStruct((rows, N_PROJ), BF16),
            jax.ShapeDtypeStruct((rows, BA_WIDTH), F32),
        ],
        scratch_shapes=[pltpu.VMEM((tm, D_MODEL), BF16)],
        compiler_params=_params(("arbitrary", "arbitrary")),
        name="inproj",
    )(h0, ln_g, ln_b, w_main, w_ba)


def _pool_kernel(xp_ref, halo_ref, pw_ref, ps_ref, wbp_ref, ya_ref, *, seq):
    i = pl.program_id(1)
    tm = xp_ref.shape[0]
    v = jnp.concatenate([halo_ref[...].astype(F32), xp_ref[...].astype(F32)], axis=0)
    row = lax.broadcasted_iota(I32, (tm, POOL_GROUP_DIM), 0) + i * tm
    meta_pos = row - (seq + PAD)
    outs = []
    for gi, w in enumerate(POOL_WINDOWS):
        vg = v[:, gi * POOL_GROUP_DIM:(gi + 1) * POOL_GROUP_DIM]
        s = vg
        shift = 1
        while shift < w:
            s = s + pltpu.roll(s, shift, axis=0)
            shift *= 2
        s = s[POOL_HALO:]
        xg = vg[POOL_HALO:]
        count = jnp.where(meta_pos >= 0, jnp.minimum(meta_pos + 1, w), w).astype(F32)
        pooled = s / count - xg
        mixed = _dot(pooled.astype(BF16), pw_ref[gi])
        outs.append(mixed)
    y_pool = jnp.concatenate(outs, axis=1) * ps_ref[...]
    ya_ref[...] = _dot(y_pool.astype(BF16), wbp_ref[...])


def _pool_branch(proj, pool_w, pool_scale, w_br_pool, *, batch, seq, t_len):
    rows = proj.shape[0]
    tiles_per_batch = 4
    tm = t_len // tiles_per_batch
    halo_blocks_per_batch = t_len // POOL_HALO
    halo_blocks_per_tile = tm // POOL_HALO
    pool_col = COL_POOL // POOL_WIDTH

    def halo_map(b, i):
        prev = lax.rem(i * halo_blocks_per_tile + halo_blocks_per_batch - 1, halo_blocks_per_batch)
        return (b * halo_blocks_per_batch + prev, pool_col)

    return pl.pallas_call(
        functools.partial(_pool_kernel, seq=seq),
        grid=(batch, tiles_per_batch),
        in_specs=[
            pl.BlockSpec((tm, POOL_WIDTH), lambda b, i: (b * tiles_per_batch + i, pool_col)),
            pl.BlockSpec((POOL_HALO, POOL_WIDTH), halo_map),
            pl.BlockSpec((POOL_GROUPS, POOL_GROUP_DIM, POOL_GROUP_DIM), lambda b, i: (0, 0, 0)),
            pl.BlockSpec((1, POOL_WIDTH), lambda b, i: (0, 0)),
            pl.BlockSpec((POOL_WIDTH, D_MODEL), lambda b, i: (0, 0)),
        ],
        out_specs=pl.BlockSpec((tm, D_MODEL), lambda b, i: (b * tiles_per_batch + i, 0)),
        out_shape=jax.ShapeDtypeStruct((rows, D_MODEL), F32),
        compiler_params=_params(("arbitrary", "arbitrary")),
        name="pool_branch",
    )(proj, proj, pool_w, pool_scale, w_br_pool)


def _conv_kernel(x_ref, w_ref, o_ref, *, seq):
    s = pl.program_id(1)
    x = x_ref[...].astype(F32)
    t_len = x.shape[0]
    w = w_ref[...]
    y = x * w[CONV_WIDTH - 1:CONV_WIDTH, :]
    for lag in range(1, CONV_WIDTH):
        y = y + pltpu.roll(x, lag, axis=0) * w[CONV_WIDTH - 1 - lag:CONV_WIDTH - lag, :]
    y = y * _sigmoid(y)
    ss = jnp.sum(y * y, axis=-1, keepdims=True)
    q_scale = jnp.where(s < DN_HEADS, DN_HEAD_DIM ** -0.5, 1.0).astype(F32)
    fac = jnp.where(s < 2 * DN_HEADS, lax.rsqrt(ss + L2_EPS) * q_scale, 1.0)
    row = lax.broadcasted_iota(I32, (t_len, 1), 0)
    o_ref[...] = _zero_pad_rows(y * fac, row, seq).astype(o_ref.dtype)


def _conv_qkv(proj, conv_w, *, batch, seq, t_len):
    rows = proj.shape[0]
    n_slabs = 3 * DN_HEADS
    return pl.pallas_call(
        functools.partial(_conv_kernel, seq=seq),
        grid=(batch, n_slabs),
        in_specs=[
            pl.BlockSpec((t_len, DN_HEAD_DIM), lambda b, s: (b, COL_QKV // DN_HEAD_DIM + s)),
            pl.BlockSpec((CONV_WIDTH, DN_HEAD_DIM), lambda b, s: (0, s)),
        ],
        out_specs=pl.BlockSpec((t_len, DN_HEAD_DIM), lambda b, s: (b, s)),
        out_shape=jax.ShapeDtypeStruct((rows, 3 * DN_WIDTH), BF16),
        compiler_params=_params(("arbitrary", "arbitrary")),
        name="conv_qkv",
    )(proj, conv_w)


def _inv_unit_lower(a, masks, eye):
    m8, m16, m32, m64 = masks
    n = [x * m8 for x in a]
    n2 = [_dot(x, x) for x in n]
    n4 = [_dot(x, x) for x in n2]
    t = [eye - x for x in n]
    t = [x + _dot(x, y) for x, y in zip(t, n2)]
    t = [x + _dot(x, y) for x, y in zip(t, n4)]
    for m in (m16, m32, m64):
        at = [_dot(x * m, y) for x, y in zip(a, t)]
        t = [x - _dot(x, y) for x, y in zip(t, at)]
    return t


DELTA_STREAMS = 2


def _delta_kernel(q_ref, k_ref, v_ref, ba_ref, z_ref, nega_ref, dtb_ref, og_ref, o_ref, state_ref):
    n = DELTA_STREAMS
    c = pl.program_id(1)

    @pl.when(c == 0)
    def _():
        state_ref[...] = jnp.zeros_like(state_ref)

    r = lax.broadcasted_iota(I32, (CHUNK, CHUNK), 0)
    cc = lax.broadcasted_iota(I32, (CHUNK, CHUNK), 1)
    causal = r >= cc
    strict = r > cc
    eye = jnp.where(r == cc, 1.0, 0.0).astype(F32)
    tril = jnp.where(causal, 1.0, 0.0).astype(F32)

    def blk(x, s):
        return lax.shift_right_logical(x, s)

    lower = jnp.where(strict, 1.0, 0.0).astype(F32)
    m8 = jnp.where(blk(r, 3) == blk(cc, 3), lower, 0.0)
    m16 = jnp.where((blk(r, 4) == blk(cc, 4)) & (blk(r, 3) != blk(cc, 3)), lower, 0.0)
    m32 = jnp.where((blk(r, 5) == blk(cc, 5)) & (blk(r, 4) != blk(cc, 4)), lower, 0.0)
    m64 = jnp.where(blk(r, 5) != blk(cc, 5), lower, 0.0)
    masks = (m8, m16, m32, m64)

    beta_all, gcum, gcum_t = [], [], []
    for s in range(n):
        ba = ba_ref[s]
        beta_all.append(_sigmoid(ba[:, :LANES]))
        x = ba[:, LANES:] + dtb_ref[...]
        softplus = jnp.maximum(x, 0.0) + jnp.log(1.0 + jnp.exp(-jnp.abs(x)))
        g_all = nega_ref[...] * softplus
        gcum.append(lax.dot_general(tril, g_all, (((1,), (0,)), ((), ())), precision=lax.Precision.HIGHEST,
                                    preferred_element_type=F32))
        gcum_t.append(gcum[s].T)

    chains = [(s, h) for s in range(n) for h in range(DN_HEADS)]
    ids = range(len(chains))
    sl = [slice(h * DN_HEAD_DIM, (h + 1) * DN_HEAD_DIM) for _, h in chains]
    q = [q_ref[s, :, sl[i]].astype(F32) for i, (s, h) in enumerate(chains)]
    k = [k_ref[s, :, sl[i]].astype(F32) for i, (s, h) in enumerate(chains)]
    v = [v_ref[s, :, sl[i]].astype(F32) for i, (s, h) in enumerate(chains)]
    beta = [beta_all[s][:, h:h + 1] for s, h in chains]
    gc = [gcum[s][:, h:h + 1] for s, h in chains]
    gr = [gcum_t[s][h:h + 1, :] for s, h in chains]
    g_last = [gcum[s][CHUNK - 1:CHUNK, h:h + 1] for s, h in chains]
    decay = [jnp.where(causal, jnp.exp(jnp.where(causal, gc[i] - gr[i], 0.0)), 0.0) for i in ids]
    eg = [jnp.exp(gc[i]) for i in ids]
    kb = [k[i] * beta[i] for i in ids]
    kk = [_dot_nt(kb[i], k[i]) for i in ids]
    qk = [_dot_nt(q[i], k[i]) for i in ids]
    a_mat = [jnp.where(strict, kk[i] * decay[i], 0.0) for i in ids]
    t_inv = _inv_unit_lower(a_mat, masks, eye)
    rhs = [jnp.concatenate([v[i] * beta[i], kb[i] * eg[i]], axis=1) for i in ids]
    uw = [_dot(t_inv[i], rhs[i]) for i in ids]
    s_prev = [state_ref[i] for i in ids]
    wq = [jnp.concatenate([uw[i][:, DN_HEAD_DIM:], q[i] * eg[i]], axis=0) for i in ids]
    ws = [_dot(wq[i], s_prev[i]) for i in ids]
    v_new = [uw[i][:, :DN_HEAD_DIM] - ws[i][:CHUNK] for i in ids]
    attn = [qk[i] * decay[i] for i in ids]
    intra = [_dot(attn[i], v_new[i]) for i in ids]
    k_dec_t = [(k[i] * jnp.exp(g_last[i] - gc[i])).T for i in ids]
    kv = [_dot(k_dec_t[i], v_new[i]) for i in ids]
    for i, (s, h) in enumerate(chains):
        state_ref[i] = s_prev[i] * jnp.exp(g_last[i]) + kv[i]
        out = ws[i][CHUNK:] + intra[i]
        out = out * lax.rsqrt(jnp.mean(out * out, axis=-1, keepdims=True) + RMS_EPS) * og_ref[...]
        zh = z_ref[s, :, sl[i]].astype(F32)
        o_ref[s, :, sl[i]] = (out * (zh * _sigmoid(zh))).astype(o_ref.dtype)


def _delta_rule(qkvn, ba, proj, neg_a, dt_bias, onorm_g, *, batch, t_len):
    rows = qkvn.shape[0]
    n_chunks = t_len // CHUNK

    n = DELTA_STREAMS
    assert batch % n == 0
    groups = batch // n
    by_stream = lambda a: a.reshape(groups, n, t_len, a.shape[-1])

    def chunk_block(col):
        return lambda g, c: (g, 0, lax.rem(c + n_chunks - 1, n_chunks), col)

    const = lambda g, c: (0, 0)
    out = pl.pallas_call(
        _delta_kernel,
        grid=(groups, n_chunks),
        in_specs=[
            pl.BlockSpec((None, n, CHUNK, DN_WIDTH), chunk_block(0)),
            pl.BlockSpec((None, n, CHUNK, DN_WIDTH), chunk_block(1)),
            pl.BlockSpec((None, n, CHUNK, DN_WIDTH), chunk_block(2)),
            pl.BlockSpec((None, n, CHUNK, BA_WIDTH), chunk_block(0)),
            pl.BlockSpec((None, n, CHUNK, DN_WIDTH), chunk_block(COL_Z // DN_WIDTH)),
            pl.BlockSpec((1, LANES), const),
            pl.BlockSpec((1, LANES), const),
            pl.BlockSpec((1, DN_HEAD_DIM), const),
        ],
        out_specs=pl.BlockSpec((None, n, CHUNK, DN_WIDTH), chunk_block(0)),
        out_shape=jax.ShapeDtypeStruct((groups, n, t_len, DN_WIDTH), BF16),
        scratch_shapes=[pltpu.VMEM((n * DN_HEADS, DN_HEAD_DIM, DN_HEAD_DIM), F32)],
        compiler_params=_params(("arbitrary", "arbitrary")),
        name="delta_rule",
    )(by_stream(qkvn), by_stream(qkvn), by_stream(qkvn), by_stream(ba), by_stream(proj), neg_a, dt_bias, onorm_g)
    return out.reshape(rows, DN_WIDTH)


def _merge_kernel(o_ref, ya_ref, gp_ref, gd_ref, h0_ref, wd_ref, wo_ref, g0_ref, b0_ref, g1_ref, b1_ref, h1_ref):
    y_b = _dot(o_ref[...], wd_ref[...])
    merged = _sigmoid(gp_ref[...].astype(F32)) * ya_ref[...] + _sigmoid(gd_ref[...].astype(F32)) * y_b
    mix = _dot(merged.astype(BF16), wo_ref[...])
    h = _layer_norm(h0_ref[...], g0_ref[...], b0_ref[...])
    h1_ref[...] = _layer_norm(ALPHA * h + mix, g1_ref[...], b1_ref[...])


def _merge(o, ya, proj, h0, w_br_delta, w_out, g0, b0, g1, b1, *, batch, seq, t_len):
    bf16_rows = 2 * SUBLANES
    tiles_per_batch = next(n for n in (10, 8, 4, 2, 1) if t_len % (n * bf16_rows) == 0)
    tm = t_len // tiles_per_batch
    row = lambda i: (i, 0)
    const = lambda i: (0, 0)
    return pl.pallas_call(
        _merge_kernel,
        grid=(batch * tiles_per_batch,),
        in_specs=[
            pl.BlockSpec((tm, DN_WIDTH), row),
            pl.BlockSpec((tm, D_MODEL), row),
            pl.BlockSpec((tm, D_MODEL), lambda i: (i, COL_GP // D_MODEL)),
            pl.BlockSpec((tm, D_MODEL), lambda i: (i, COL_GD // D_MODEL)),
            pl.BlockSpec((tm, D_MODEL), row),
            pl.BlockSpec((DN_WIDTH, D_MODEL), const),
            pl.BlockSpec((D_MODEL, D_MODEL), const),
            pl.BlockSpec((1, D_MODEL), const),
            pl.BlockSpec((1, D_MODEL), const),
            pl.BlockSpec((1, D_MODEL), const),
            pl.BlockSpec((1, D_MODEL), const),
        ],
        out_specs=pl.BlockSpec((tm, D_MODEL), row),
        out_shape=jax.ShapeDtypeStruct((batch * t_len, D_MODEL), F32),
        compiler_params=_params(("arbitrary",)),
        name="merge",
    )(o, ya, proj, proj, h0, w_br_delta, w_out, g0, b0, g1, b1)


def _tile(ref, k):
    return ref[k * SUBLANES:(k + 1) * SUBLANES, :]


def _argmax_tournament(vals, ids):
    nodes = list(zip(vals, ids))
    while len(nodes) > 1:
        nxt = []
        for p in range(0, len(nodes) - 1, 2):
            (va, ia), (vb, ib) = nodes[p], nodes[p + 1]
            first = va >= vb
            nxt.append((jnp.maximum(va, vb), jnp.where(first, ia, ib)))
        if len(nodes) % 2:
            nxt.append(nodes[-1])
        nodes = nxt
    return nodes[0]


def _query_kernel(x_ref, wq_ref, k1_ref, k2_ref, idx_ref, gate_ref, s1_ref, s2_ref, m1_ref, i1_ref, m2_ref, i2_ref,
                  c_ref, e_ref, sc_ref):
    xb = x_ref[...].astype(BF16)
    q = _dot(xb, wq_ref[...])
    for h in range(PEER_HEADS):
        q1 = q[:, (2 * h) * PEER_HALF:(2 * h + 1) * PEER_HALF].astype(BF16)
        q2 = q[:, (2 * h + 1) * PEER_HALF:(2 * h + 2) * PEER_HALF].astype(BF16)
        s1_ref[pl.ds(h, N_KEYS, stride=SUBLANES), :] = _dot_nt(k1_ref[h], q1)
        s2_ref[pl.ds(h, N_KEYS, stride=SUBLANES), :] = _dot_nt(k2_ref[h], q2)

    neg_inf = jnp.float32(-jnp.inf)

    def sub_round(r, carry):
        for s_ref, m_ref, i_ref in ((s1_ref, m1_ref, i1_ref), (s2_ref, m2_ref, i2_ref)):
            vals = [_tile(s_ref, k) for k in range(N_KEYS)]
            best, arg = _argmax_tournament(vals, list(range(N_KEYS)))
            m_ref[r] = best
            i_ref[r] = arg
            for k in range(N_KEYS):
                s_ref[k * SUBLANES:(k + 1) * SUBLANES, :] = jnp.where(arg == k, neg_inf, vals[k])
        return carry

    lax.fori_loop(0, PEER_TOPK, sub_round, 0)

    for j, (a, b) in enumerate(_CAND):
        c_ref[j * SUBLANES:(j + 1) * SUBLANES, :] = m1_ref[a] + m2_ref[b]
        e_ref[j * SUBLANES:(j + 1) * SUBLANES, :] = i1_ref[a] * N_KEYS + i2_ref[b]

    def pair_round(r, carry):
        vals = [_tile(c_ref, j) for j in range(len(_CAND))]
        ids = [_tile(e_ref, j) for j in range(len(_CAND))]
        best, arg = _argmax_tournament(vals, ids)
        sc_ref[r] = best
        idx_ref[r] = arg
        for j in range(len(_CAND)):
            c_ref[j * SUBLANES:(j + 1) * SUBLANES, :] = jnp.where(ids[j] == arg, neg_inf, vals[j])
        return carry

    lax.fori_loop(0, PEER_TOPK, pair_round, 0)

    sc = sc_ref[...]
    ex = jnp.exp(sc - sc[0:1])
    gate_ref[...] = ex / jnp.sum(ex, axis=0, keepdims=True)


def _token_tile_map(seq, tile):
    per_batch = seq // tile
    return lambda i, *_: (i // per_batch, lax.rem(i, per_batch), 0)


def _peer_query(h1, wq, k1, k2, *, seq, tq):
    n_tok = h1.shape[0] * seq
    slot_shape = (PEER_TOPK, PEER_HEADS, tq)
    return pl.pallas_call(
        _query_kernel,
        grid=(n_tok // tq,),
        in_specs=[
            pl.BlockSpec((None, tq, D_MODEL), _token_tile_map(seq, tq)),
            pl.BlockSpec((D_MODEL, 2 * PEER_HALF * PEER_HEADS), lambda i: (0, 0)),
            pl.BlockSpec((PEER_HEADS, N_KEYS, PEER_HALF), lambda i: (0, 0, 0)),
            pl.BlockSpec((PEER_HEADS, N_KEYS, PEER_HALF), lambda i: (0, 0, 0)),
        ],
        out_specs=[
            pl.BlockSpec(slot_shape, lambda i: (0, 0, i)),
            pl.BlockSpec(slot_shape, lambda i: (0, 0, i)),
        ],
        out_shape=[
            jax.ShapeDtypeStruct((PEER_TOPK, PEER_HEADS, n_tok), I32),
            jax.ShapeDtypeStruct((PEER_TOPK, PEER_HEADS, n_tok), F32),
        ],
        scratch_shapes=[
            pltpu.VMEM((N_KEYS * SUBLANES, tq), F32),
            pltpu.VMEM((N_KEYS * SUBLANES, tq), F32),
            pltpu.VMEM(slot_shape, F32),
            pltpu.VMEM(slot_shape, I32),
            pltpu.VMEM(slot_shape, F32),
            pltpu.VMEM(slot_shape, I32),
            pltpu.VMEM((len(_CAND) * SUBLANES, tq), F32),
            pltpu.VMEM((len(_CAND) * SUBLANES, tq), I32),
            pltpu.VMEM(slot_shape, F32),
        ],
        compiler_params=_params(("arbitrary",)),
        name="peer_query",
    )(h1, wq, k1, k2)


IA_PER_STEP = 16
EXPERTS_PER_STEP = IA_PER_STEP * N_KEYS
PEER_STEPS = (N_KEYS * N_KEYS) // EXPERTS_PER_STEP
HIDDEN_CHUNKS = 8
OUTPUT_CHUNKS = 8
TABLE_STREAMS = 4


def _table_specs():
    rows = EXPERTS_PER_STEP // TABLE_STREAMS
    return [pl.BlockSpec((rows, D_MODEL), lambda i, j, p=p: (j * TABLE_STREAMS + p, 0)) for p in range(TABLE_STREAMS)]


def _one_hot_rows(idx_row):
    iota = lax.broadcasted_iota(I32, (N_KEYS, N_SLOTS), 0)
    return jnp.where(iota == idx_row, 1.0, 0.0).astype(BF16)


def _hidden_kernel(x_ref, *refs, tb, hp):
    u_refs = refs[:TABLE_STREAMS]
    ia_ref, ib_ref, gate_ref, act_ref, xb_ref, h0_ref, h1_ref, hs_ref = refs[TABLE_STREAMS:]
    chunks_per_part = HIDDEN_CHUNKS // TABLE_STREAMS
    rows_per_chunk = EXPERTS_PER_STEP // HIDDEN_CHUNKS
    i = pl.program_id(0)
    j = pl.program_id(1)
    group = tb // PEER_STEPS // HIDDEN_CHUNKS
    pairs_per_chunk = IA_PER_STEP // HIDDEN_CHUNKS // 2

    @pl.when(jnp.logical_and(i == 0, j == 0))
    def _():
        h1_ref[...] = jnp.zeros_like(h1_ref)

    @pl.when(j == 0)
    def _():
        xb_ref[...] = x_ref[...].astype(BF16)

    def step(fill_ref, drain_ref):
        iota = lax.broadcasted_iota(I32, (N_KEYS, N_SLOTS), 0)
        for c in range(HIDDEN_CHUNKS):
            base = pl.multiple_of((j * HIDDEN_CHUNKS + c) * group, group)
            ia_rows = ia_ref[pl.ds(base, group), :]
            ib_rows = ib_ref[pl.ds(base, group), :]
            lhs, sel = [], []
            for u in range(group):
                sel.append(_one_hot_rows(ib_rows[u:u + 1, :]))
                start = pl.multiple_of((base + u) * hp, SUBLANES)
                lhs.append(pltpu.bitcast(drain_ref[pl.ds(start, N_KEYS // 2), :], BF16))
            by_slot = [_dot(lhs[u], sel[u]) for u in range(group)]
            rows = []
            for u in range(group):
                picked = jnp.where(iota == ia_rows[u:u + 1, :], by_slot[u], 0.0)
                rows.append(jnp.sum(picked, axis=0, keepdims=True))
            hs_ref[pl.ds(base, group), :] = jnp.concatenate(rows, axis=0)

            first = (c % chunks_per_part) * rows_per_chunk
            experts = u_refs[c // chunks_per_part][first:first + rows_per_chunk, :]
            hc = _dot_nt(xb_ref[...], experts)
            for m in range(pairs_per_chunk):
                even = hc[:, (2 * m) * N_KEYS:(2 * m + 1) * N_KEYS]
                odd = hc[:, (2 * m + 1) * N_KEYS:(2 * m + 2) * N_KEYS]
                packed = pltpu.pack_elementwise([even, odd], packed_dtype=BF16)
                pair = (j * IA_PER_STEP) // 2 + c * pairs_per_chunk + m
                fill_ref[pl.ds(pair, tb, stride=hp), :] = lax.bitcast_convert_type(packed, I32)

    parity = lax.rem(i, 2)

    @pl.when(parity == 0)
    def _():
        step(h0_ref, h1_ref)

    @pl.when(parity == 1)
    def _():
        step(h1_ref, h0_ref)

    @pl.when(j == PEER_STEPS - 1)
    def _():
        hid = hs_ref[...]
        gelu = 0.5 * hid * (1.0 + lax.erf(hid * (2.0 ** -0.5)))
        act_ref[...] = gelu * gate_ref[...]


def _peer_hidden(h1, u_tab, ia, ib, gate, *, seq, tb):
    n_tiles = h1.shape[0] * seq // tb
    hp = N_KEYS // 2 + SUBLANES
    x_map = _token_tile_map(seq, tb)
    ahead = lambda i, j: x_map(jnp.minimum(i, n_tiles - 1))
    behind = lambda i, j: (jnp.maximum(i - 1, 0), 0)
    return pl.pallas_call(
        functools.partial(_hidden_kernel, tb=tb, hp=hp),
        grid=(n_tiles + 1, PEER_STEPS),
        in_specs=[
            pl.BlockSpec((None, tb, D_MODEL), ahead),
            *_table_specs(),
            pl.BlockSpec((tb, N_SLOTS), behind),
            pl.BlockSpec((tb, N_SLOTS), behind),
            pl.BlockSpec((tb, N_SLOTS), behind),
        ],
        out_specs=pl.BlockSpec((tb, N_SLOTS), behind),
        out_shape=jax.ShapeDtypeStruct((n_tiles * tb, N_SLOTS), F32),
        scratch_shapes=[
            pltpu.VMEM((tb, D_MODEL), BF16),
            pltpu.VMEM((tb * hp, N_KEYS), I32),
            pltpu.VMEM((tb * hp, N_KEYS), I32),
            pltpu.VMEM((tb, N_SLOTS), F32),
        ],
        compiler_params=_params(("arbitrary", "arbitrary")),
        name="peer_hidden",
    )(h1, *([u_tab] * TABLE_STREAMS), ia, ib, gate)


def _output_kernel(act_ref, ia_ref, ib_ref, *refs, tb, hp):
    chunks_per_part = OUTPUT_CHUNKS // TABLE_STREAMS
    rows_per_chunk = EXPERTS_PER_STEP // OUTPUT_CHUNKS
    v_refs = refs[:TABLE_STREAMS]
    h1_ref, g_ref, b_ref, out_ref, a0_ref, a1_ref, acc_ref = refs[TABLE_STREAMS:]
    i = pl.program_id(0)
    j = pl.program_id(1)
    group = tb // PEER_STEPS // OUTPUT_CHUNKS
    ia_per_chunk = IA_PER_STEP // OUTPUT_CHUNKS

    @pl.when(jnp.logical_and(i == 0, j == 0))
    def _():
        a1_ref[...] = jnp.zeros_like(a1_ref)

    @pl.when(j == 0)
    def _():
        acc_ref[...] = jnp.zeros_like(acc_ref)

    def step(fill_ref, drain_ref):
        iota = lax.broadcasted_iota(I32, (N_KEYS, N_SLOTS), 0)
        acc = acc_ref[...]
        for c in range(OUTPUT_CHUNKS):
            base = pl.multiple_of((j * OUTPUT_CHUNKS + c) * group, group)
            ia_rows = ia_ref[pl.ds(base, group), :]
            ib_rows = ib_ref[pl.ds(base, group), :]
            act_rows = act_ref[pl.ds(base, group), :]
            sel, act_a = [], []
            for u in range(group):
                sel.append(_one_hot_rows(ib_rows[u:u + 1, :]))
                act_a.append(jnp.where(iota == ia_rows[u:u + 1, :], act_rows[u:u + 1, :], 0.0).astype(BF16))
            dense = [_dot_nt(act_a[u], sel[u]) for u in range(group)]
            pair_base = (j * OUTPUT_CHUNKS + c) * (group // 2)
            for p in range(group // 2):
                packed = pltpu.pack_elementwise([dense[2 * p], dense[2 * p + 1]], packed_dtype=BF16)
                fill_ref[pl.ds(pair_base + p, N_KEYS, stride=hp), :] = lax.bitcast_convert_type(packed, I32)

            tiles = []
            for t in range(ia_per_chunk):
                ia = j * IA_PER_STEP + c * ia_per_chunk + t
                start = pl.multiple_of(ia * hp, SUBLANES)
                tiles.append(pltpu.bitcast(drain_ref[pl.ds(start, tb // 2), :], BF16))
            first = (c % chunks_per_part) * rows_per_chunk
            table_rows = v_refs[c // chunks_per_part][first:first + rows_per_chunk, :]
            acc = acc + _dot(jnp.concatenate(tiles, axis=1), table_rows)
        acc_ref[...] = acc

    parity = lax.rem(i, 2)

    @pl.when(parity == 0)
    def _():
        step(a0_ref, a1_ref)

    @pl.when(parity == 1)
    def _():
        step(a1_ref, a0_ref)

    @pl.when(j == PEER_STEPS - 1)
    def _():
        out_ref[...] = _layer_norm(ALPHA * h1_ref[...] + acc_ref[...], g_ref[...], b_ref[...])


def _peer_output(act, ia, ib, v_tab, h1, g2, b2, *, seq, tb):
    n_tiles = h1.shape[0] * seq // tb
    hp = tb // 2 + SUBLANES
    x_map = _token_tile_map(seq, tb)
    ahead = lambda i, j: (jnp.minimum(i, n_tiles - 1), 0)
    behind = lambda i, j: (jnp.maximum(i - 1, 0), 0)
    const = lambda i, j: (0, 0)
    return pl.pallas_call(
        functools.partial(_output_kernel, tb=tb, hp=hp),
        grid=(n_tiles + 1, PEER_STEPS),
        in_specs=[
            pl.BlockSpec((tb, N_SLOTS), ahead),
            pl.BlockSpec((tb, N_SLOTS), ahead),
            pl.BlockSpec((tb, N_SLOTS), ahead),
            *_table_specs(),
            pl.BlockSpec((None, tb, D_MODEL), lambda i, j: x_map(jnp.maximum(i - 1, 0))),
            pl.BlockSpec((1, D_MODEL), const),
            pl.BlockSpec((1, D_MODEL), const),
        ],
        out_specs=pl.BlockSpec((tb, D_MODEL), behind),
        out_shape=jax.ShapeDtypeStruct((n_tiles * tb, D_MODEL), F32),
        scratch_shapes=[
            pltpu.VMEM((N_KEYS * hp, N_KEYS), I32),
            pltpu.VMEM((N_KEYS * hp, N_KEYS), I32),
            pltpu.VMEM((tb, D_MODEL), F32),
        ],
        compiler_params=_params(("arbitrary", "arbitrary")),
        name="peer_output",
    )(act, ia, ib, *([v_tab] * TABLE_STREAMS), h1, g2, b2)


def _pick_tile(n, candidates):
    for c in candidates:
        if n % c == 0:
            return c
    raise ValueError(f"no tile in {candidates} divides {n}")


def kernel(x, meta, ln0_g, ln0_b, w_in, pool_w, pool_scale, w_br_pool, conv_w, a_log, dt_bias, onorm_g, w_br_delta, w_out, ln1_g, ln1_b, peer_wq, peer_k1, peer_k2, peer_u, peer_v, ln2_g, ln2_b):
    batch, seq, d = x.shape
    assert d == D_MODEL and seq % CHUNK == 0 and w_in.shape[0] == 1
    t_len = seq + CHUNK
    n_tok = batch * seq
    row = lambda p: p.reshape(1, -1).astype(F32)

    meta_b = jnp.broadcast_to(meta[None].astype(x.dtype), (batch, N_META, d))
    h0 = jnp.concatenate([x, jnp.zeros((batch, PAD, d), x.dtype), meta_b], axis=1).reshape(batch * t_len, d)

    w = w_in[0]
    c_pool, c_qkv, c_z = POOL_WIDTH, POOL_WIDTH + 3 * DN_WIDTH, POOL_WIDTH + 4 * DN_WIDTH
    c_b, c_a = c_z + DN_HEADS, c_z + 2 * DN_HEADS
    w_main = jnp.concatenate([w[:, c_pool:c_z], w[:, c_a:], w[:, :c_pool]], axis=1).astype(BF16)
    w_ba = jnp.zeros((d, BA_WIDTH), F32)
    w_ba = w_ba.at[:, :DN_HEADS].set(w[:, c_z:c_b]).at[:, LANES:LANES + DN_HEADS].set(w[:, c_b:c_a]).astype(BF16)
    lane_pad = lambda p: jnp.zeros((1, LANES), F32).at[0, :DN_HEADS].set(p.astype(F32))

    proj, ba = _inproj(h0, row(ln0_g), row(ln0_b), w_main, w_ba, seq=seq, t_len=t_len)
    ya = _pool_branch(proj, pool_w[0].astype(BF16), row(pool_scale[0]), w_br_pool[0].astype(BF16),
                      batch=batch, seq=seq, t_len=t_len)
    qkvn = _conv_qkv(proj, conv_w[0].astype(F32), batch=batch, seq=seq, t_len=t_len)
    o = _delta_rule(qkvn, ba, proj, lane_pad(-jnp.exp(a_log[0].astype(F32))), lane_pad(dt_bias[0]),
                    row(onorm_g[0]), batch=batch, t_len=t_len)
    h1 = _merge(o, ya, proj, h0, w_br_delta[0].astype(BF16), w_out[0].astype(BF16), row(ln0_g), row(ln0_b),
                row(ln1_g[0]), row(ln1_b[0]), batch=batch, seq=seq, t_len=t_len)
    h1 = h1.reshape(batch, t_len, d)

    idx, gate = _peer_query(h1, peer_wq[0].astype(BF16), peer_k1[0].astype(BF16), peer_k2[0].astype(BF16),
                            seq=seq, tq=LANES)
    idx = idx.reshape(N_SLOTS, n_tok).T
    gate = gate.reshape(N_SLOTS, n_tok).T
    ia = lax.shift_right_logical(idx, 7)
    ib = lax.bitwise_and(idx, N_KEYS - 1)

    act = _peer_hidden(h1, peer_u[0].astype(BF16), ia, ib, gate, seq=seq, tb=_pick_tile(seq, (512,)))
    out = _peer_output(act, ia, ib, peer_v[0].astype(BF16), h1, row(ln2_g[0]), row(ln2_b[0]),
                       seq=seq, tb=_pick_tile(seq, (512,)))
    return out.reshape(batch, seq, d)
```

```python
import functools
import math

import jax
import jax.numpy as jnp
from jax import lax
from jax.experimental import pallas as pl
from jax.experimental.pallas import tpu as pltpu

F32 = jnp.float32
BF16 = jnp.bfloat16
I32 = jnp.int32

D_MODEL = 1024
N_META = 16
CHUNK = 64
PAD = CHUNK - N_META
POOL_GROUPS = 4
POOL_GROUP_DIM = 128
POOL_WIDTH = POOL_GROUPS * POOL_GROUP_DIM
POOL_WINDOWS = (2, 4, 8, 16)
POOL_HALO = 16
DN_HEADS = 8
DN_HEAD_DIM = 128
DN_WIDTH = DN_HEADS * DN_HEAD_DIM
CONV_WIDTH = 4
PEER_HEADS = 8
PEER_HALF = 128
N_KEYS = 128
PEER_TOPK = 16
N_SLOTS = PEER_HEADS * PEER_TOPK
LN_EPS = 1e-5
RMS_EPS = 1e-6
L2_EPS = 1e-6
ALPHA = 2.0 ** 0.25

COL_QKV = 0
COL_Z = 3 * DN_WIDTH
COL_GP = COL_Z + DN_WIDTH
COL_GD = COL_GP + D_MODEL
COL_POOL = COL_GD + D_MODEL
N_PROJ = COL_POOL + POOL_WIDTH
BA_WIDTH = 256

SUBLANES = 8
LANES = 128
VMEM_LIMIT = 60 * 1024 * 1024

_CAND = [(a, b) for a in range(PEER_TOPK) for b in range(PEER_TOPK) if (a + 1) * (b + 1) <= PEER_TOPK]


def _params(sem):
    return pltpu.CompilerParams(dimension_semantics=sem, vmem_limit_bytes=VMEM_LIMIT)


def _layer_norm(x, g, b):
    mu = jnp.mean(x, axis=-1, keepdims=True)
    xc = x - mu
    var = jnp.mean(xc * xc, axis=-1, keepdims=True)
    return xc * lax.rsqrt(var + LN_EPS) * g + b


def _zero_pad_rows(y, row, seq):
    return jnp.where(row >= seq, jnp.where(row < seq + PAD, 0.0, y), y)


def _sigmoid(x):
    return 1.0 / (1.0 + jnp.exp(-x))


def _dot(a, b):
    return jnp.dot(a, b, preferred_element_type=F32)


def _dot_nt(a, b):
    return lax.dot_general(a, b, (((1,), (1,)), ((), ())), preferred_element_type=F32)


def _inproj_kernel(x_ref, g_ref, b_ref, w_ref, wba_ref, proj_ref, ba_ref, xn_ref, *, tiles_per_batch, seq):
    i = pl.program_id(0)
    j = pl.program_id(1)
    tm = x_ref.shape[0]

    @pl.when(j == 0)
    def _():
        y = _layer_norm(x_ref[...], g_ref[...], b_ref[...])
        row = lax.broadcasted_iota(I32, (tm, 1), 0) + lax.rem(i, tiles_per_batch) * tm
        xn = _zero_pad_rows(y, row, seq).astype(BF16)
        xn_ref[...] = xn
        ba_ref[...] = _dot(xn, wba_ref[...])

    proj_ref[...] = _dot(xn_ref[...], w_ref[...]).astype(proj_ref.dtype)


def _inproj(h0, ln_g, ln_b, w_main, w_ba, *, seq, t_len):
    rows = h0.shape[0]
    tiles_per_batch = 4
    tm = t_len // tiles_per_batch
    tn = N_PROJ // 4
    kern = functools.partial(_inproj_kernel, tiles_per_batch=tiles_per_batch, seq=seq)
    return pl.pallas_call(
        kern,
        grid=(rows // tm, N_PROJ // tn),
        in_specs=[
            pl.BlockSpec((tm, D_MODEL), lambda i, j: (i, 0)),
            pl.BlockSpec((1, D_MODEL), lambda i, j: (0, 0)),
            pl.BlockSpec((1, D_MODEL), lambda i, j: (0, 0)),
            pl.BlockSpec((D_MODEL, tn), lambda i, j: (0, j)),
            pl.BlockSpec((D_MODEL, BA_WIDTH), lambda i, j: (0, 0)),
        ],
        out_specs=[
            pl.BlockSpec((tm, tn), lambda i, j: (i, j)),
            pl.BlockSpec((tm, BA_WIDTH), lambda i, j: (i, 0)),
        ],
        out_shape=[
            jax.ShapeDtypeStruct((rows, N_PROJ), BF16),
            jax.ShapeDtypeStruct((rows, BA_WIDTH), F32),
        ],
        scratch_shapes=[pltpu.VMEM((tm, D_MODEL), BF16)],
        compiler_params=_params(("arbitrary", "arbitrary")),
        name="inproj",
    )(h0, ln_g, ln_b, w_main, w_ba)


def _pool_kernel(xp_ref, halo_ref, pw_ref, ps_ref, wbp_ref, ya_ref, *, seq):
    i = pl.program_id(1)
    tm = xp_ref.shape[0]
    v = jnp.concatenate([halo_ref[...].astype(F32), xp_ref[...].astype(F32)], axis=0)
    row = lax.broadcasted_iota(I32, (tm, POOL_GROUP_DIM), 0) + i * tm
    meta_pos = row - (seq + PAD)
    outs = []
    for gi, w in enumerate(POOL_WINDOWS):
        vg = v[:, gi * POOL_GROUP_DIM:(gi + 1) * POOL_GROUP_DIM]
        s = vg
        shift = 1
        while shift < w:
            s = s + pltpu.roll(s, shift, axis=0)
            shift *= 2
        s = s[POOL_HALO:]
        xg = vg[POOL_HALO:]
        count = jnp.where(meta_pos >= 0, jnp.minimum(meta_pos + 1, w), w).astype(F32)
        pooled = s / count - xg
        mixed = _dot(pooled.astype(BF16), pw_ref[gi])
        outs.append(mixed)
    y_pool = jnp.concatenate(outs, axis=1) * ps_ref[...]
    ya_ref[...] = _dot(y_pool.astype(BF16), wbp_ref[...])


def _pool_branch(proj, pool_w, pool_scale, w_br_pool, *, batch, seq, t_len):
    rows = proj.shape[0]
    tiles_per_batch = 4
    tm = t_len // tiles_per_batch
    halo_blocks_per_batch = t_len // POOL_HALO
    halo_blocks_per_tile = tm // POOL_HALO
    pool_col = COL_POOL // POOL_WIDTH

    def halo_map(b, i):
        prev = lax.rem(i * halo_blocks_per_tile + halo_blocks_per_batch - 1, halo_blocks_per_batch)
        return (b * halo_blocks_per_batch + prev, pool_col)

    return pl.pallas_call(
        functools.partial(_pool_kernel, seq=seq),
        grid=(batch, tiles_per_batch),
        in_specs=[
            pl.BlockSpec((tm, POOL_WIDTH), lambda b, i: (b * tiles_per_batch + i, pool_col)),
            pl.BlockSpec((POOL_HALO, POOL_WIDTH), halo_map),
            pl.BlockSpec((POOL_GROUPS, POOL_GROUP_DIM, POOL_GROUP_DIM), lambda b, i: (0, 0, 0)),
            pl.BlockSpec((1, POOL_WIDTH), lambda b, i: (0, 0)),
            pl.BlockSpec((POOL_WIDTH, D_MODEL), lambda b, i: (0, 0)),
        ],
        out_specs=pl.BlockSpec((tm, D_MODEL), lambda b, i: (b * tiles_per_batch + i, 0)),
        out_shape=jax.ShapeDtypeStruct((rows, D_MODEL), F32),
        compiler_params=_params(("arbitrary", "arbitrary")),
        name="pool_branch",
    )(proj, proj, pool_w, pool_scale, w_br_pool)


def _conv_kernel(x_ref, w_ref, o_ref, *, seq):
    s = pl.program_id(1)
    x = x_ref[...].astype(F32)
    t_len = x.shape[0]
    w = w_ref[...]
    y = x * w[CONV_WIDTH - 1:CONV_WIDTH, :]
    for lag in range(1, CONV_WIDTH):
        y = y + pltpu.roll(x, lag, axis=0) * w[CONV_WIDTH - 1 - lag:CONV_WIDTH - lag, :]
    y = y * _sigmoid(y)
    ss = jnp.sum(y * y, axis=-1, keepdims=True)
    q_scale = jnp.where(s < DN_HEADS, DN_HEAD_DIM ** -0.5, 1.0).astype(F32)
    fac = jnp.where(s < 2 * DN_HEADS, lax.rsqrt(ss + L2_EPS) * q_scale, 1.0)
    row = lax.broadcasted_iota(I32, (t_len, 1), 0)
    o_ref[...] = _zero_pad_rows(y * fac, row, seq).astype(o_ref.dtype)


def _conv_qkv(proj, conv_w, *, batch, seq, t_len):
    rows = proj.shape[0]
    n_slabs = 3 * DN_HEADS
    return pl.pallas_call(
        functools.partial(_conv_kernel, seq=seq),
        grid=(batch, n_slabs),
        in_specs=[
            pl.BlockSpec((t_len, DN_HEAD_DIM), lambda b, s: (b, COL_QKV // DN_HEAD_DIM + s)),
            pl.BlockSpec((CONV_WIDTH, DN_HEAD_DIM), lambda b, s: (0, s)),
        ],
        out_specs=pl.BlockSpec((t_len, DN_HEAD_DIM), lambda b, s: (b, s)),
        out_shape=jax.ShapeDtypeStruct((rows, 3 * DN_WIDTH), BF16),
        compiler_params=_params(("arbitrary", "arbitrary")),
        name="conv_qkv",
    )(proj, conv_w)


def _inv_unit_lower(a, masks, eye):
    m8, m16, m32, m64 = masks
    n = [x * m8 for x in a]
    n2 = [_dot(x, x) for x in n]
    n4 = [_dot(x, x) for x in n2]
    t = [eye - x for x in n]
    t = [x + _dot(x, y) for x, y in zip(t, n2)]
    t = [x + _dot(x, y) for x, y in zip(t, n4)]
    for m in (m16, m32, m64):
        at = [_dot(x * m, y) for x, y in zip(a, t)]
        t = [x - _dot(x, y) for x, y in zip(t, at)]
    return t


DELTA_STREAMS = 2


def _delta_kernel(q_ref, k_ref, v_ref, ba_ref, z_ref, nega_ref, dtb_ref, og_ref, o_ref, state_ref):
    n = DELTA_STREAMS
    c = pl.program_id(1)

    @pl.when(c == 0)
    def _():
        state_ref[...] = jnp.zeros_like(state_ref)

    r = lax.broadcasted_iota(I32, (CHUNK, CHUNK), 0)
    cc = lax.broadcasted_iota(I32, (CHUNK, CHUNK), 1)
    causal = r >= cc
    strict = r > cc
    eye = jnp.where(r == cc, 1.0, 0.0).astype(F32)
    tril = jnp.where(causal, 1.0, 0.0).astype(F32)

    def blk(x, s):
        return lax.shift_right_logical(x, s)

    lower = jnp.where(strict, 1.0, 0.0).astype(F32)
    m8 = jnp.where(blk(r, 3) == blk(cc, 3), lower, 0.0)
    m16 = jnp.where((blk(r, 4) == blk(cc, 4)) & (blk(r, 3) != blk(cc, 3)), lower, 0.0)
    m32 = jnp.where((blk(r, 5) == blk(cc, 5)) & (blk(r, 4) != blk(cc, 4)), lower, 0.0)
    m64 = jnp.where(blk(r, 5) != blk(cc, 5), lower, 0.0)
    masks = (m8, m16, m32, m64)

    beta_all, gcum, gcum_t = [], [], []
    for s in range(n):
        ba = ba_ref[s]
        beta_all.append(_sigmoid(ba[:, :LANES]))
        x = ba[:, LANES:] + dtb_ref[...]
        softplus = jnp.maximum(x, 0.0) + jnp.log(1.0 + jnp.exp(-jnp.abs(x)))
        g_all = nega_ref[...] * softplus
        gcum.append(lax.dot_general(tril, g_all, (((1,), (0,)), ((), ())), precision=lax.Precision.HIGHEST,
                                    preferred_element_type=F32))
        gcum_t.append(gcum[s].T)

    chains = [(s, h) for s in range(n) for h in range(DN_HEADS)]
    ids = range(len(chains))
    sl = [slice(h * DN_HEAD_DIM, (h + 1) * DN_HEAD_DIM) for _, h in chains]
    q = [q_ref[s, :, sl[i]].astype(F32) for i, (s, h) in enumerate(chains)]
    k = [k_ref[s, :, sl[i]].astype(F32) for i, (s, h) in enumerate(chains)]
    v = [v_ref[s, :, sl[i]].astype(F32) for i, (s, h) in enumerate(chains)]
    beta = [beta_all[s][:, h:h + 1] for s, h in chains]
    gc = [gcum[s][:, h:h + 1] for s, h in chains]
    gr = [gcum_t[s][h:h + 1, :] for s, h in chains]
    g_last = [gcum[s][CHUNK - 1:CHUNK, h:h + 1] for s, h in chains]
    decay = [jnp.where(causal, jnp.exp(jnp.where(causal, gc[i] - gr[i], 0.0)), 0.0) for i in ids]
    eg = [jnp.exp(gc[i]) for i in ids]
    kb = [k[i] * beta[i] for i in ids]
    kq = [_dot_nt(jnp.concatenate([kb[i], q[i]], axis=0), k[i]) for i in ids]
    a_mat = [jnp.where(strict, kq[i][:CHUNK] * decay[i], 0.0) for i in ids]
    t_inv = _inv_unit_lower(a_mat, masks, eye)
    rhs = [jnp.concatenate([v[i] * beta[i], kb[i] * eg[i]], axis=1) for i in ids]
    uw = [_dot(t_inv[i], rhs[i]) for i in ids]
    s_prev = [state_ref[i] for i in ids]
    wq = [jnp.concatenate([uw[i][:, DN_HEAD_DIM:], q[i] * eg[i]], axis=0) for i in ids]
    ws = [_dot(wq[i], s_prev[i]) for i in ids]
    v_new = [uw[i][:, :DN_HEAD_DIM] - ws[i][:CHUNK] for i in ids]
    attn = [kq[i][CHUNK:] * decay[i] for i in ids]
    intra = [_dot(attn[i], v_new[i]) for i in ids]
    k_dec_t = [(k[i] * jnp.exp(g_last[i] - gc[i])).T for i in ids]
    kv = [_dot(k_dec_t[i], v_new[i]) for i in ids]
    for i, (s, h) in enumerate(chains):
        state_ref[i] = s_prev[i] * jnp.exp(g_last[i]) + kv[i]
        out = ws[i][CHUNK:] + intra[i]
        out = out * lax.rsqrt(jnp.mean(out * out, axis=-1, keepdims=True) + RMS_EPS) * og_ref[...]
        zh = z_ref[s, :, sl[i]].astype(F32)
        o_ref[s, :, sl[i]] = (out * (zh * _sigmoid(zh))).astype(o_ref.dtype)


def _delta_rule(qkvn, ba, proj, neg_a, dt_bias, onorm_g, *, batch, t_len):
    rows = qkvn.shape[0]
    n_chunks = t_len // CHUNK

    n = DELTA_STREAMS
    assert batch % n == 0
    groups = batch // n
    by_stream = lambda a: a.reshape(groups, n, t_len, a.shape[-1])

    def chunk_block(col):
        return lambda g, c: (g, 0, lax.rem(c + n_chunks - 1, n_chunks), col)

    const = lambda g, c: (0, 0)
    out = pl.pallas_call(
        _delta_kernel,
        grid=(groups, n_chunks),
        in_specs=[
            pl.BlockSpec((None, n, CHUNK, DN_WIDTH), chunk_block(0)),
            pl.BlockSpec((None, n, CHUNK, DN_WIDTH), chunk_block(1)),
            pl.BlockSpec((None, n, CHUNK, DN_WIDTH), chunk_block(2)),
            pl.BlockSpec((None, n, CHUNK, BA_WIDTH), chunk_block(0)),
            pl.BlockSpec((None, n, CHUNK, DN_WIDTH), chunk_block(COL_Z // DN_WIDTH)),
            pl.BlockSpec((1, LANES), const),
            pl.BlockSpec((1, LANES), const),
            pl.BlockSpec((1, DN_HEAD_DIM), const),
        ],
        out_specs=pl.BlockSpec((None, n, CHUNK, DN_WIDTH), chunk_block(0)),
        out_shape=jax.ShapeDtypeStruct((groups, n, t_len, DN_WIDTH), BF16),
        scratch_shapes=[pltpu.VMEM((n * DN_HEADS, DN_HEAD_DIM, DN_HEAD_DIM), F32)],
        compiler_params=_params(("arbitrary", "arbitrary")),
        name="delta_rule",
    )(by_stream(qkvn), by_stream(qkvn), by_stream(qkvn), by_stream(ba), by_stream(proj), neg_a, dt_bias, onorm_g)
    return out.reshape(rows, DN_WIDTH)


def _merge_kernel(o_ref, ya_ref, gp_ref, gd_ref, h0_ref, wd_ref, wo_ref, g0_ref, b0_ref, g1_ref, b1_ref, h1_ref):
    y_b = _dot(o_ref[...], wd_ref[...])
    merged = _sigmoid(gp_ref[...].astype(F32)) * ya_ref[...] + _sigmoid(gd_ref[...].astype(F32)) * y_b
    mix = _dot(merged.astype(BF16), wo_ref[...])
    h = _layer_norm(h0_ref[...], g0_ref[...], b0_ref[...])
    h1_ref[...] = _layer_norm(ALPHA * h + mix, g1_ref[...], b1_ref[...])


def _merge(o, ya, proj, h0, w_br_delta, w_out, g0, b0, g1, b1, *, batch, seq, t_len):
    bf16_rows = 2 * SUBLANES
    tiles_per_batch = next(n for n in (10, 8, 4, 2, 1) if t_len % (n * bf16_rows) == 0)
    tm = t_len // tiles_per_batch
    row = lambda i: (i, 0)
    const = lambda i: (0, 0)
    return pl.pallas_call(
        _merge_kernel,
        grid=(batch * tiles_per_batch,),
        in_specs=[
            pl.BlockSpec((tm, DN_WIDTH), row),
            pl.BlockSpec((tm, D_MODEL), row),
            pl.BlockSpec((tm, D_MODEL), lambda i: (i, COL_GP // D_MODEL)),
            pl.BlockSpec((tm, D_MODEL), lambda i: (i, COL_GD // D_MODEL)),
            pl.BlockSpec((tm, D_MODEL), row),
            pl.BlockSpec((DN_WIDTH, D_MODEL), const),
            pl.BlockSpec((D_MODEL, D_MODEL), const),
            pl.BlockSpec((1, D_MODEL), const),
            pl.BlockSpec((1, D_MODEL), const),
            pl.BlockSpec((1, D_MODEL), const),
            pl.BlockSpec((1, D_MODEL), const),
        ],
        out_specs=pl.BlockSpec((tm, D_MODEL), row),
        out_shape=jax.ShapeDtypeStruct((batch * t_len, D_MODEL), F32),
        compiler_params=_params(("arbitrary",)),
        name="merge",
    )(o, ya, proj, proj, h0, w_br_delta, w_out, g0, b0, g1, b1)


def _tile(ref, k):
    return ref[k * SUBLANES:(k + 1) * SUBLANES, :]


def _argmax_tournament(vals, ids):
    nodes = list(zip(vals, ids))
    while len(nodes) > 1:
        nxt = []
        for p in range(0, len(nodes) - 1, 2):
            (va, ia), (vb, ib) = nodes[p], nodes[p + 1]
            first = va >= vb
            nxt.append((jnp.maximum(va, vb), jnp.where(first, ia, ib)))
        if len(nodes) % 2:
            nxt.append(nodes[-1])
        nodes = nxt
    return nodes[0]


def _query_kernel(x_ref, wq_ref, k1_ref, k2_ref, idx_ref, gate_ref, s1_ref, s2_ref, m1_ref, i1_ref, m2_ref, i2_ref,
                  c_ref, e_ref, sc_ref):
    xb = x_ref[...].astype(BF16)
    q = _dot(xb, wq_ref[...])
    for h in range(PEER_HEADS):
        q1 = q[:, (2 * h) * PEER_HALF:(2 * h + 1) * PEER_HALF].astype(BF16)
        q2 = q[:, (2 * h + 1) * PEER_HALF:(2 * h + 2) * PEER_HALF].astype(BF16)
        s1_ref[pl.ds(h, N_KEYS, stride=SUBLANES), :] = _dot_nt(k1_ref[h], q1)
        s2_ref[pl.ds(h, N_KEYS, stride=SUBLANES), :] = _dot_nt(k2_ref[h], q2)

    neg_inf = jnp.float32(-jnp.inf)

    def sub_round(r, carry):
        for s_ref, m_ref, i_ref in ((s1_ref, m1_ref, i1_ref), (s2_ref, m2_ref, i2_ref)):
            vals = [_tile(s_ref, k) for k in range(N_KEYS)]
            best, arg = _argmax_tournament(vals, list(range(N_KEYS)))
            m_ref[r] = best
            i_ref[r] = arg
            for k in range(N_KEYS):
                s_ref[k * SUBLANES:(k + 1) * SUBLANES, :] = jnp.where(arg == k, neg_inf, vals[k])
        return carry

    lax.fori_loop(0, PEER_TOPK, sub_round, 0)

    for j, (a, b) in enumerate(_CAND):
        c_ref[j * SUBLANES:(j + 1) * SUBLANES, :] = m1_ref[a] + m2_ref[b]
        e_ref[j * SUBLANES:(j + 1) * SUBLANES, :] = i1_ref[a] * N_KEYS + i2_ref[b]

    def pair_round(r, carry):
        vals = [_tile(c_ref, j) for j in range(len(_CAND))]
        ids = [_tile(e_ref, j) for j in range(len(_CAND))]
        best, arg = _argmax_tournament(vals, ids)
        sc_ref[r] = best
        idx_ref[r] = arg
        for j in range(len(_CAND)):
            c_ref[j * SUBLANES:(j + 1) * SUBLANES, :] = jnp.where(ids[j] == arg, neg_inf, vals[j])
        return carry

    lax.fori_loop(0, PEER_TOPK, pair_round, 0)

    sc = sc_ref[...]
    ex = jnp.exp(sc - sc[0:1])
    gate_ref[...] = ex / jnp.sum(ex, axis=0, keepdims=True)


def _token_tile_map(seq, tile):
    per_batch = seq // tile
    return lambda i, *_: (i // per_batch, lax.rem(i, per_batch), 0)


def _peer_query(h1, wq, k1, k2, *, seq, tq):
    n_tok = h1.shape[0] * seq
    slot_shape = (PEER_TOPK, PEER_HEADS, tq)
    return pl.pallas_call(
        _query_kernel,
        grid=(n_tok // tq,),
        in_specs=[
            pl.BlockSpec((None, tq, D_MODEL), _token_tile_map(seq, tq)),
            pl.BlockSpec((D_MODEL, 2 * PEER_HALF * PEER_HEADS), lambda i: (0, 0)),
            pl.BlockSpec((PEER_HEADS, N_KEYS, PEER_HALF), lambda i: (0, 0, 0)),
            pl.BlockSpec((PEER_HEADS, N_KEYS, PEER_HALF), lambda i: (0, 0, 0)),
        ],
        out_specs=[
            pl.BlockSpec(slot_shape, lambda i: (0, 0, i)),
            pl.BlockSpec(slot_shape, lambda i: (0, 0, i)),
        ],
        out_shape=[
            jax.ShapeDtypeStruct((PEER_TOPK, PEER_HEADS, n_tok), I32),
            jax.ShapeDtypeStruct((PEER_TOPK, PEER_HEADS, n_tok), F32),
        ],
        scratch_shapes=[
            pltpu.VMEM((N_KEYS * SUBLANES, tq), F32),
            pltpu.VMEM((N_KEYS * SUBLANES, tq), F32),
            pltpu.VMEM(slot_shape, F32),
            pltpu.VMEM(slot_shape, I32),
            pltpu.VMEM(slot_shape, F32),
            pltpu.VMEM(slot_shape, I32),
            pltpu.VMEM((len(_CAND) * SUBLANES, tq), F32),
            pltpu.VMEM((len(_CAND) * SUBLANES, tq), I32),
            pltpu.VMEM(slot_shape, F32),
        ],
        compiler_params=_params(("arbitrary",)),
        name="peer_query",
    )(h1, wq, k1, k2)


IA_PER_STEP = 16
EXPERTS_PER_STEP = IA_PER_STEP * N_KEYS
PEER_STEPS = (N_KEYS * N_KEYS) // EXPERTS_PER_STEP
PEER_PIECES = 2
TABLE_STREAMS = 4


def _table_specs():
    rows = EXPERTS_PER_STEP // TABLE_STREAMS
    return [pl.BlockSpec((rows, D_MODEL), lambda i, j, p=p: (j * TABLE_STREAMS + p, 0)) for p in range(TABLE_STREAMS)]


def _one_hot_rows(idx_row):
    iota = lax.broadcasted_iota(I32, (N_KEYS, N_SLOTS), 0)
    return jnp.where(iota == idx_row, 1.0, 0.0).astype(BF16)


def _block_diag(a, b):
    zero = jnp.zeros_like(a)
    return jnp.concatenate([jnp.concatenate([a, zero], axis=1), jnp.concatenate([zero, b], axis=1)], axis=0)


def _hidden_kernel(x_ref, *refs, tb, hp):
    u_refs = refs[:TABLE_STREAMS]
    ia_ref, ib_ref, gate_ref, act_ref, xb_ref, h0_ref, h1_ref, hs_ref = refs[TABLE_STREAMS:]
    i = pl.program_id(0)
    j = pl.program_id(1)
    tokens = tb // PEER_STEPS // PEER_PIECES
    parts = TABLE_STREAMS // PEER_PIECES
    ia_pairs = IA_PER_STEP // PEER_PIECES // 2

    @pl.when(jnp.logical_and(i == 0, j == 0))
    def _():
        h1_ref[...] = jnp.zeros_like(h1_ref)

    @pl.when(j == 0)
    def _():
        xb_ref[...] = x_ref[...].astype(BF16)

    def step(fill_ref, drain_ref):
        iota = lax.broadcasted_iota(I32, (N_KEYS, N_SLOTS), 0)
        for q in range(PEER_PIECES):
            base = pl.multiple_of((j * PEER_PIECES + q) * tokens, tokens)
            ia_rows = ia_ref[pl.ds(base, tokens), :]
            ib_rows = ib_ref[pl.ds(base, tokens), :]
            lhs, rhs = [], []
            for p in range(tokens // 2):
                tiles = []
                for t in (2 * p, 2 * p + 1):
                    start = pl.multiple_of((base + t) * hp, SUBLANES)
                    tiles.append(pltpu.bitcast(drain_ref[pl.ds(start, N_KEYS // 2), :], BF16))
                lhs.append(jnp.concatenate(tiles, axis=1))
                rhs.append(_block_diag(_one_hot_rows(ib_rows[2 * p:2 * p + 1, :]),
                                       _one_hot_rows(ib_rows[2 * p + 1:2 * p + 2, :])))

            experts = jnp.concatenate([u_refs[q * parts + p][...] for p in range(parts)], axis=0)
            hc = _dot_nt(xb_ref[...], experts)
            for m in range(ia_pairs):
                even = hc[:, (2 * m) * N_KEYS:(2 * m + 1) * N_KEYS]
                odd = hc[:, (2 * m + 1) * N_KEYS:(2 * m + 2) * N_KEYS]
                packed = pltpu.pack_elementwise([even, odd], packed_dtype=BF16)
                pair = (j * PEER_PIECES + q) * ia_pairs + m
                fill_ref[pl.ds(pair, tb, stride=hp), :] = lax.bitcast_convert_type(packed, I32)

            rows = []
            for p in range(tokens // 2):
                both = _dot(lhs[p], rhs[p])
                for k in range(2):
                    mine = both[:, k * N_SLOTS:(k + 1) * N_SLOTS]
                    picked = jnp.where(iota == ia_rows[2 * p + k:2 * p + k + 1, :], mine, 0.0)
                    rows.append(jnp.sum(picked, axis=0, keepdims=True))
            hs_ref[pl.ds(base, tokens), :] = jnp.concatenate(rows, axis=0)

    parity = lax.rem(i, 2)

    @pl.when(parity == 0)
    def _():
        step(h0_ref, h1_ref)

    @pl.when(parity == 1)
    def _():
        step(h1_ref, h0_ref)

    @pl.when(j == PEER_STEPS - 1)
    def _():
        hid = hs_ref[...]
        gelu = 0.5 * hid * (1.0 + lax.erf(hid * (2.0 ** -0.5)))
        act_ref[...] = gelu * gate_ref[...]


def _peer_hidden(h1, u_tab, ia, ib, gate, *, seq, tb):
    n_tiles = h1.shape[0] * seq // tb
    hp = N_KEYS // 2 + SUBLANES
    x_map = _token_tile_map(seq, tb)
    ahead = lambda i, j: x_map(jnp.minimum(i, n_tiles - 1))
    behind = lambda i, j: (jnp.maximum(i - 1, 0), 0)
    return pl.pallas_call(
        functools.partial(_hidden_kernel, tb=tb, hp=hp),
        grid=(n_tiles + 1, PEER_STEPS),
        in_specs=[
            pl.BlockSpec((None, tb, D_MODEL), ahead),
            *_table_specs(),
            pl.BlockSpec((tb, N_SLOTS), behind),
            pl.BlockSpec((tb, N_SLOTS), behind),
            pl.BlockSpec((tb, N_SLOTS), behind),
        ],
        out_specs=pl.BlockSpec((tb, N_SLOTS), behind),
        out_shape=jax.ShapeDtypeStruct((n_tiles * tb, N_SLOTS), F32),
        scratch_shapes=[
            pltpu.VMEM((tb, D_MODEL), BF16),
            pltpu.VMEM((tb * hp, N_KEYS), I32),
            pltpu.VMEM((tb * hp, N_KEYS), I32),
            pltpu.VMEM((tb, N_SLOTS), F32),
        ],
        compiler_params=_params(("arbitrary", "arbitrary")),
        name="peer_hidden",
    )(h1, *([u_tab] * TABLE_STREAMS), ia, ib, gate)


def _output_kernel(act_ref, ia_ref, ib_ref, *refs, tb, hp):
    v_refs = refs[:TABLE_STREAMS]
    h1_ref, g_ref, b_ref, out_ref, a0_ref, a1_ref, acc_ref = refs[TABLE_STREAMS:]
    i = pl.program_id(0)
    j = pl.program_id(1)
    tokens = tb // PEER_STEPS // PEER_PIECES
    parts = TABLE_STREAMS // PEER_PIECES
    ia_per_piece = IA_PER_STEP // PEER_PIECES

    @pl.when(jnp.logical_and(i == 0, j == 0))
    def _():
        a1_ref[...] = jnp.zeros_like(a1_ref)

    @pl.when(j == 0)
    def _():
        acc_ref[...] = jnp.zeros_like(acc_ref)

    def step(fill_ref, drain_ref):
        iota = lax.broadcasted_iota(I32, (N_KEYS, N_SLOTS), 0)
        acc = acc_ref[...]
        for q in range(PEER_PIECES):
            base = pl.multiple_of((j * PEER_PIECES + q) * tokens, tokens)
            ia_rows = ia_ref[pl.ds(base, tokens), :]
            ib_rows = ib_ref[pl.ds(base, tokens), :]
            act_rows = act_ref[pl.ds(base, tokens), :]
            lhs, sels = [], []
            for p in range(tokens // 2):
                spread = [jnp.where(iota == ia_rows[t:t + 1, :], act_rows[t:t + 1, :], 0.0).astype(BF16)
                          for t in (2 * p, 2 * p + 1)]
                lhs.append(jnp.concatenate(spread, axis=1))
                sels += [_one_hot_rows(ib_rows[t:t + 1, :]) for t in (2 * p, 2 * p + 1)]

            tiles = []
            for t in range(ia_per_piece):
                ia = j * IA_PER_STEP + q * ia_per_piece + t
                start = pl.multiple_of(ia * hp, SUBLANES)
                tiles.append(pltpu.bitcast(drain_ref[pl.ds(start, tb // 2), :], BF16))
            table_rows = jnp.concatenate([v_refs[q * parts + p][...] for p in range(parts)], axis=0)
            acc = acc + _dot(jnp.concatenate(tiles, axis=1), table_rows)

            pair_base = (j * PEER_PIECES + q) * (tokens // 2)
            for p in range(tokens // 2):
                both = _dot_nt(lhs[p], _block_diag(sels[2 * p], sels[2 * p + 1]))
                packed = pltpu.pack_elementwise([both[:, :N_KEYS], both[:, N_KEYS:]], packed_dtype=BF16)
                fill_ref[pl.ds(pair_base + p, N_KEYS, stride=hp), :] = lax.bitcast_convert_type(packed, I32)
        acc_ref[...] = acc

    parity = lax.rem(i, 2)

    @pl.when(parity == 0)
    def _():
        step(a0_ref, a1_ref)

    @pl.when(parity == 1)
    def _():
        step(a1_ref, a0_ref)

    @pl.when(j == PEER_STEPS - 1)
    def _():
        out_ref[...] = _layer_norm(ALPHA * h1_ref[...] + acc_ref[...], g_ref[...], b_ref[...])


def _peer_output(act, ia, ib, v_tab, h1, g2, b2, *, seq, tb):
    n_tiles = h1.shape[0] * seq // tb
    hp = tb // 2 + SUBLANES
    x_map = _token_tile_map(seq, tb)
    ahead = lambda i, j: (jnp.minimum(i, n_tiles - 1), 0)
    behind = lambda i, j: (jnp.maximum(i - 1, 0), 0)
    const = lambda i, j: (0, 0)
    return pl.pallas_call(
        functools.partial(_output_kernel, tb=tb, hp=hp),
        grid=(n_tiles + 1, PEER_STEPS),
        in_specs=[
            pl.BlockSpec((tb, N_SLOTS), ahead),
            pl.BlockSpec((tb, N_SLOTS), ahead),
            pl.BlockSpec((tb, N_SLOTS), ahead),
            *_table_specs(),
            pl.BlockSpec((None, tb, D_MODEL), lambda i, j: x_map(jnp.maximum(i - 1, 0))),
            pl.BlockSpec((1, D_MODEL), const),
            pl.BlockSpec((1, D_MODEL), const),
        ],
        out_specs=pl.BlockSpec((tb, D_MODEL), behind),
        out_shape=jax.ShapeDtypeStruct((n_tiles * tb, D_MODEL), F32),
        scratch_shapes=[
            pltpu.VMEM((N_KEYS * hp, N_KEYS), I32),
            pltpu.VMEM((N_KEYS * hp, N_KEYS), I32),
            pltpu.VMEM((tb, D_MODEL), F32),
        ],
        compiler_params=_params(("arbitrary", "arbitrary")),
        name="peer_output",
    )(act, ia, ib, *([v_tab] * TABLE_STREAMS), h1, g2, b2)


def _pick_tile(n, candidates):
    for c in candidates:
        if n % c == 0:
            return c
    raise ValueError(f"no tile in {candidates} divides {n}")


def kernel(x, meta, ln0_g, ln0_b, w_in, pool_w, pool_scale, w_br_pool, conv_w, a_log, dt_bias, onorm_g, w_br_delta, w_out, ln1_g, ln1_b, peer_wq, peer_k1, peer_k2, peer_u, peer_v, ln2_g, ln2_b):
    batch, seq, d = x.shape
    assert d == D_MODEL and seq % CHUNK == 0 and w_in.shape[0] == 1
    t_len = seq + CHUNK
    n_tok = batch * seq
    row = lambda p: p.reshape(1, -1).astype(F32)

    meta_b = jnp.broadcast_to(meta[None].astype(x.dtype), (batch, N_META, d))
    h0 = jnp.concatenate([x, jnp.zeros((batch, PAD, d), x.dtype), meta_b], axis=1).reshape(batch * t_len, d)

    w = w_in[0]
    c_pool, c_qkv, c_z = POOL_WIDTH, POOL_WIDTH + 3 * DN_WIDTH, POOL_WIDTH + 4 * DN_WIDTH
    c_b, c_a = c_z + DN_HEADS, c_z + 2 * DN_HEADS
    w_main = jnp.concatenate([w[:, c_pool:c_z], w[:, c_a:], w[:, :c_pool]], axis=1).astype(BF16)
    w_ba = jnp.zeros((d, BA_WIDTH), F32)
    w_ba = w_ba.at[:, :DN_HEADS].set(w[:, c_z:c_b]).at[:, LANES:LANES + DN_HEADS].set(w[:, c_b:c_a]).astype(BF16)
    lane_pad = lambda p: jnp.zeros((1, LANES), F32).at[0, :DN_HEADS].set(p.astype(F32))

    proj, ba = _inproj(h0, row(ln0_g), row(ln0_b), w_main, w_ba, seq=seq, t_len=t_len)
    ya = _pool_branch(proj, pool_w[0].astype(BF16), row(pool_scale[0]), w_br_pool[0].astype(BF16),
                      batch=batch, seq=seq, t_len=t_len)
    qkvn = _conv_qkv(proj, conv_w[0].astype(F32), batch=batch, seq=seq, t_len=t_len)
    o = _delta_rule(qkvn, ba, proj, lane_pad(-jnp.exp(a_log[0].astype(F32))), lane_pad(dt_bias[0]),
                    row(onorm_g[0]), batch=batch, t_len=t_len)
    h1 = _merge(o, ya, proj, h0, w_br_delta[0].astype(BF16), w_out[0].astype(BF16), row(ln0_g), row(ln0_b),
                row(ln1_g[0]), row(ln1_b[0]), batch=batch, seq=seq, t_len=t_len)
    h1 = h1.reshape(batch, t_len, d)

    idx, gate = _peer_query(h1, peer_wq[0].astype(BF16), peer_k1[0].astype(BF16), peer_k2[0].astype(BF16),
                            seq=seq, tq=LANES)
    idx = idx.reshape(N_SLOTS, n_tok).T
    gate = gate.reshape(N_SLOTS, n_tok).T
    ia = lax.shift_right_logical(idx, 7)
    ib = lax.bitwise_and(idx, N_KEYS - 1)

    act = _peer_hidden(h1, peer_u[0].astype(BF16), ia, ib, gate, seq=seq, tb=_pick_tile(seq, (512,)))
    out = _peer_output(act, ia, ib, peer_v[0].astype(BF16), h1, row(ln2_g[0]), row(ln2_b[0]),
                       seq=seq, tb=_pick_tile(seq, (512,)))
    return out.reshape(batch, seq, d)
```

```python
import functools
import math

import jax
import jax.numpy as jnp
from jax import lax
from jax.experimental import pallas as pl
from jax.experimental.pallas import tpu as pltpu

F32 = jnp.float32
BF16 = jnp.bfloat16
I32 = jnp.int32

D_MODEL = 1024
N_META = 16
CHUNK = 64
PAD = CHUNK - N_META
POOL_GROUPS = 4
POOL_GROUP_DIM = 128
POOL_WIDTH = POOL_GROUPS * POOL_GROUP_DIM
POOL_WINDOWS = (2, 4, 8, 16)
POOL_HALO = 16
DN_HEADS = 8
DN_HEAD_DIM = 128
DN_WIDTH = DN_HEADS * DN_HEAD_DIM
CONV_WIDTH = 4
PEER_HEADS = 8
PEER_HALF = 128
N_KEYS = 128
PEER_TOPK = 16
N_SLOTS = PEER_HEADS * PEER_TOPK
LN_EPS = 1e-5
RMS_EPS = 1e-6
L2_EPS = 1e-6
ALPHA = 2.0 ** 0.25

COL_QKV = 0
COL_Z = 3 * DN_WIDTH
COL_GP = COL_Z + DN_WIDTH
COL_GD = COL_GP + D_MODEL
COL_POOL = COL_GD + D_MODEL
N_PROJ = COL_POOL + POOL_WIDTH
BA_WIDTH = 256

SUBLANES = 8
LANES = 128
VMEM_LIMIT = 60 * 1024 * 1024

_CAND = [(a, b) for a in range(PEER_TOPK) for b in range(PEER_TOPK) if (a + 1) * (b + 1) <= PEER_TOPK]


def _params(sem):
    return pltpu.CompilerParams(dimension_semantics=sem, vmem_limit_bytes=VMEM_LIMIT)


def _layer_norm(x, g, b):
    mu = jnp.mean(x, axis=-1, keepdims=True)
    xc = x - mu
    var = jnp.mean(xc * xc, axis=-1, keepdims=True)
    return xc * lax.rsqrt(var + LN_EPS) * g + b


def _zero_pad_rows(y, row, seq):
    return jnp.where(row >= seq, jnp.where(row < seq + PAD, 0.0, y), y)


def _sigmoid(x):
    return 1.0 / (1.0 + jnp.exp(-x))


def _dot(a, b):
    return jnp.dot(a, b, preferred_element_type=F32)


def _dot_nt(a, b):
    return lax.dot_general(a, b, (((1,), (1,)), ((), ())), preferred_element_type=F32)


def _inproj_kernel(x_ref, g_ref, b_ref, w_ref, wba_ref, proj_ref, ba_ref, xn_ref, *, tiles_per_batch, seq):
    i = pl.program_id(0)
    j = pl.program_id(1)
    tm = x_ref.shape[0]

    @pl.when(j == 0)
    def _():
        y = _layer_norm(x_ref[...], g_ref[...], b_ref[...])
        row = lax.broadcasted_iota(I32, (tm, 1), 0) + lax.rem(i, tiles_per_batch) * tm
        xn = _zero_pad_rows(y, row, seq).astype(BF16)
        xn_ref[...] = xn
        ba_ref[...] = _dot(xn, wba_ref[...])

    proj_ref[...] = _dot(xn_ref[...], w_ref[...]).astype(proj_ref.dtype)


def _inproj(h0, ln_g, ln_b, w_main, w_ba, *, seq, t_len):
    rows = h0.shape[0]
    tiles_per_batch = 4
    tm = t_len // tiles_per_batch
    tn = N_PROJ // 4
    kern = functools.partial(_inproj_kernel, tiles_per_batch=tiles_per_batch, seq=seq)
    return pl.pallas_call(
        kern,
        grid=(rows // tm, N_PROJ // tn),
        in_specs=[
            pl.BlockSpec((tm, D_MODEL), lambda i, j: (i, 0)),
            pl.BlockSpec((1, D_MODEL), lambda i, j: (0, 0)),
            pl.BlockSpec((1, D_MODEL), lambda i, j: (0, 0)),
            pl.BlockSpec((D_MODEL, tn), lambda i, j: (0, j)),
            pl.BlockSpec((D_MODEL, BA_WIDTH), lambda i, j: (0, 0)),
        ],
        out_specs=[
            pl.BlockSpec((tm, tn), lambda i, j: (i, j)),
            pl.BlockSpec((tm, BA_WIDTH), lambda i, j: (i, 0)),
        ],
        out_shape=[
            jax.ShapeDtypeStruct((rows, N_PROJ), BF16),
            jax.ShapeDtypeStruct((rows, BA_WIDTH), F32),
        ],
        scratch_shapes=[pltpu.VMEM((tm, D_MODEL), BF16)],
        compiler_params=_params(("arbitrary", "arbitrary")),
        name="inproj",
    )(h0, ln_g, ln_b, w_main, w_ba)


def _pool_kernel(xp_ref, halo_ref, pw_ref, ps_ref, wbp_ref, ya_ref, *, seq):
    i = pl.program_id(1)
    tm = xp_ref.shape[0]
    v = jnp.concatenate([halo_ref[...].astype(F32), xp_ref[...].astype(F32)], axis=0)
    row = lax.broadcasted_iota(I32, (tm, POOL_GROUP_DIM), 0) + i * tm
    meta_pos = row - (seq + PAD)
    outs = []
    for gi, w in enumerate(POOL_WINDOWS):
        vg = v[:, gi * POOL_GROUP_DIM:(gi + 1) * POOL_GROUP_DIM]
        s = vg
        shift = 1
        while shift < w:
            s = s + pltpu.roll(s, shift, axis=0)
            shift *= 2
        s = s[POOL_HALO:]
        xg = vg[POOL_HALO:]
        count = jnp.where(meta_pos >= 0, jnp.minimum(meta_pos + 1, w), w).astype(F32)
        pooled = s / count - xg
        mixed = _dot(pooled.astype(BF16), pw_ref[gi])
        outs.append(mixed)
    y_pool = jnp.concatenate(outs, axis=1) * ps_ref[...]
    ya_ref[...] = _dot(y_pool.astype(BF16), wbp_ref[...])


def _pool_branch(proj, pool_w, pool_scale, w_br_pool, *, batch, seq, t_len):
    rows = proj.shape[0]
    tiles_per_batch = 4
    tm = t_len // tiles_per_batch
    halo_blocks_per_batch = t_len // POOL_HALO
    halo_blocks_per_tile = tm // POOL_HALO
    pool_col = COL_POOL // POOL_WIDTH

    def halo_map(b, i):
        prev = lax.rem(i * halo_blocks_per_tile + halo_blocks_per_batch - 1, halo_blocks_per_batch)
        return (b * halo_blocks_per_batch + prev, pool_col)

    return pl.pallas_call(
        functools.partial(_pool_kernel, seq=seq),
        grid=(batch, tiles_per_batch),
        in_specs=[
            pl.BlockSpec((tm, POOL_WIDTH), lambda b, i: (b * tiles_per_batch + i, pool_col)),
            pl.BlockSpec((POOL_HALO, POOL_WIDTH), halo_map),
            pl.BlockSpec((POOL_GROUPS, POOL_GROUP_DIM, POOL_GROUP_DIM), lambda b, i: (0, 0, 0)),
            pl.BlockSpec((1, POOL_WIDTH), lambda b, i: (0, 0)),
            pl.BlockSpec((POOL_WIDTH, D_MODEL), lambda b, i: (0, 0)),
        ],
        out_specs=pl.BlockSpec((tm, D_MODEL), lambda b, i: (b * tiles_per_batch + i, 0)),
        out_shape=jax.ShapeDtypeStruct((rows, D_MODEL), F32),
        compiler_params=_params(("arbitrary", "arbitrary")),
        name="pool_branch",
    )(proj, proj, pool_w, pool_scale, w_br_pool)


def _conv_kernel(x_ref, w_ref, o_ref, *, seq):
    s = pl.program_id(1)
    x = x_ref[...].astype(F32)
    t_len = x.shape[0]
    w = w_ref[...]
    y = x * w[CONV_WIDTH - 1:CONV_WIDTH, :]
    for lag in range(1, CONV_WIDTH):
        y = y + pltpu.roll(x, lag, axis=0) * w[CONV_WIDTH - 1 - lag:CONV_WIDTH - lag, :]
    y = y * _sigmoid(y)
    ss = jnp.sum(y * y, axis=-1, keepdims=True)
    q_scale = jnp.where(s < DN_HEADS, DN_HEAD_DIM ** -0.5, 1.0).astype(F32)
    fac = jnp.where(s < 2 * DN_HEADS, lax.rsqrt(ss + L2_EPS) * q_scale, 1.0)
    row = lax.broadcasted_iota(I32, (t_len, 1), 0)
    o_ref[...] = _zero_pad_rows(y * fac, row, seq).astype(o_ref.dtype)


def _conv_qkv(proj, conv_w, *, batch, seq, t_len):
    rows = proj.shape[0]
    n_slabs = 3 * DN_HEADS
    return pl.pallas_call(
        functools.partial(_conv_kernel, seq=seq),
        grid=(batch, n_slabs),
        in_specs=[
            pl.BlockSpec((t_len, DN_HEAD_DIM), lambda b, s: (b, COL_QKV // DN_HEAD_DIM + s)),
            pl.BlockSpec((CONV_WIDTH, DN_HEAD_DIM), lambda b, s: (0, s)),
        ],
        out_specs=pl.BlockSpec((t_len, DN_HEAD_DIM), lambda b, s: (b, s)),
        out_shape=jax.ShapeDtypeStruct((rows, 3 * DN_WIDTH), BF16),
        compiler_params=_params(("arbitrary", "arbitrary")),
        name="conv_qkv",
    )(proj, conv_w)


def _inv_unit_lower(a, masks, eye):
    m8, m16, m32, m64 = masks
    n = [x * m8 for x in a]
    n2 = [_dot(x, x) for x in n]
    n4 = [_dot(x, x) for x in n2]
    t = [eye - x for x in n]
    t = [x + _dot(x, y) for x, y in zip(t, n2)]
    t = [x + _dot(x, y) for x, y in zip(t, n4)]
    for m in (m16, m32, m64):
        at = [_dot(x * m, y) for x, y in zip(a, t)]
        t = [x - _dot(x, y) for x, y in zip(t, at)]
    return t


DELTA_STREAMS = 4


def _delta_kernel(q_ref, k_ref, v_ref, ba_ref, z_ref, nega_ref, dtb_ref, og_ref, o_ref, state_ref):
    n = DELTA_STREAMS
    c = pl.program_id(1)

    @pl.when(c == 0)
    def _():
        state_ref[...] = jnp.zeros_like(state_ref)

    r = lax.broadcasted_iota(I32, (CHUNK, CHUNK), 0)
    cc = lax.broadcasted_iota(I32, (CHUNK, CHUNK), 1)
    causal = r >= cc
    strict = r > cc
    eye = jnp.where(r == cc, 1.0, 0.0).astype(F32)
    tril = jnp.where(causal, 1.0, 0.0).astype(F32)

    def blk(x, s):
        return lax.shift_right_logical(x, s)

    lower = jnp.where(strict, 1.0, 0.0).astype(F32)
    m8 = jnp.where(blk(r, 3) == blk(cc, 3), lower, 0.0)
    m16 = jnp.where((blk(r, 4) == blk(cc, 4)) & (blk(r, 3) != blk(cc, 3)), lower, 0.0)
    m32 = jnp.where((blk(r, 5) == blk(cc, 5)) & (blk(r, 4) != blk(cc, 4)), lower, 0.0)
    m64 = jnp.where(blk(r, 5) != blk(cc, 5), lower, 0.0)
    masks = (m8, m16, m32, m64)

    beta_all, gcum, gcum_t = [], [], []
    for s in range(n):
        ba = ba_ref[s]
        beta_all.append(_sigmoid(ba[:, :LANES]))
        x = ba[:, LANES:] + dtb_ref[...]
        softplus = jnp.maximum(x, 0.0) + jnp.log(1.0 + jnp.exp(-jnp.abs(x)))
        g_all = nega_ref[...] * softplus
        gcum.append(lax.dot_general(tril, g_all, (((1,), (0,)), ((), ())), precision=lax.Precision.HIGHEST,
                                    preferred_element_type=F32))
        gcum_t.append(gcum[s].T)

    chains = [(s, h) for s in range(n) for h in range(DN_HEADS)]
    ids = range(len(chains))
    sl = [slice(h * DN_HEAD_DIM, (h + 1) * DN_HEAD_DIM) for _, h in chains]
    q = [q_ref[s, :, sl[i]].astype(F32) for i, (s, h) in enumerate(chains)]
    k = [k_ref[s, :, sl[i]].astype(F32) for i, (s, h) in enumerate(chains)]
    v = [v_ref[s, :, sl[i]].astype(F32) for i, (s, h) in enumerate(chains)]
    beta = [beta_all[s][:, h:h + 1] for s, h in chains]
    gc = [gcum[s][:, h:h + 1] for s, h in chains]
    gr = [gcum_t[s][h:h + 1, :] for s, h in chains]
    g_last = [gcum[s][CHUNK - 1:CHUNK, h:h + 1] for s, h in chains]
    decay = [jnp.where(causal, jnp.exp(jnp.where(causal, gc[i] - gr[i], 0.0)), 0.0) for i in ids]
    eg = [jnp.exp(gc[i]) for i in ids]
    kb = [k[i] * beta[i] for i in ids]
    kq = [_dot_nt(jnp.concatenate([kb[i], q[i]], axis=0), k[i]) for i in ids]
    a_mat = [jnp.where(strict, kq[i][:CHUNK] * decay[i], 0.0) for i in ids]
    t_inv = _inv_unit_lower(a_mat, masks, eye)
    rhs = [jnp.concatenate([v[i] * beta[i], kb[i] * eg[i]], axis=1) for i in ids]
    uw = [_dot(t_inv[i], rhs[i]) for i in ids]
    s_prev = [state_ref[i] for i in ids]
    wq = [jnp.concatenate([uw[i][:, DN_HEAD_DIM:], q[i] * eg[i]], axis=0) for i in ids]
    ws = [_dot(wq[i], s_prev[i]) for i in ids]
    v_new = [uw[i][:, :DN_HEAD_DIM] - ws[i][:CHUNK] for i in ids]
    attn = [kq[i][CHUNK:] * decay[i] for i in ids]
    intra = [_dot(attn[i], v_new[i]) for i in ids]
    k_dec_t = [(k[i] * jnp.exp(g_last[i] - gc[i])).T for i in ids]
    kv = [_dot(k_dec_t[i], v_new[i]) for i in ids]
    for i, (s, h) in enumerate(chains):
        state_ref[i] = s_prev[i] * jnp.exp(g_last[i]) + kv[i]
        out = ws[i][CHUNK:] + intra[i]
        out = out * lax.rsqrt(jnp.mean(out * out, axis=-1, keepdims=True) + RMS_EPS) * og_ref[...]
        zh = z_ref[s, :, sl[i]].astype(F32)
        o_ref[s, :, sl[i]] = (out * (zh * _sigmoid(zh))).astype(o_ref.dtype)


def _delta_rule(qkvn, ba, proj, neg_a, dt_bias, onorm_g, *, batch, t_len):
    rows = qkvn.shape[0]
    n_chunks = t_len // CHUNK

    n = DELTA_STREAMS
    assert batch % n == 0
    groups = batch // n
    by_stream = lambda a: a.reshape(groups, n, t_len, a.shape[-1])

    def chunk_block(col):
        return lambda g, c: (g, 0, lax.rem(c + n_chunks - 1, n_chunks), col)

    const = lambda g, c: (0, 0)
    out = pl.pallas_call(
        _delta_kernel,
        grid=(groups, n_chunks),
        in_specs=[
            pl.BlockSpec((None, n, CHUNK, DN_WIDTH), chunk_block(0)),
            pl.BlockSpec((None, n, CHUNK, DN_WIDTH), chunk_block(1)),
            pl.BlockSpec((None, n, CHUNK, DN_WIDTH), chunk_block(2)),
            pl.BlockSpec((None, n, CHUNK, BA_WIDTH), chunk_block(0)),
            pl.BlockSpec((None, n, CHUNK, DN_WIDTH), chunk_block(COL_Z // DN_WIDTH)),
            pl.BlockSpec((1, LANES), const),
            pl.BlockSpec((1, LANES), const),
            pl.BlockSpec((1, DN_HEAD_DIM), const),
        ],
        out_specs=pl.BlockSpec((None, n, CHUNK, DN_WIDTH), chunk_block(0)),
        out_shape=jax.ShapeDtypeStruct((groups, n, t_len, DN_WIDTH), BF16),
        scratch_shapes=[pltpu.VMEM((n * DN_HEADS, DN_HEAD_DIM, DN_HEAD_DIM), F32)],
        compiler_params=_params(("arbitrary", "arbitrary")),
        name="delta_rule",
    )(by_stream(qkvn), by_stream(qkvn), by_stream(qkvn), by_stream(ba), by_stream(proj), neg_a, dt_bias, onorm_g)
    return out.reshape(rows, DN_WIDTH)


def _merge_kernel(o_ref, ya_ref, gp_ref, gd_ref, h0_ref, wd_ref, wo_ref, g0_ref, b0_ref, g1_ref, b1_ref, h1_ref):
    y_b = _dot(o_ref[...], wd_ref[...])
    merged = _sigmoid(gp_ref[...].astype(F32)) * ya_ref[...] + _sigmoid(gd_ref[...].astype(F32)) * y_b
    mix = _dot(merged.astype(BF16), wo_ref[...])
    h = _layer_norm(h0_ref[...], g0_ref[...], b0_ref[...])
    h1_ref[...] = _layer_norm(ALPHA * h + mix, g1_ref[...], b1_ref[...])


def _merge(o, ya, proj, h0, w_br_delta, w_out, g0, b0, g1, b1, *, batch, seq, t_len):
    bf16_rows = 2 * SUBLANES
    tiles_per_batch = next(n for n in (10, 8, 4, 2, 1) if t_len % (n * bf16_rows) == 0)
    tm = t_len // tiles_per_batch
    row = lambda i: (i, 0)
    const = lambda i: (0, 0)
    return pl.pallas_call(
        _merge_kernel,
        grid=(batch * tiles_per_batch,),
        in_specs=[
            pl.BlockSpec((tm, DN_WIDTH), row),
            pl.BlockSpec((tm, D_MODEL), row),
            pl.BlockSpec((tm, D_MODEL), lambda i: (i, COL_GP // D_MODEL)),
            pl.BlockSpec((tm, D_MODEL), lambda i: (i, COL_GD // D_MODEL)),
            pl.BlockSpec((tm, D_MODEL), row),
            pl.BlockSpec((DN_WIDTH, D_MODEL), const),
            pl.BlockSpec((D_MODEL, D_MODEL), const),
            pl.BlockSpec((1, D_MODEL), const),
            pl.BlockSpec((1, D_MODEL), const),
            pl.BlockSpec((1, D_MODEL), const),
            pl.BlockSpec((1, D_MODEL), const),
        ],
        out_specs=pl.BlockSpec((tm, D_MODEL), row),
        out_shape=jax.ShapeDtypeStruct((batch * t_len, D_MODEL), F32),
        compiler_params=_params(("arbitrary",)),
        name="merge",
    )(o, ya, proj, proj, h0, w_br_delta, w_out, g0, b0, g1, b1)


def _tile(ref, k):
    return ref[k * SUBLANES:(k + 1) * SUBLANES, :]


def _argmax_tournament(vals, ids):
    nodes = list(zip(vals, ids))
    while len(nodes) > 1:
        nxt = []
        for p in range(0, len(nodes) - 1, 2):
            (va, ia), (vb, ib) = nodes[p], nodes[p + 1]
            first = va >= vb
            nxt.append((jnp.maximum(va, vb), jnp.where(first, ia, ib)))
        if len(nodes) % 2:
            nxt.append(nodes[-1])
        nodes = nxt
    return nodes[0]


def _query_kernel(x_ref, wq_ref, k1_ref, k2_ref, idx_ref, gate_ref, s1_ref, s2_ref, m1_ref, i1_ref, m2_ref, i2_ref,
                  c_ref, e_ref, sc_ref):
    xb = x_ref[...].astype(BF16)
    q = _dot(xb, wq_ref[...])
    for h in range(PEER_HEADS):
        q1 = q[:, (2 * h) * PEER_HALF:(2 * h + 1) * PEER_HALF].astype(BF16)
        q2 = q[:, (2 * h + 1) * PEER_HALF:(2 * h + 2) * PEER_HALF].astype(BF16)
        s1_ref[pl.ds(h, N_KEYS, stride=SUBLANES), :] = _dot_nt(k1_ref[h], q1)
        s2_ref[pl.ds(h, N_KEYS, stride=SUBLANES), :] = _dot_nt(k2_ref[h], q2)

    neg_inf = jnp.float32(-jnp.inf)

    def sub_round(r, carry):
        for s_ref, m_ref, i_ref in ((s1_ref, m1_ref, i1_ref), (s2_ref, m2_ref, i2_ref)):
            vals = [_tile(s_ref, k) for k in range(N_KEYS)]
            best, arg = _argmax_tournament(vals, list(range(N_KEYS)))
            m_ref[r] = best
            i_ref[r] = arg
            for k in range(N_KEYS):
                s_ref[k * SUBLANES:(k + 1) * SUBLANES, :] = jnp.where(arg == k, neg_inf, vals[k])
        return carry

    lax.fori_loop(0, PEER_TOPK, sub_round, 0)

    for j, (a, b) in enumerate(_CAND):
        c_ref[j * SUBLANES:(j + 1) * SUBLANES, :] = m1_ref[a] + m2_ref[b]
        e_ref[j * SUBLANES:(j + 1) * SUBLANES, :] = i1_ref[a] * N_KEYS + i2_ref[b]

    def pair_round(r, carry):
        vals = [_tile(c_ref, j) for j in range(len(_CAND))]
        ids = [_tile(e_ref, j) for j in range(len(_CAND))]
        best, arg = _argmax_tournament(vals, ids)
        sc_ref[r] = best
        idx_ref[r] = arg
        for j in range(len(_CAND)):
            c_ref[j * SUBLANES:(j + 1) * SUBLANES, :] = jnp.where(ids[j] == arg, neg_inf, vals[j])
        return carry

    lax.fori_loop(0, PEER_TOPK, pair_round, 0)

    sc = sc_ref[...]
    ex = jnp.exp(sc - sc[0:1])
    gate_ref[...] = ex / jnp.sum(ex, axis=0, keepdims=True)


def _token_tile_map(seq, tile):
    per_batch = seq // tile
    return lambda i, *_: (i // per_batch, lax.rem(i, per_batch), 0)


def _peer_query(h1, wq, k1, k2, *, seq, tq):
    n_tok = h1.shape[0] * seq
    slot_shape = (PEER_TOPK, PEER_HEADS, tq)
    return pl.pallas_call(
        _query_kernel,
        grid=(n_tok // tq,),
        in_specs=[
            pl.BlockSpec((None, tq, D_MODEL), _token_tile_map(seq, tq)),
            pl.BlockSpec((D_MODEL, 2 * PEER_HALF * PEER_HEADS), lambda i: (0, 0)),
            pl.BlockSpec((PEER_HEADS, N_KEYS, PEER_HALF), lambda i: (0, 0, 0)),
            pl.BlockSpec((PEER_HEADS, N_KEYS, PEER_HALF), lambda i: (0, 0, 0)),
        ],
        out_specs=[
            pl.BlockSpec(slot_shape, lambda i: (0, 0, i)),
            pl.BlockSpec(slot_shape, lambda i: (0, 0, i)),
        ],
        out_shape=[
            jax.ShapeDtypeStruct((PEER_TOPK, PEER_HEADS, n_tok), I32),
            jax.ShapeDtypeStruct((PEER_TOPK, PEER_HEADS, n_tok), F32),
        ],
        scratch_shapes=[
            pltpu.VMEM((N_KEYS * SUBLANES, tq), F32),
            pltpu.VMEM((N_KEYS * SUBLANES, tq), F32),
            pltpu.VMEM(slot_shape, F32),
            pltpu.VMEM(slot_shape, I32),
            pltpu.VMEM(slot_shape, F32),
            pltpu.VMEM(slot_shape, I32),
            pltpu.VMEM((len(_CAND) * SUBLANES, tq), F32),
            pltpu.VMEM((len(_CAND) * SUBLANES, tq), I32),
            pltpu.VMEM(slot_shape, F32),
        ],
        compiler_params=_params(("arbitrary",)),
        name="peer_query",
    )(h1, wq, k1, k2)


IA_PER_STEP = 16
EXPERTS_PER_STEP = IA_PER_STEP * N_KEYS
PEER_STEPS = (N_KEYS * N_KEYS) // EXPERTS_PER_STEP
PEER_PIECES = 2
TABLE_STREAMS = 4


def _table_specs():
    rows = EXPERTS_PER_STEP // TABLE_STREAMS
    return [pl.BlockSpec((rows, D_MODEL), lambda i, j, p=p: (j * TABLE_STREAMS + p, 0)) for p in range(TABLE_STREAMS)]


def _one_hot_rows(idx_row):
    iota = lax.broadcasted_iota(I32, (N_KEYS, N_SLOTS), 0)
    return jnp.where(iota == idx_row, 1.0, 0.0).astype(BF16)


def _block_diag(a, b):
    zero = jnp.zeros_like(a)
    return jnp.concatenate([jnp.concatenate([a, zero], axis=1), jnp.concatenate([zero, b], axis=1)], axis=0)


def _hidden_kernel(x_ref, *refs, tb, hp):
    u_refs = refs[:TABLE_STREAMS]
    ia_ref, ib_ref, gate_ref, act_ref, xb_ref, h0_ref, h1_ref, hs_ref = refs[TABLE_STREAMS:]
    i = pl.program_id(0)
    j = pl.program_id(1)
    tokens = tb // PEER_STEPS // PEER_PIECES
    parts = TABLE_STREAMS // PEER_PIECES
    ia_pairs = IA_PER_STEP // PEER_PIECES // 2

    @pl.when(jnp.logical_and(i == 0, j == 0))
    def _():
        h1_ref[...] = jnp.zeros_like(h1_ref)

    @pl.when(j == 0)
    def _():
        xb_ref[...] = x_ref[...].astype(BF16)

    def step(fill_ref, drain_ref):
        iota = lax.broadcasted_iota(I32, (N_KEYS, N_SLOTS), 0)
        for q in range(PEER_PIECES):
            base = pl.multiple_of((j * PEER_PIECES + q) * tokens, tokens)
            ia_rows = ia_ref[pl.ds(base, tokens), :]
            ib_rows = ib_ref[pl.ds(base, tokens), :]
            lhs, rhs = [], []
            for p in range(tokens // 2):
                tiles = []
                for t in (2 * p, 2 * p + 1):
                    start = pl.multiple_of((base + t) * hp, SUBLANES)
                    tiles.append(pltpu.bitcast(drain_ref[pl.ds(start, N_KEYS // 2), :], BF16))
                lhs.append(jnp.concatenate(tiles, axis=1))
                rhs.append(_block_diag(_one_hot_rows(ib_rows[2 * p:2 * p + 1, :]),
                                       _one_hot_rows(ib_rows[2 * p + 1:2 * p + 2, :])))

            experts = jnp.concatenate([u_refs[q * parts + p][...] for p in range(parts)], axis=0)
            hc = _dot_nt(xb_ref[...], experts)
            for m in range(ia_pairs):
                even = hc[:, (2 * m) * N_KEYS:(2 * m + 1) * N_KEYS]
                odd = hc[:, (2 * m + 1) * N_KEYS:(2 * m + 2) * N_KEYS]
                packed = pltpu.pack_elementwise([even, odd], packed_dtype=BF16)
                pair = (j * PEER_PIECES + q) * ia_pairs + m
                fill_ref[pl.ds(pair, tb, stride=hp), :] = lax.bitcast_convert_type(packed, I32)

            rows = []
            for p in range(tokens // 2):
                both = _dot(lhs[p], rhs[p])
                for k in range(2):
                    mine = both[:, k * N_SLOTS:(k + 1) * N_SLOTS]
                    picked = jnp.where(iota == ia_rows[2 * p + k:2 * p + k + 1, :], mine, 0.0)
                    rows.append(jnp.sum(picked, axis=0, keepdims=True))
            hs_ref[pl.ds(base, tokens), :] = jnp.concatenate(rows, axis=0)

    parity = lax.rem(i, 2)

    @pl.when(parity == 0)
    def _():
        step(h0_ref, h1_ref)

    @pl.when(parity == 1)
    def _():
        step(h1_ref, h0_ref)

    @pl.when(j == PEER_STEPS - 1)
    def _():
        hid = hs_ref[...]
        gelu = 0.5 * hid * (1.0 + lax.erf(hid * (2.0 ** -0.5)))
        act_ref[...] = gelu * gate_ref[...]


def _peer_hidden(h1, u_tab, ia, ib, gate, *, seq, tb):
    n_tiles = h1.shape[0] * seq // tb
    hp = N_KEYS // 2 + SUBLANES
    x_map = _token_tile_map(seq, tb)
    ahead = lambda i, j: x_map(jnp.minimum(i, n_tiles - 1))
    behind = lambda i, j: (jnp.maximum(i - 1, 0), 0)
    return pl.pallas_call(
        functools.partial(_hidden_kernel, tb=tb, hp=hp),
        grid=(n_tiles + 1, PEER_STEPS),
        in_specs=[
            pl.BlockSpec((None, tb, D_MODEL), ahead),
            *_table_specs(),
            pl.BlockSpec((tb, N_SLOTS), behind),
            pl.BlockSpec((tb, N_SLOTS), behind),
            pl.BlockSpec((tb, N_SLOTS), behind),
        ],
        out_specs=pl.BlockSpec((tb, N_SLOTS), behind),
        out_shape=jax.ShapeDtypeStruct((n_tiles * tb, N_SLOTS), F32),
        scratch_shapes=[
            pltpu.VMEM((tb, D_MODEL), BF16),
            pltpu.VMEM((tb * hp, N_KEYS), I32),
            pltpu.VMEM((tb * hp, N_KEYS), I32),
            pltpu.VMEM((tb, N_SLOTS), F32),
        ],
        compiler_params=_params(("arbitrary", "arbitrary")),
        name="peer_hidden",
    )(h1, *([u_tab] * TABLE_STREAMS), ia, ib, gate)


def _output_kernel(act_ref, ia_ref, ib_ref, *refs, tb, hp):
    v_refs = refs[:TABLE_STREAMS]
    h1_ref, g_ref, b_ref, out_ref, a0_ref, a1_ref, acc_ref = refs[TABLE_STREAMS:]
    i = pl.program_id(0)
    j = pl.program_id(1)
    tokens = tb // PEER_STEPS // PEER_PIECES
    parts = TABLE_STREAMS // PEER_PIECES
    ia_per_piece = IA_PER_STEP // PEER_PIECES

    @pl.when(jnp.logical_and(i == 0, j == 0))
    def _():
        a1_ref[...] = jnp.zeros_like(a1_ref)

    @pl.when(j == 0)
    def _():
        acc_ref[...] = jnp.zeros_like(acc_ref)

    def step(fill_ref, drain_ref):
        iota = lax.broadcasted_iota(I32, (N_KEYS, N_SLOTS), 0)
        acc = acc_ref[...]
        for q in range(PEER_PIECES):
            base = pl.multiple_of((j * PEER_PIECES + q) * tokens, tokens)
            ia_rows = ia_ref[pl.ds(base, tokens), :]
            ib_rows = ib_ref[pl.ds(base, tokens), :]
            act_rows = act_ref[pl.ds(base, tokens), :]
            lhs, sels = [], []
            for p in range(tokens // 2):
                spread = [jnp.where(iota == ia_rows[t:t + 1, :], act_rows[t:t + 1, :], 0.0).astype(BF16)
                          for t in (2 * p, 2 * p + 1)]
                lhs.append(jnp.concatenate(spread, axis=1))
                sels += [_one_hot_rows(ib_rows[t:t + 1, :]) for t in (2 * p, 2 * p + 1)]

            tiles = []
            for t in range(ia_per_piece):
                ia = j * IA_PER_STEP + q * ia_per_piece + t
                start = pl.multiple_of(ia * hp, SUBLANES)
                tiles.append(pltpu.bitcast(drain_ref[pl.ds(start, tb // 2), :], BF16))
            table_rows = jnp.concatenate([v_refs[q * parts + p][...] for p in range(parts)], axis=0)
            acc = acc + _dot(jnp.concatenate(tiles, axis=1), table_rows)

            pair_base = (j * PEER_PIECES + q) * (tokens // 2)
            for p in range(tokens // 2):
                both = _dot_nt(lhs[p], _block_diag(sels[2 * p], sels[2 * p + 1]))
                packed = pltpu.pack_elementwise([both[:, :N_KEYS], both[:, N_KEYS:]], packed_dtype=BF16)
                fill_ref[pl.ds(pair_base + p, N_KEYS, stride=hp), :] = lax.bitcast_convert_type(packed, I32)
        acc_ref[...] = acc

    parity = lax.rem(i, 2)

    @pl.when(parity == 0)
    def _():
        step(a0_ref, a1_ref)

    @pl.when(parity == 1)
    def _():
        step(a1_ref, a0_ref)

    @pl.when(j == PEER_STEPS - 1)
    def _():
        out_ref[...] = _layer_norm(ALPHA * h1_ref[...] + acc_ref[...], g_ref[...], b_ref[...])


def _peer_output(act, ia, ib, v_tab, h1, g2, b2, *, seq, tb):
    n_tiles = h1.shape[0] * seq // tb
    hp = tb // 2 + SUBLANES
    x_map = _token_tile_map(seq, tb)
    ahead = lambda i, j: (jnp.minimum(i, n_tiles - 1), 0)
    behind = lambda i, j: (jnp.maximum(i - 1, 0), 0)
    const = lambda i, j: (0, 0)
    return pl.pallas_call(
        functools.partial(_output_kernel, tb=tb, hp=hp),
        grid=(n_tiles + 1, PEER_STEPS),
        in_specs=[
            pl.BlockSpec((tb, N_SLOTS), ahead),
            pl.BlockSpec((tb, N_SLOTS), ahead),
            pl.BlockSpec((tb, N_SLOTS), ahead),
            *_table_specs(),
            pl.BlockSpec((None, tb, D_MODEL), lambda i, j: x_map(jnp.maximum(i - 1, 0))),
            pl.BlockSpec((1, D_MODEL), const),
            pl.BlockSpec((1, D_MODEL), const),
        ],
        out_specs=pl.BlockSpec((tb, D_MODEL), behind),
        out_shape=jax.ShapeDtypeStruct((n_tiles * tb, D_MODEL), F32),
        scratch_shapes=[
            pltpu.VMEM((N_KEYS * hp, N_KEYS), I32),
            pltpu.VMEM((N_KEYS * hp, N_KEYS), I32),
            pltpu.VMEM((tb, D_MODEL), F32),
        ],
        compiler_params=_params(("arbitrary", "arbitrary")),
        name="peer_output",
    )(act, ia, ib, *([v_tab] * TABLE_STREAMS), h1, g2, b2)


def _pick_tile(n, candidates):
    for c in candidates:
        if n % c == 0:
            return c
    raise ValueError(f"no tile in {candidates} divides {n}")


def kernel(x, meta, ln0_g, ln0_b, w_in, pool_w, pool_scale, w_br_pool, conv_w, a_log, dt_bias, onorm_g, w_br_delta, w_out, ln1_g, ln1_b, peer_wq, peer_k1, peer_k2, peer_u, peer_v, ln2_g, ln2_b):
    batch, seq, d = x.shape
    assert d == D_MODEL and seq % CHUNK == 0 and w_in.shape[0] == 1
    t_len = seq + CHUNK
    n_tok = batch * seq
    row = lambda p: p.reshape(1, -1).astype(F32)

    meta_b = jnp.broadcast_to(meta[None].astype(x.dtype), (batch, N_META, d))
    h0 = jnp.concatenate([x, jnp.zeros((batch, PAD, d), x.dtype), meta_b], axis=1).reshape(batch * t_len, d)

    w = w_in[0]
    c_pool, c_qkv, c_z = POOL_WIDTH, POOL_WIDTH + 3 * DN_WIDTH, POOL_WIDTH + 4 * DN_WIDTH
    c_b, c_a = c_z + DN_HEADS, c_z + 2 * DN_HEADS
    w_main = jnp.concatenate([w[:, c_pool:c_z], w[:, c_a:], w[:, :c_pool]], axis=1).astype(BF16)
    w_ba = jnp.zeros((d, BA_WIDTH), F32)
    w_ba = w_ba.at[:, :DN_HEADS].set(w[:, c_z:c_b]).at[:, LANES:LANES + DN_HEADS].set(w[:, c_b:c_a]).astype(BF16)
    lane_pad = lambda p: jnp.zeros((1, LANES), F32).at[0, :DN_HEADS].set(p.astype(F32))

    proj, ba = _inproj(h0, row(ln0_g), row(ln0_b), w_main, w_ba, seq=seq, t_len=t_len)
    ya = _pool_branch(proj, pool_w[0].astype(BF16), row(pool_scale[0]), w_br_pool[0].astype(BF16),
                      batch=batch, seq=seq, t_len=t_len)
    qkvn = _conv_qkv(proj, conv_w[0].astype(F32), batch=batch, seq=seq, t_len=t_len)
    o = _delta_rule(qkvn, ba, proj, lane_pad(-jnp.exp(a_log[0].astype(F32))), lane_pad(dt_bias[0]),
                    row(onorm_g[0]), batch=batch, t_len=t_len)
    h1 = _merge(o, ya, proj, h0, w_br_delta[0].astype(BF16), w_out[0].astype(BF16), row(ln0_g), row(ln0_b),
                row(ln1_g[0]), row(ln1_b[0]), batch=batch, seq=seq, t_len=t_len)
    h1 = h1.reshape(batch, t_len, d)

    idx, gate = _peer_query(h1, peer_wq[0].astype(BF16), peer_k1[0].astype(BF16), peer_k2[0].astype(BF16),
                            seq=seq, tq=LANES)
    idx = idx.reshape(N_SLOTS, n_tok).T
    gate = gate.reshape(N_SLOTS, n_tok).T
    ia = lax.shift_right_logical(idx, 7)
    ib = lax.bitwise_and(idx, N_KEYS - 1)

    act = _peer_hidden(h1, peer_u[0].astype(BF16), ia, ib, gate, seq=seq, tb=_pick_tile(seq, (512,)))
    out = _peer_output(act, ia, ib, peer_v[0].astype(BF16), h1, row(ln2_g[0]), row(ln2_b[0]),
                       seq=seq, tb=_pick_tile(seq, (512,)))
    return out.reshape(batch, seq, d)
```

```python
import functools
import math

import jax
import jax.numpy as jnp
from jax import lax
from jax.experimental import pallas as pl
from jax.experimental.pallas import tpu as pltpu

F32 = jnp.float32
BF16 = jnp.bfloat16
I32 = jnp.int32

D_MODEL = 1024
N_META = 16
CHUNK = 64
PAD = CHUNK - N_META
POOL_GROUPS = 4
POOL_GROUP_DIM = 128
POOL_WIDTH = POOL_GROUPS * POOL_GROUP_DIM
POOL_WINDOWS = (2, 4, 8, 16)
POOL_HALO = 16
DN_HEADS = 8
DN_HEAD_DIM = 128
DN_WIDTH = DN_HEADS * DN_HEAD_DIM
CONV_WIDTH = 4
PEER_HEADS = 8
PEER_HALF = 128
N_KEYS = 128
PEER_TOPK = 16
N_SLOTS = PEER_HEADS * PEER_TOPK
LN_EPS = 1e-5
RMS_EPS = 1e-6
L2_EPS = 1e-6
ALPHA = 2.0 ** 0.25

COL_QKV = 0
COL_Z = 3 * DN_WIDTH
COL_GP = COL_Z + DN_WIDTH
COL_GD = COL_GP + D_MODEL
COL_POOL = COL_GD + D_MODEL
N_PROJ = COL_POOL + POOL_WIDTH
BA_WIDTH = 256

SUBLANES = 8
LANES = 128
VMEM_LIMIT = 60 * 1024 * 1024

_CAND = [(a, b) for a in range(PEER_TOPK) for b in range(PEER_TOPK) if (a + 1) * (b + 1) <= PEER_TOPK]


def _params(sem):
    return pltpu.CompilerParams(dimension_semantics=sem, vmem_limit_bytes=VMEM_LIMIT)


def _layer_norm(x, g, b):
    mu = jnp.mean(x, axis=-1, keepdims=True)
    xc = x - mu
    var = jnp.mean(xc * xc, axis=-1, keepdims=True)
    return xc * lax.rsqrt(var + LN_EPS) * g + b


def _zero_pad_rows(y, row, seq):
    return jnp.where(row >= seq, jnp.where(row < seq + PAD, 0.0, y), y)


def _sigmoid(x):
    return 1.0 / (1.0 + jnp.exp(-x))


def _dot(a, b):
    return jnp.dot(a.astype(BF16), b.astype(BF16), preferred_element_type=F32)


def _dot_nt(a, b):
    return lax.dot_general(a.astype(BF16), b.astype(BF16), (((1,), (1,)), ((), ())), preferred_element_type=F32)


def _assemble_rows(x_ref, tail_ref, rows_ref, tile_in_batch, tiles_per_batch):
    body = rows_ref.shape[0] - CHUNK

    @pl.when(tile_in_batch < tiles_per_batch - 1)
    def _():
        rows_ref[...] = x_ref[...]

    @pl.when(tile_in_batch == tiles_per_batch - 1)
    def _():
        rows_ref[:body] = x_ref[:body]
        rows_ref[body:] = tail_ref[...]


def _inproj_kernel(x_ref, tail_ref, g_ref, b_ref, w_ref, wba_ref, proj_ref, ba_ref, rows_ref, xn_ref, *,
                   tiles_per_batch, seq):
    i = pl.program_id(0)
    j = pl.program_id(1)
    tm = x_ref.shape[0]

    @pl.when(j == 0)
    def _():
        _assemble_rows(x_ref, tail_ref, rows_ref, lax.rem(i, tiles_per_batch), tiles_per_batch)
        y = _layer_norm(rows_ref[...], g_ref[...], b_ref[...])
        row = lax.broadcasted_iota(I32, (tm, 1), 0) + lax.rem(i, tiles_per_batch) * tm
        xn = _zero_pad_rows(y, row, seq).astype(BF16)
        xn_ref[...] = xn
        ba_ref[...] = _dot(xn, wba_ref[...])

    proj_ref[...] = _dot(xn_ref[...], w_ref[...]).astype(proj_ref.dtype)


def _inproj(x, tail, ln_g, ln_b, w_main, w_ba, *, seq, t_len):
    rows = x.shape[0] * t_len
    tiles_per_batch = 4
    tm = t_len // tiles_per_batch
    tn = N_PROJ // 4
    kern = functools.partial(_inproj_kernel, tiles_per_batch=tiles_per_batch, seq=seq)
    return pl.pallas_call(
        kern,
        grid=(rows // tm, N_PROJ // tn),
        in_specs=[
            pl.BlockSpec((None, tm, D_MODEL), lambda i, j: (i // tiles_per_batch, lax.rem(i, tiles_per_batch), 0)),
            pl.BlockSpec((CHUNK, D_MODEL), lambda i, j: (0, 0)),
            pl.BlockSpec((1, D_MODEL), lambda i, j: (0, 0)),
            pl.BlockSpec((1, D_MODEL), lambda i, j: (0, 0)),
            pl.BlockSpec((D_MODEL, tn), lambda i, j: (0, j)),
            pl.BlockSpec((D_MODEL, BA_WIDTH), lambda i, j: (0, 0)),
        ],
        out_specs=[
            pl.BlockSpec((tm, tn), lambda i, j: (i, j)),
            pl.BlockSpec((tm, BA_WIDTH), lambda i, j: (i, 0)),
        ],
        out_shape=[
            jax.ShapeDtypeStruct((rows, N_PROJ), BF16),
            jax.ShapeDtypeStruct((rows, BA_WIDTH), F32),
        ],
        scratch_shapes=[pltpu.VMEM((tm, D_MODEL), F32), pltpu.VMEM((tm, D_MODEL), BF16)],
        compiler_params=_params(("arbitrary", "arbitrary")),
        name="inproj",
    )(x, tail, ln_g, ln_b, w_main, w_ba)


def _pool_branch_tile(xp_ref, halo_ref, pw_ref, ps_ref, wbp_ref, first_row, seq):
    tm = xp_ref.shape[0]
    v = jnp.concatenate([halo_ref[...].astype(F32), xp_ref[...].astype(F32)], axis=0)
    row = lax.broadcasted_iota(I32, (tm, POOL_GROUP_DIM), 0) + first_row
    meta_pos = row - (seq + PAD)
    outs = []
    for gi, w in enumerate(POOL_WINDOWS):
        vg = v[:, gi * POOL_GROUP_DIM:(gi + 1) * POOL_GROUP_DIM]
        s = vg
        shift = 1
        while shift < w:
            s = s + pltpu.roll(s, shift, axis=0)
            shift *= 2
        s = s[POOL_HALO:]
        xg = vg[POOL_HALO:]
        count = jnp.where(meta_pos >= 0, jnp.minimum(meta_pos + 1, w), w).astype(F32)
        pooled = s / count - xg
        mixed = _dot(pooled.astype(BF16), pw_ref[gi])
        outs.append(mixed)
    y_pool = jnp.concatenate(outs, axis=1) * ps_ref[...]
    return _dot(y_pool.astype(BF16), wbp_ref[...])


def _conv_kernel(x_ref, w_ref, o_ref, *, seq):
    s = pl.program_id(1)
    x = x_ref[...].astype(F32)
    t_len = x.shape[0]
    w = w_ref[...]
    y = x * w[CONV_WIDTH - 1:CONV_WIDTH, :]
    for lag in range(1, CONV_WIDTH):
        y = y + pltpu.roll(x, lag, axis=0) * w[CONV_WIDTH - 1 - lag:CONV_WIDTH - lag, :]
    y = y * _sigmoid(y)
    ss = jnp.sum(y * y, axis=-1, keepdims=True)
    q_scale = jnp.where(s < DN_HEADS, DN_HEAD_DIM ** -0.5, 1.0).astype(F32)
    fac = jnp.where(s < 2 * DN_HEADS, lax.rsqrt(ss + L2_EPS) * q_scale, 1.0)
    row = lax.broadcasted_iota(I32, (t_len, 1), 0)
    o_ref[...] = _zero_pad_rows(y * fac, row, seq).astype(o_ref.dtype)


def _conv_qkv(proj, conv_w, *, batch, seq, t_len):
    rows = proj.shape[0]
    n_slabs = 3 * DN_HEADS
    return pl.pallas_call(
        functools.partial(_conv_kernel, seq=seq),
        grid=(batch, n_slabs),
        in_specs=[
            pl.BlockSpec((t_len, DN_HEAD_DIM), lambda b, s: (b, COL_QKV // DN_HEAD_DIM + s)),
            pl.BlockSpec((CONV_WIDTH, DN_HEAD_DIM), lambda b, s: (0, s)),
        ],
        out_specs=pl.BlockSpec((t_len, DN_HEAD_DIM), lambda b, s: (b, s)),
        out_shape=jax.ShapeDtypeStruct((rows, 3 * DN_WIDTH), BF16),
        compiler_params=_params(("arbitrary", "arbitrary")),
        name="conv_qkv",
    )(proj, conv_w)


def _inv_unit_lower(a, masks, eye):
    m8, m16, m32, m64 = masks
    n = [x * m8 for x in a]
    n2 = [_dot(x, x) for x in n]
    n4 = [_dot(x, x) for x in n2]
    t = [eye - x for x in n]
    t = [x + _dot(x, y) for x, y in zip(t, n2)]
    t = [x + _dot(x, y) for x, y in zip(t, n4)]
    for m in (m16, m32, m64):
        at = [_dot(x * m, y) for x, y in zip(a, t)]
        t = [x - _dot(x, y) for x, y in zip(t, at)]
    return t


DELTA_STREAMS = 4


def _delta_kernel(q_ref, k_ref, v_ref, ba_ref, z_ref, nega_ref, dtb_ref, og_ref, o_ref, state_ref):
    n = DELTA_STREAMS
    c = pl.program_id(1)

    @pl.when(c == 0)
    def _():
        state_ref[...] = jnp.zeros_like(state_ref)

    r = lax.broadcasted_iota(I32, (CHUNK, CHUNK), 0)
    cc = lax.broadcasted_iota(I32, (CHUNK, CHUNK), 1)
    causal = r >= cc
    strict = r > cc
    eye = jnp.where(r == cc, 1.0, 0.0).astype(F32)
    tril = jnp.where(causal, 1.0, 0.0).astype(F32)

    def blk(x, s):
        return lax.shift_right_logical(x, s)

    lower = jnp.where(strict, 1.0, 0.0).astype(F32)
    m8 = jnp.where(blk(r, 3) == blk(cc, 3), lower, 0.0)
    m16 = jnp.where((blk(r, 4) == blk(cc, 4)) & (blk(r, 3) != blk(cc, 3)), lower, 0.0)
    m32 = jnp.where((blk(r, 5) == blk(cc, 5)) & (blk(r, 4) != blk(cc, 4)), lower, 0.0)
    m64 = jnp.where(blk(r, 5) != blk(cc, 5), lower, 0.0)
    masks = (m8, m16, m32, m64)

    beta_all, gcum, gcum_t = [], [], []
    for s in range(n):
        ba = ba_ref[s]
        beta_all.append(_sigmoid(ba[:, :LANES]))
        x = ba[:, LANES:] + dtb_ref[...]
        softplus = jnp.maximum(x, 0.0) + jnp.log(1.0 + jnp.exp(-jnp.abs(x)))
        g_all = nega_ref[...] * softplus
        gcum.append(lax.dot_general(tril, g_all, (((1,), (0,)), ((), ())), precision=lax.Precision.HIGHEST,
                                    preferred_element_type=F32))
        gcum_t.append(gcum[s].T)

    chains = [(s, h) for s in range(n) for h in range(DN_HEADS)]
    ids = range(len(chains))
    sl = [slice(h * DN_HEAD_DIM, (h + 1) * DN_HEAD_DIM) for _, h in chains]
    q = [q_ref[s, :, sl[i]].astype(F32) for i, (s, h) in enumerate(chains)]
    k = [k_ref[s, :, sl[i]].astype(F32) for i, (s, h) in enumerate(chains)]
    v = [v_ref[s, :, sl[i]].astype(F32) for i, (s, h) in enumerate(chains)]
    beta = [beta_all[s][:, h:h + 1] for s, h in chains]
    gc = [gcum[s][:, h:h + 1] for s, h in chains]
    gr = [gcum_t[s][h:h + 1, :] for s, h in chains]
    g_last = [gcum[s][CHUNK - 1:CHUNK, h:h + 1] for s, h in chains]
    decay = [jnp.where(causal, jnp.exp(jnp.where(causal, gc[i] - gr[i], 0.0)), 0.0) for i in ids]
    eg = [jnp.exp(gc[i]) for i in ids]
    kb = [k[i] * beta[i] for i in ids]
    kq = [_dot_nt(jnp.concatenate([kb[i], q[i]], axis=0), k[i]) for i in ids]
    a_mat = [jnp.where(strict, kq[i][:CHUNK] * decay[i], 0.0) for i in ids]
    t_inv = _inv_unit_lower(a_mat, masks, eye)
    rhs = [jnp.concatenate([v[i] * beta[i], kb[i] * eg[i]], axis=1) for i in ids]
    uw = [_dot(t_inv[i], rhs[i]) for i in ids]
    s_prev = [state_ref[i] for i in ids]
    wq = [jnp.concatenate([uw[i][:, DN_HEAD_DIM:], q[i] * eg[i]], axis=0) for i in ids]
    ws = [_dot(wq[i], s_prev[i]) for i in ids]
    v_new = [uw[i][:, :DN_HEAD_DIM] - ws[i][:CHUNK] for i in ids]
    attn = [kq[i][CHUNK:] * decay[i] for i in ids]
    intra = [_dot(attn[i], v_new[i]) for i in ids]
    k_dec_t = [(k[i] * jnp.exp(g_last[i] - gc[i])).T for i in ids]
    kv = [_dot(k_dec_t[i], v_new[i]) for i in ids]
    for i, (s, h) in enumerate(chains):
        state_ref[i] = s_prev[i] * jnp.exp(g_last[i]) + kv[i]
        out = ws[i][CHUNK:] + intra[i]
        out = out * lax.rsqrt(jnp.mean(out * out, axis=-1, keepdims=True) + RMS_EPS) * og_ref[...]
        zh = z_ref[s, :, sl[i]].astype(F32)
        o_ref[s, :, sl[i]] = (out * (zh * _sigmoid(zh))).astype(o_ref.dtype)


def _delta_rule(qkvn, ba, proj, neg_a, dt_bias, onorm_g, *, batch, t_len):
    rows = qkvn.shape[0]
    n_chunks = t_len // CHUNK

    n = DELTA_STREAMS
    assert batch % n == 0
    groups = batch // n
    by_stream = lambda a: a.reshape(groups, n, t_len, a.shape[-1])

    def chunk_block(col):
        return lambda g, c: (g, 0, lax.rem(c + n_chunks - 1, n_chunks), col)

    const = lambda g, c: (0, 0)
    out = pl.pallas_call(
        _delta_kernel,
        grid=(groups, n_chunks),
        in_specs=[
            pl.BlockSpec((None, n, CHUNK, DN_WIDTH), chunk_block(0)),
            pl.BlockSpec((None, n, CHUNK, DN_WIDTH), chunk_block(1)),
            pl.BlockSpec((None, n, CHUNK, DN_WIDTH), chunk_block(2)),
            pl.BlockSpec((None, n, CHUNK, BA_WIDTH), chunk_block(0)),
            pl.BlockSpec((None, n, CHUNK, DN_WIDTH), chunk_block(COL_Z // DN_WIDTH)),
            pl.BlockSpec((1, LANES), const),
            pl.BlockSpec((1, LANES), const),
            pl.BlockSpec((1, DN_HEAD_DIM), const),
        ],
        out_specs=pl.BlockSpec((None, n, CHUNK, DN_WIDTH), chunk_block(0)),
        out_shape=jax.ShapeDtypeStruct((groups, n, t_len, DN_WIDTH), BF16),
        scratch_shapes=[pltpu.VMEM((n * DN_HEADS, DN_HEAD_DIM, DN_HEAD_DIM), F32)],
        compiler_params=_params(("arbitrary", "arbitrary")),
        name="delta_rule",
    )(by_stream(qkvn), by_stream(qkvn), by_stream(qkvn), by_stream(ba), by_stream(proj), neg_a, dt_bias, onorm_g)
    return out.reshape(rows, DN_WIDTH)


def _merge_kernel(o_ref, xp_ref, halo_ref, gp_ref, gd_ref, x_ref, tail_ref, wd_ref, wo_ref, pw_ref, ps_ref, wbp_ref,
                  g0_ref, b0_ref, g1_ref, b1_ref, h1_ref, rows_ref, *, tiles_per_batch, seq):
    tile_in_batch = lax.rem(pl.program_id(0), tiles_per_batch)
    first_row = tile_in_batch * o_ref.shape[0]
    y_a = _pool_branch_tile(xp_ref, halo_ref, pw_ref, ps_ref, wbp_ref, first_row, seq)
    y_b = _dot(o_ref[...], wd_ref[...])
    merged = _sigmoid(gp_ref[...].astype(F32)) * y_a + _sigmoid(gd_ref[...].astype(F32)) * y_b
    mix = _dot(merged.astype(BF16), wo_ref[...])
    _assemble_rows(x_ref, tail_ref, rows_ref, tile_in_batch, tiles_per_batch)
    h = _layer_norm(rows_ref[...], g0_ref[...], b0_ref[...])
    h1_ref[...] = _layer_norm(ALPHA * h + mix, g1_ref[...], b1_ref[...])


def _merge(o, proj, x, tail, w_br_delta, w_out, pool_w, pool_scale, w_br_pool, g0, b0, g1, b1, *, batch, seq, t_len):
    bf16_rows = 2 * SUBLANES
    tiles_per_batch = next(n for n in (10, 8, 4, 2, 1) if t_len % (n * bf16_rows) == 0)
    tm = t_len // tiles_per_batch
    halo_blocks_per_batch = t_len // POOL_HALO
    halo_blocks_per_tile = tm // POOL_HALO
    pool_col = COL_POOL // POOL_WIDTH
    row = lambda i: (i, 0)
    const = lambda i: (0, 0)

    def halo_map(i):
        b, t = i // tiles_per_batch, lax.rem(i, tiles_per_batch)
        prev = lax.rem(t * halo_blocks_per_tile + halo_blocks_per_batch - 1, halo_blocks_per_batch)
        return (b * halo_blocks_per_batch + prev, pool_col)

    return pl.pallas_call(
        functools.partial(_merge_kernel, tiles_per_batch=tiles_per_batch, seq=seq),
        grid=(batch * tiles_per_batch,),
        in_specs=[
            pl.BlockSpec((tm, DN_WIDTH), row),
            pl.BlockSpec((tm, POOL_WIDTH), lambda i: (i, pool_col)),
            pl.BlockSpec((POOL_HALO, POOL_WIDTH), halo_map),
            pl.BlockSpec((tm, D_MODEL), lambda i: (i, COL_GP // D_MODEL)),
            pl.BlockSpec((tm, D_MODEL), lambda i: (i, COL_GD // D_MODEL)),
            pl.BlockSpec((None, tm, D_MODEL), lambda i: (i // tiles_per_batch, lax.rem(i, tiles_per_batch), 0)),
            pl.BlockSpec((CHUNK, D_MODEL), const),
            pl.BlockSpec((DN_WIDTH, D_MODEL), const),
            pl.BlockSpec((D_MODEL, D_MODEL), const),
            pl.BlockSpec((POOL_GROUPS, POOL_GROUP_DIM, POOL_GROUP_DIM), lambda i: (0, 0, 0)),
            pl.BlockSpec((1, POOL_WIDTH), const),
            pl.BlockSpec((POOL_WIDTH, D_MODEL), const),
            pl.BlockSpec((1, D_MODEL), const),
            pl.BlockSpec((1, D_MODEL), const),
            pl.BlockSpec((1, D_MODEL), const),
            pl.BlockSpec((1, D_MODEL), const),
        ],
        out_specs=pl.BlockSpec((tm, D_MODEL), row),
        out_shape=jax.ShapeDtypeStruct((batch * t_len, D_MODEL), F32),
        scratch_shapes=[pltpu.VMEM((tm, D_MODEL), F32)],
        compiler_params=_params(("arbitrary",)),
        name="merge",
    )(o, proj, proj, proj, proj, x, tail, w_br_delta, w_out, pool_w, pool_scale, w_br_pool, g0, b0, g1, b1)


def _tile(ref, k):
    return ref[k * SUBLANES:(k + 1) * SUBLANES, :]


def _argmax_tournament(vals, ids):
    nodes = list(zip(vals, ids))
    while len(nodes) > 1:
        nxt = []
        for p in range(0, len(nodes) - 1, 2):
            (va, ia), (vb, ib) = nodes[p], nodes[p + 1]
            first = va >= vb
            nxt.append((jnp.maximum(va, vb), jnp.where(first, ia, ib)))
        if len(nodes) % 2:
            nxt.append(nodes[-1])
        nodes = nxt
    return nodes[0]


def _query_kernel(x_ref, wq_ref, k1_ref, k2_ref, idx_ref, gate_ref, s1_ref, s2_ref, m1_ref, i1_ref, m2_ref, i2_ref,
                  c_ref, e_ref, sc_ref):
    xb = x_ref[...].astype(BF16)
    q = _dot(xb, wq_ref[...])
    for h in range(PEER_HEADS):
        q1 = q[:, (2 * h) * PEER_HALF:(2 * h + 1) * PEER_HALF].astype(BF16)
        q2 = q[:, (2 * h + 1) * PEER_HALF:(2 * h + 2) * PEER_HALF].astype(BF16)
        s1_ref[pl.ds(h, N_KEYS, stride=SUBLANES), :] = _dot_nt(k1_ref[h], q1)
        s2_ref[pl.ds(h, N_KEYS, stride=SUBLANES), :] = _dot_nt(k2_ref[h], q2)

    neg_inf = jnp.float32(-jnp.inf)

    def sub_round(r, carry):
        for s_ref, m_ref, i_ref in ((s1_ref, m1_ref, i1_ref), (s2_ref, m2_ref, i2_ref)):
            vals = [_tile(s_ref, k) for k in range(N_KEYS)]
            best, arg = _argmax_tournament(vals, list(range(N_KEYS)))
            m_ref[r] = best
            i_ref[r] = arg
            for k in range(N_KEYS):
                s_ref[k * SUBLANES:(k + 1) * SUBLANES, :] = jnp.where(arg == k, neg_inf, vals[k])
        return carry

    lax.fori_loop(0, PEER_TOPK, sub_round, 0)

    for j, (a, b) in enumerate(_CAND):
        c_ref[j * SUBLANES:(j + 1) * SUBLANES, :] = m1_ref[a] + m2_ref[b]
        e_ref[j * SUBLANES:(j + 1) * SUBLANES, :] = i1_ref[a] * N_KEYS + i2_ref[b]

    def pair_round(r, carry):
        vals = [_tile(c_ref, j) for j in range(len(_CAND))]
        ids = [_tile(e_ref, j) for j in range(len(_CAND))]
        best, arg = _argmax_tournament(vals, ids)
        sc_ref[r] = best
        idx_ref[r] = arg
        for j in range(len(_CAND)):
            c_ref[j * SUBLANES:(j + 1) * SUBLANES, :] = jnp.where(ids[j] == arg, neg_inf, vals[j])
        return carry

    lax.fori_loop(0, PEER_TOPK, pair_round, 0)

    sc = sc_ref[...]
    ex = jnp.exp(sc - sc[0:1])
    gate_ref[...] = ex / jnp.sum(ex, axis=0, keepdims=True)


def _token_tile_map(seq, tile):
    per_batch = seq // tile
    return lambda i, *_: (i // per_batch, lax.rem(i, per_batch), 0)


def _peer_query(h1, wq, k1, k2, *, seq, tq):
    n_tok = h1.shape[0] * seq
    slot_shape = (PEER_TOPK, PEER_HEADS, tq)
    return pl.pallas_call(
        _query_kernel,
        grid=(n_tok // tq,),
        in_specs=[
            pl.BlockSpec((None, tq, D_MODEL), _token_tile_map(seq, tq)),
            pl.BlockSpec((D_MODEL, 2 * PEER_HALF * PEER_HEADS), lambda i: (0, 0)),
            pl.BlockSpec((PEER_HEADS, N_KEYS, PEER_HALF), lambda i: (0, 0, 0)),
            pl.BlockSpec((PEER_HEADS, N_KEYS, PEER_HALF), lambda i: (0, 0, 0)),
        ],
        out_specs=[
            pl.BlockSpec(slot_shape, lambda i: (0, 0, i)),
            pl.BlockSpec(slot_shape, lambda i: (0, 0, i)),
        ],
        out_shape=[
            jax.ShapeDtypeStruct((PEER_TOPK, PEER_HEADS, n_tok), I32),
            jax.ShapeDtypeStruct((PEER_TOPK, PEER_HEADS, n_tok), F32),
        ],
        scratch_shapes=[
            pltpu.VMEM((N_KEYS * SUBLANES, tq), F32),
            pltpu.VMEM((N_KEYS * SUBLANES, tq), F32),
            pltpu.VMEM(slot_shape, F32),
            pltpu.VMEM(slot_shape, I32),
            pltpu.VMEM(slot_shape, F32),
            pltpu.VMEM(slot_shape, I32),
            pltpu.VMEM((len(_CAND) * SUBLANES, tq), F32),
            pltpu.VMEM((len(_CAND) * SUBLANES, tq), I32),
            pltpu.VMEM(slot_shape, F32),
        ],
        compiler_params=_params(("arbitrary",)),
        name="peer_query",
    )(h1, wq, k1, k2)


IA_PER_STEP = 16
EXPERTS_PER_STEP = IA_PER_STEP * N_KEYS
PEER_STEPS = (N_KEYS * N_KEYS) // EXPERTS_PER_STEP
PEER_PIECES = 2
TABLE_STREAMS = 4


def _table_specs():
    rows = EXPERTS_PER_STEP // TABLE_STREAMS
    return [pl.BlockSpec((rows, D_MODEL), lambda i, j, p=p: (j * TABLE_STREAMS + p, 0)) for p in range(TABLE_STREAMS)]


def _one_hot_rows(idx_row):
    iota = lax.broadcasted_iota(I32, (N_KEYS, N_SLOTS), 0)
    return jnp.where(iota == idx_row, 1.0, 0.0).astype(BF16)


def _block_diag(a, b):
    zero = jnp.zeros_like(a)
    return jnp.concatenate([jnp.concatenate([a, zero], axis=1), jnp.concatenate([zero, b], axis=1)], axis=0)


def _hidden_kernel(x_ref, *refs, tb, hp):
    u_refs = refs[:TABLE_STREAMS]
    ia_ref, ib_ref, gate_ref, act_ref, xb_ref, h0_ref, h1_ref, hs_ref = refs[TABLE_STREAMS:]
    i = pl.program_id(0)
    j = pl.program_id(1)
    tokens = tb // PEER_STEPS // PEER_PIECES
    parts = TABLE_STREAMS // PEER_PIECES
    ia_pairs = IA_PER_STEP // PEER_PIECES // 2

    @pl.when(jnp.logical_and(i == 0, j == 0))
    def _():
        h1_ref[...] = jnp.zeros_like(h1_ref)

    @pl.when(j == 0)
    def _():
        xb_ref[...] = x_ref[...].astype(BF16)

    def step(fill_ref, drain_ref):
        iota = lax.broadcasted_iota(I32, (N_KEYS, N_SLOTS), 0)
        for q in range(PEER_PIECES):
            base = pl.multiple_of((j * PEER_PIECES + q) * tokens, tokens)
            ia_rows = ia_ref[pl.ds(base, tokens), :]
            ib_rows = ib_ref[pl.ds(base, tokens), :]
            lhs, rhs = [], []
            for p in range(tokens // 2):
                tiles = []
                for t in (2 * p, 2 * p + 1):
                    start = pl.multiple_of((base + t) * hp, SUBLANES)
                    tiles.append(pltpu.bitcast(drain_ref[pl.ds(start, N_KEYS // 2), :], BF16))
                lhs.append(jnp.concatenate(tiles, axis=1))
                rhs.append(_block_diag(_one_hot_rows(ib_rows[2 * p:2 * p + 1, :]),
                                       _one_hot_rows(ib_rows[2 * p + 1:2 * p + 2, :])))

            experts = jnp.concatenate([u_refs[q * parts + p][...] for p in range(parts)], axis=0)
            hc = _dot_nt(xb_ref[...], experts)
            for m in range(ia_pairs):
                even = hc[:, (2 * m) * N_KEYS:(2 * m + 1) * N_KEYS]
                odd = hc[:, (2 * m + 1) * N_KEYS:(2 * m + 2) * N_KEYS]
                packed = pltpu.pack_elementwise([even, odd], packed_dtype=BF16)
                pair = (j * PEER_PIECES + q) * ia_pairs + m
                fill_ref[pl.ds(pair, tb, stride=hp), :] = lax.bitcast_convert_type(packed, I32)

            rows = []
            for p in range(tokens // 2):
                both = _dot(lhs[p], rhs[p])
                for k in range(2):
                    mine = both[:, k * N_SLOTS:(k + 1) * N_SLOTS]
                    picked = jnp.where(iota == ia_rows[2 * p + k:2 * p + k + 1, :], mine, 0.0)
                    rows.append(jnp.sum(picked, axis=0, keepdims=True))
            hs_ref[pl.ds(base, tokens), :] = jnp.concatenate(rows, axis=0)

    parity = lax.rem(i, 2)

    @pl.when(parity == 0)
    def _():
        step(h0_ref, h1_ref)

    @pl.when(parity == 1)
    def _():
        step(h1_ref, h0_ref)

    @pl.when(j == PEER_STEPS - 1)
    def _():
        hid = hs_ref[...]
        gelu = 0.5 * hid * (1.0 + lax.erf(hid * (2.0 ** -0.5)))
        act_ref[...] = gelu * gate_ref[...]


def _peer_hidden(h1, u_tab, ia, ib, gate, *, seq, tb):
    n_tiles = h1.shape[0] * seq // tb
    hp = N_KEYS // 2 + SUBLANES
    x_map = _token_tile_map(seq, tb)
    ahead = lambda i, j: x_map(jnp.minimum(i, n_tiles - 1))
    behind = lambda i, j: (jnp.maximum(i - 1, 0), 0)
    return pl.pallas_call(
        functools.partial(_hidden_kernel, tb=tb, hp=hp),
        grid=(n_tiles + 1, PEER_STEPS),
        in_specs=[
            pl.BlockSpec((None, tb, D_MODEL), ahead),
            *_table_specs(),
            pl.BlockSpec((tb, N_SLOTS), behind),
            pl.BlockSpec((tb, N_SLOTS), behind),
            pl.BlockSpec((tb, N_SLOTS), behind),
        ],
        out_specs=pl.BlockSpec((tb, N_SLOTS), behind),
        out_shape=jax.ShapeDtypeStruct((n_tiles * tb, N_SLOTS), F32),
        scratch_shapes=[
            pltpu.VMEM((tb, D_MODEL), BF16),
            pltpu.VMEM((tb * hp, N_KEYS), I32),
            pltpu.VMEM((tb * hp, N_KEYS), I32),
            pltpu.VMEM((tb, N_SLOTS), F32),
        ],
        compiler_params=_params(("arbitrary", "arbitrary")),
        name="peer_hidden",
    )(h1, *([u_tab] * TABLE_STREAMS), ia, ib, gate)


def _output_kernel(act_ref, ia_ref, ib_ref, *refs, tb, hp):
    v_refs = refs[:TABLE_STREAMS]
    h1_ref, g_ref, b_ref, out_ref, a0_ref, a1_ref, acc_ref = refs[TABLE_STREAMS:]
    i = pl.program_id(0)
    j = pl.program_id(1)
    tokens = tb // PEER_STEPS // PEER_PIECES
    parts = TABLE_STREAMS // PEER_PIECES
    ia_per_piece = IA_PER_STEP // PEER_PIECES

    @pl.when(jnp.logical_and(i == 0, j == 0))
    def _():
        a1_ref[...] = jnp.zeros_like(a1_ref)

    @pl.when(j == 0)
    def _():
        acc_ref[...] = jnp.zeros_like(acc_ref)

    def step(fill_ref, drain_ref):
        iota = lax.broadcasted_iota(I32, (N_KEYS, N_SLOTS), 0)
        acc = acc_ref[...]
        for q in range(PEER_PIECES):
            base = pl.multiple_of((j * PEER_PIECES + q) * tokens, tokens)
            ia_rows = ia_ref[pl.ds(base, tokens), :]
            ib_rows = ib_ref[pl.ds(base, tokens), :]
            act_rows = act_ref[pl.ds(base, tokens), :]
            lhs, sels = [], []
            for p in range(tokens // 2):
                spread = [jnp.where(iota == ia_rows[t:t + 1, :], act_rows[t:t + 1, :], 0.0).astype(BF16)
                          for t in (2 * p, 2 * p + 1)]
                lhs.append(jnp.concatenate(spread, axis=1))
                sels += [_one_hot_rows(ib_rows[t:t + 1, :]) for t in (2 * p, 2 * p + 1)]

            tiles = []
            for t in range(ia_per_piece):
                ia = j * IA_PER_STEP + q * ia_per_piece + t
                start = pl.multiple_of(ia * hp, SUBLANES)
                tiles.append(pltpu.bitcast(drain_ref[pl.ds(start, tb // 2), :], BF16))
            table_rows = jnp.concatenate([v_refs[q * parts + p][...] for p in range(parts)], axis=0)
            acc = acc + _dot(jnp.concatenate(tiles, axis=1), table_rows)

            pair_base = (j * PEER_PIECES + q) * (tokens // 2)
            for p in range(tokens // 2):
                both = _dot_nt(lhs[p], _block_diag(sels[2 * p], sels[2 * p + 1]))
                packed = pltpu.pack_elementwise([both[:, :N_KEYS], both[:, N_KEYS:]], packed_dtype=BF16)
                fill_ref[pl.ds(pair_base + p, N_KEYS, stride=hp), :] = lax.bitcast_convert_type(packed, I32)
        acc_ref[...] = acc

    parity = lax.rem(i, 2)

    @pl.when(parity == 0)
    def _():
        step(a0_ref, a1_ref)

    @pl.when(parity == 1)
    def _():
        step(a1_ref, a0_ref)

    @pl.when(j == PEER_STEPS - 1)
    def _():
        out_ref[...] = _layer_norm(ALPHA * h1_ref[...] + acc_ref[...], g_ref[...], b_ref[...])


def _peer_output(act, ia, ib, v_tab, h1, g2, b2, *, seq, tb):
    n_tiles = h1.shape[0] * seq // tb
    hp = tb // 2 + SUBLANES
    x_map = _token_tile_map(seq, tb)
    ahead = lambda i, j: (jnp.minimum(i, n_tiles - 1), 0)
    behind = lambda i, j: (jnp.maximum(i - 1, 0), 0)
    const = lambda i, j: (0, 0)
    return pl.pallas_call(
        functools.partial(_output_kernel, tb=tb, hp=hp),
        grid=(n_tiles + 1, PEER_STEPS),
        in_specs=[
            pl.BlockSpec((tb, N_SLOTS), ahead),
            pl.BlockSpec((tb, N_SLOTS), ahead),
            pl.BlockSpec((tb, N_SLOTS), ahead),
            *_table_specs(),
            pl.BlockSpec((None, tb, D_MODEL), lambda i, j: x_map(jnp.maximum(i - 1, 0))),
            pl.BlockSpec((1, D_MODEL), const),
            pl.BlockSpec((1, D_MODEL), const),
        ],
        out_specs=pl.BlockSpec((tb, D_MODEL), behind),
        out_shape=jax.ShapeDtypeStruct((n_tiles * tb, D_MODEL), F32),
        scratch_shapes=[
            pltpu.VMEM((N_KEYS * hp, N_KEYS), I32),
            pltpu.VMEM((N_KEYS * hp, N_KEYS), I32),
            pltpu.VMEM((tb, D_MODEL), F32),
        ],
        compiler_params=_params(("arbitrary", "arbitrary")),
        name="peer_output",
    )(act, ia, ib, *([v_tab] * TABLE_STREAMS), h1, g2, b2)


def _pick_tile(n, candidates):
    for c in candidates:
        if n % c == 0:
            return c
    raise ValueError(f"no tile in {candidates} divides {n}")


def kernel(x, meta, ln0_g, ln0_b, w_in, pool_w, pool_scale, w_br_pool, conv_w, a_log, dt_bias, onorm_g, w_br_delta, w_out, ln1_g, ln1_b, peer_wq, peer_k1, peer_k2, peer_u, peer_v, ln2_g, ln2_b):
    batch, seq, d = x.shape
    assert d == D_MODEL and seq % CHUNK == 0 and w_in.shape[0] == 1
    t_len = seq + CHUNK
    n_tok = batch * seq
    row = lambda p: p.reshape(1, -1).astype(F32)

    tail = jnp.concatenate([jnp.zeros((PAD, d), x.dtype), meta.astype(x.dtype)], axis=0)

    w = w_in[0]
    c_pool, c_qkv, c_z = POOL_WIDTH, POOL_WIDTH + 3 * DN_WIDTH, POOL_WIDTH + 4 * DN_WIDTH
    c_b, c_a = c_z + DN_HEADS, c_z + 2 * DN_HEADS
    w_main = jnp.concatenate([w[:, c_pool:c_z], w[:, c_a:], w[:, :c_pool]], axis=1).astype(BF16)
    w_ba = jnp.zeros((d, BA_WIDTH), F32)
    w_ba = w_ba.at[:, :DN_HEADS].set(w[:, c_z:c_b]).at[:, LANES:LANES + DN_HEADS].set(w[:, c_b:c_a]).astype(BF16)
    lane_pad = lambda p: jnp.zeros((1, LANES), F32).at[0, :DN_HEADS].set(p.astype(F32))

    proj, ba = _inproj(x, tail, row(ln0_g), row(ln0_b), w_main, w_ba, seq=seq, t_len=t_len)
    qkvn = _conv_qkv(proj, conv_w[0].astype(F32), batch=batch, seq=seq, t_len=t_len)
    o = _delta_rule(qkvn, ba, proj, lane_pad(-jnp.exp(a_log[0].astype(F32))), lane_pad(dt_bias[0]),
                    row(onorm_g[0]), batch=batch, t_len=t_len)
    h1 = _merge(o, proj, x, tail, w_br_delta[0].astype(BF16), w_out[0].astype(BF16), pool_w[0].astype(BF16),
                row(pool_scale[0]), w_br_pool[0].astype(BF16), row(ln0_g), row(ln0_b), row(ln1_g[0]), row(ln1_b[0]),
                batch=batch, seq=seq, t_len=t_len)
    h1 = h1.reshape(batch, t_len, d)

    idx, gate = _peer_query(h1, peer_wq[0].astype(BF16), peer_k1[0].astype(BF16), peer_k2[0].astype(BF16),
                            seq=seq, tq=LANES)
    idx = idx.reshape(N_SLOTS, n_tok).T
    gate = gate.reshape(N_SLOTS, n_tok).T
    ia = lax.shift_right_logical(idx, 7)
    ib = lax.bitwise_and(idx, N_KEYS - 1)

    act = _peer_hidden(h1, peer_u[0].astype(BF16), ia, ib, gate, seq=seq, tb=_pick_tile(seq, (512,)))
    out = _peer_output(act, ia, ib, peer_v[0].astype(BF16), h1, row(ln2_g[0]), row(ln2_b[0]),
                       seq=seq, tb=_pick_tile(seq, (512,)))
    return out.reshape(batch, seq, d)
```

```python
import functools

import jax
import jax.numpy as jnp
from jax import lax
from jax.experimental import pallas as pl
from jax.experimental.pallas import tpu as pltpu

F32 = jnp.float32
BF16 = jnp.bfloat16
I32 = jnp.int32

D_MODEL = 1024
N_META = 16
CHUNK = 64
PAD = CHUNK - N_META
POOL_GROUPS = 4
POOL_GROUP_DIM = 128
POOL_WIDTH = POOL_GROUPS * POOL_GROUP_DIM
POOL_WINDOWS = (2, 4, 8, 16)
POOL_HALO = 16
DN_HEADS = 8
DN_HEAD_DIM = 128
DN_WIDTH = DN_HEADS * DN_HEAD_DIM
CONV_WIDTH = 4
PEER_HEADS = 8
PEER_HALF = 128
N_KEYS = 128
PEER_TOPK = 16
N_SLOTS = PEER_HEADS * PEER_TOPK
LN_EPS = 1e-5
RMS_EPS = 1e-6
L2_EPS = 1e-6
ALPHA = 2.0 ** 0.25

COL_QKV = 0
COL_Z = 3 * DN_WIDTH
COL_GP = COL_Z + DN_WIDTH
COL_GD = COL_GP + D_MODEL
COL_POOL = COL_GD + D_MODEL
N_PROJ = COL_POOL + POOL_WIDTH
BA_WIDTH = 256

SUBLANES = 8
LANES = 128
V7X_VMEM_BYTES = 64 * 1024 * 1024
VMEM_LIMIT = V7X_VMEM_BYTES - 4 * 1024 * 1024

_CAND = [(a, b) for a in range(PEER_TOPK) for b in range(PEER_TOPK) if (a + 1) * (b + 1) <= PEER_TOPK]


def _params(sem):
    return pltpu.CompilerParams(dimension_semantics=sem, vmem_limit_bytes=VMEM_LIMIT)


def _layer_norm(x, g, b):
    mu = jnp.mean(x, axis=-1, keepdims=True)
    xc = x - mu
    var = jnp.mean(xc * xc, axis=-1, keepdims=True)
    return xc * lax.rsqrt(var + LN_EPS) * g + b


def _zero_pad_rows(y, row, seq):
    return jnp.where(row >= seq, jnp.where(row < seq + PAD, 0.0, y), y)


def _sigmoid(x):
    return 1.0 / (1.0 + jnp.exp(-x))


def _dot(a, b):
    return jnp.dot(a.astype(BF16), b.astype(BF16), preferred_element_type=F32)


def _dot_nt(a, b):
    return lax.dot_general(a.astype(BF16), b.astype(BF16), (((1,), (1,)), ((), ())), preferred_element_type=F32)


def _assemble_rows(x_ref, tail_ref, rows_ref, tile_in_batch, tiles_per_batch):
    body = rows_ref.shape[0] - CHUNK

    @pl.when(tile_in_batch < tiles_per_batch - 1)
    def _():
        rows_ref[...] = x_ref[...]

    @pl.when(tile_in_batch == tiles_per_batch - 1)
    def _():
        rows_ref[:body] = x_ref[:body]
        rows_ref[body:] = tail_ref[...]


def _inproj_kernel(x_ref, tail_ref, g_ref, b_ref, w_ref, wba_ref, proj_ref, ba_ref, rows_ref, xn_ref, *,
                   tiles_per_batch, seq):
    i = pl.program_id(0)
    j = pl.program_id(1)
    tm = x_ref.shape[0]

    @pl.when(j == 0)
    def _():
        _assemble_rows(x_ref, tail_ref, rows_ref, lax.rem(i, tiles_per_batch), tiles_per_batch)
        y = _layer_norm(rows_ref[...], g_ref[...], b_ref[...])
        row = lax.broadcasted_iota(I32, (tm, 1), 0) + lax.rem(i, tiles_per_batch) * tm
        xn = _zero_pad_rows(y, row, seq).astype(BF16)
        xn_ref[...] = xn
        ba_ref[...] = _dot(xn, wba_ref[...])

    proj_ref[...] = _dot(xn_ref[...], w_ref[...]).astype(proj_ref.dtype)


def _inproj(x, tail, ln_g, ln_b, w_main, w_ba, *, seq, t_len):
    rows = x.shape[0] * t_len
    tiles_per_batch = 4
    tm = t_len // tiles_per_batch
    tn = N_PROJ // 4
    kern = functools.partial(_inproj_kernel, tiles_per_batch=tiles_per_batch, seq=seq)
    return pl.pallas_call(
        kern,
        grid=(rows // tm, N_PROJ // tn),
        in_specs=[
            pl.BlockSpec((None, tm, D_MODEL), lambda i, j: (i // tiles_per_batch, lax.rem(i, tiles_per_batch), 0)),
            pl.BlockSpec((CHUNK, D_MODEL), lambda i, j: (0, 0)),
            pl.BlockSpec((1, D_MODEL), lambda i, j: (0, 0)),
            pl.BlockSpec((1, D_MODEL), lambda i, j: (0, 0)),
            pl.BlockSpec((D_MODEL, tn), lambda i, j: (0, j)),
            pl.BlockSpec((D_MODEL, BA_WIDTH), lambda i, j: (0, 0)),
        ],
        out_specs=[
            pl.BlockSpec((tm, tn), lambda i, j: (i, j)),
            pl.BlockSpec((tm, BA_WIDTH), lambda i, j: (i, 0)),
        ],
        out_shape=[
            jax.ShapeDtypeStruct((rows, N_PROJ), BF16),
            jax.ShapeDtypeStruct((rows, BA_WIDTH), F32),
        ],
        scratch_shapes=[pltpu.VMEM((tm, D_MODEL), F32), pltpu.VMEM((tm, D_MODEL), BF16)],
        compiler_params=_params(("arbitrary", "arbitrary")),
        name="inproj",
    )(x, tail, ln_g, ln_b, w_main, w_ba)


def _pool_branch_tile(xp_ref, halo_ref, pw_ref, ps_ref, wbp_ref, first_row, seq):
    tm = xp_ref.shape[0]
    v = jnp.concatenate([halo_ref[...].astype(F32), xp_ref[...].astype(F32)], axis=0)
    row = lax.broadcasted_iota(I32, (tm, POOL_GROUP_DIM), 0) + first_row
    meta_pos = row - (seq + PAD)
    outs = []
    for gi, w in enumerate(POOL_WINDOWS):
        vg = v[:, gi * POOL_GROUP_DIM:(gi + 1) * POOL_GROUP_DIM]
        s = vg
        shift = 1
        while shift < w:
            s = s + pltpu.roll(s, shift, axis=0)
            shift *= 2
        s = s[POOL_HALO:]
        xg = vg[POOL_HALO:]
        count = jnp.where(meta_pos >= 0, jnp.minimum(meta_pos + 1, w), w).astype(F32)
        pooled = s / count - xg
        mixed = _dot(pooled.astype(BF16), pw_ref[gi])
        outs.append(mixed)
    y_pool = jnp.concatenate(outs, axis=1) * ps_ref[...]
    return _dot(y_pool.astype(BF16), wbp_ref[...])


def _conv_kernel(x_ref, w_ref, o_ref, *, seq):
    s = pl.program_id(1)
    x = x_ref[...].astype(F32)
    t_len = x.shape[0]
    w = w_ref[...]
    y = x * w[CONV_WIDTH - 1:CONV_WIDTH, :]
    for lag in range(1, CONV_WIDTH):
        y = y + pltpu.roll(x, lag, axis=0) * w[CONV_WIDTH - 1 - lag:CONV_WIDTH - lag, :]
    y = y * _sigmoid(y)
    ss = jnp.sum(y * y, axis=-1, keepdims=True)
    q_scale = jnp.where(s < DN_HEADS, DN_HEAD_DIM ** -0.5, 1.0).astype(F32)
    fac = jnp.where(s < 2 * DN_HEADS, lax.rsqrt(ss + L2_EPS) * q_scale, 1.0)
    row = lax.broadcasted_iota(I32, (t_len, 1), 0)
    o_ref[...] = _zero_pad_rows(y * fac, row, seq).astype(o_ref.dtype)


def _conv_qkv(proj, conv_w, *, batch, seq, t_len):
    rows = proj.shape[0]
    n_slabs = 3 * DN_HEADS
    return pl.pallas_call(
        functools.partial(_conv_kernel, seq=seq),
        grid=(batch, n_slabs),
        in_specs=[
            pl.BlockSpec((t_len, DN_HEAD_DIM), lambda b, s: (b, COL_QKV // DN_HEAD_DIM + s)),
            pl.BlockSpec((CONV_WIDTH, DN_HEAD_DIM), lambda b, s: (0, s)),
        ],
        out_specs=pl.BlockSpec((t_len, DN_HEAD_DIM), lambda b, s: (b, s)),
        out_shape=jax.ShapeDtypeStruct((rows, 3 * DN_WIDTH), BF16),
        compiler_params=_params(("arbitrary", "arbitrary")),
        name="conv_qkv",
    )(proj, conv_w)


def _inv_unit_lower(a, masks, eye):
    m8, m16, m32, m64 = masks
    n = [x * m8 for x in a]
    n2 = [_dot(x, x) for x in n]
    n4 = [_dot(x, x) for x in n2]
    t = [eye - x for x in n]
    t = [x + _dot(x, y) for x, y in zip(t, n2)]
    t = [x + _dot(x, y) for x, y in zip(t, n4)]
    for m in (m16, m32, m64):
        at = [_dot(x * m, y) for x, y in zip(a, t)]
        t = [x - _dot(x, y) for x, y in zip(t, at)]
    return t


DELTA_STREAMS = 4


def _delta_kernel(q_ref, k_ref, v_ref, ba_ref, z_ref, nega_ref, dtb_ref, og_ref, o_ref, state_ref):
    n = DELTA_STREAMS
    c = pl.program_id(1)

    @pl.when(c == 0)
    def _():
        state_ref[...] = jnp.zeros_like(state_ref)

    r = lax.broadcasted_iota(I32, (CHUNK, CHUNK), 0)
    cc = lax.broadcasted_iota(I32, (CHUNK, CHUNK), 1)
    causal = r >= cc
    strict = r > cc
    eye = jnp.where(r == cc, 1.0, 0.0).astype(F32)
    tril = jnp.where(causal, 1.0, 0.0).astype(F32)

    def blk(x, s):
        return lax.shift_right_logical(x, s)

    lower = jnp.where(strict, 1.0, 0.0).astype(F32)
    m8 = jnp.where(blk(r, 3) == blk(cc, 3), lower, 0.0)
    m16 = jnp.where((blk(r, 4) == blk(cc, 4)) & (blk(r, 3) != blk(cc, 3)), lower, 0.0)
    m32 = jnp.where((blk(r, 5) == blk(cc, 5)) & (blk(r, 4) != blk(cc, 4)), lower, 0.0)
    m64 = jnp.where(blk(r, 5) != blk(cc, 5), lower, 0.0)
    masks = (m8, m16, m32, m64)

    beta_all, gcum, gcum_t = [], [], []
    for s in range(n):
        ba = ba_ref[s]
        beta_all.append(_sigmoid(ba[:, :LANES]))
        x = ba[:, LANES:] + dtb_ref[...]
        softplus = jnp.maximum(x, 0.0) + jnp.log(1.0 + jnp.exp(-jnp.abs(x)))
        g_all = nega_ref[...] * softplus
        gcum.append(lax.dot_general(tril, g_all, (((1,), (0,)), ((), ())), precision=lax.Precision.HIGHEST,
                                    preferred_element_type=F32))
        gcum_t.append(gcum[s].T)

    chains = [(s, h) for s in range(n) for h in range(DN_HEADS)]
    ids = range(len(chains))
    sl = [slice(h * DN_HEAD_DIM, (h + 1) * DN_HEAD_DIM) for _, h in chains]
    q = [q_ref[s, :, sl[i]].astype(F32) for i, (s, h) in enumerate(chains)]
    k = [k_ref[s, :, sl[i]].astype(F32) for i, (s, h) in enumerate(chains)]
    v = [v_ref[s, :, sl[i]].astype(F32) for i, (s, h) in enumerate(chains)]
    beta = [beta_all[s][:, h:h + 1] for s, h in chains]
    gc = [gcum[s][:, h:h + 1] for s, h in chains]
    gr = [gcum_t[s][h:h + 1, :] for s, h in chains]
    g_last = [gcum[s][CHUNK - 1:CHUNK, h:h + 1] for s, h in chains]
    decay = [jnp.where(causal, jnp.exp(jnp.where(causal, gc[i] - gr[i], 0.0)), 0.0) for i in ids]
    eg = [jnp.exp(gc[i]) for i in ids]
    kb = [k[i] * beta[i] for i in ids]
    kq = [_dot_nt(jnp.concatenate([kb[i], q[i]], axis=0), k[i]) for i in ids]
    a_mat = [jnp.where(strict, kq[i][:CHUNK] * decay[i], 0.0) for i in ids]
    t_inv = _inv_unit_lower(a_mat, masks, eye)
    rhs = [jnp.concatenate([v[i] * beta[i], kb[i] * eg[i]], axis=1) for i in ids]
    uw = [_dot(t_inv[i], rhs[i]) for i in ids]
    s_prev = [state_ref[i] for i in ids]
    wq = [jnp.concatenate([uw[i][:, DN_HEAD_DIM:], q[i] * eg[i]], axis=0) for i in ids]
    ws = [_dot(wq[i], s_prev[i]) for i in ids]
    v_new = [uw[i][:, :DN_HEAD_DIM] - ws[i][:CHUNK] for i in ids]
    attn = [kq[i][CHUNK:] * decay[i] for i in ids]
    intra = [_dot(attn[i], v_new[i]) for i in ids]
    k_dec_t = [(k[i] * jnp.exp(g_last[i] - gc[i])).T for i in ids]
    kv = [_dot(k_dec_t[i], v_new[i]) for i in ids]
    for i, (s, h) in enumerate(chains):
        state_ref[i] = s_prev[i] * jnp.exp(g_last[i]) + kv[i]
        out = ws[i][CHUNK:] + intra[i]
        out = out * lax.rsqrt(jnp.mean(out * out, axis=-1, keepdims=True) + RMS_EPS) * og_ref[...]
        zh = z_ref[s, :, sl[i]].astype(F32)
        o_ref[s, :, sl[i]] = (out * (zh * _sigmoid(zh))).astype(o_ref.dtype)


def _delta_rule(qkvn, ba, proj, neg_a, dt_bias, onorm_g, *, batch, t_len):
    rows = qkvn.shape[0]
    n_chunks = t_len // CHUNK

    n = DELTA_STREAMS
    assert batch % n == 0
    groups = batch // n
    by_stream = lambda a: a.reshape(groups, n, t_len, a.shape[-1])

    def chunk_block(col):
        return lambda g, c: (g, 0, lax.rem(c + n_chunks - 1, n_chunks), col)

    const = lambda g, c: (0, 0)
    out = pl.pallas_call(
        _delta_kernel,
        grid=(groups, n_chunks),
        in_specs=[
            pl.BlockSpec((None, n, CHUNK, DN_WIDTH), chunk_block(0)),
            pl.BlockSpec((None, n, CHUNK, DN_WIDTH), chunk_block(1)),
            pl.BlockSpec((None, n, CHUNK, DN_WIDTH), chunk_block(2)),
            pl.BlockSpec((None, n, CHUNK, BA_WIDTH), chunk_block(0)),
            pl.BlockSpec((None, n, CHUNK, DN_WIDTH), chunk_block(COL_Z // DN_WIDTH)),
            pl.BlockSpec((1, LANES), const),
            pl.BlockSpec((1, LANES), const),
            pl.BlockSpec((1, DN_HEAD_DIM), const),
        ],
        out_specs=pl.BlockSpec((None, n, CHUNK, DN_WIDTH), chunk_block(0)),
        out_shape=jax.ShapeDtypeStruct((groups, n, t_len, DN_WIDTH), BF16),
        scratch_shapes=[pltpu.VMEM((n * DN_HEADS, DN_HEAD_DIM, DN_HEAD_DIM), F32)],
        compiler_params=_params(("arbitrary", "arbitrary")),
        name="delta_rule",
    )(by_stream(qkvn), by_stream(qkvn), by_stream(qkvn), by_stream(ba), by_stream(proj), neg_a, dt_bias, onorm_g)
    return out.reshape(rows, DN_WIDTH)


def _merge_kernel(o_ref, xp_ref, halo_ref, gp_ref, gd_ref, x_ref, tail_ref, wd_ref, wo_ref, pw_ref, ps_ref, wbp_ref,
                  g0_ref, b0_ref, g1_ref, b1_ref, h1_ref, rows_ref, *, tiles_per_batch, seq):
    tile_in_batch = lax.rem(pl.program_id(0), tiles_per_batch)
    first_row = tile_in_batch * o_ref.shape[0]
    y_a = _pool_branch_tile(xp_ref, halo_ref, pw_ref, ps_ref, wbp_ref, first_row, seq)
    y_b = _dot(o_ref[...], wd_ref[...])
    merged = _sigmoid(gp_ref[...].astype(F32)) * y_a + _sigmoid(gd_ref[...].astype(F32)) * y_b
    mix = _dot(merged.astype(BF16), wo_ref[...])
    _assemble_rows(x_ref, tail_ref, rows_ref, tile_in_batch, tiles_per_batch)
    h = _layer_norm(rows_ref[...], g0_ref[...], b0_ref[...])
    h1_ref[...] = _layer_norm(ALPHA * h + mix, g1_ref[...], b1_ref[...])


def _merge(o, proj, x, tail, w_br_delta, w_out, pool_w, pool_scale, w_br_pool, g0, b0, g1, b1, *, batch, seq, t_len):
    bf16_rows = 2 * SUBLANES
    tiles_per_batch = next(n for n in (10, 8, 4, 2, 1) if t_len % (n * bf16_rows) == 0)
    tm = t_len // tiles_per_batch
    halo_blocks_per_batch = t_len // POOL_HALO
    halo_blocks_per_tile = tm // POOL_HALO
    pool_col = COL_POOL // POOL_WIDTH
    row = lambda i: (i, 0)
    const = lambda i: (0, 0)

    def halo_map(i):
        b, t = i // tiles_per_batch, lax.rem(i, tiles_per_batch)
        prev = lax.rem(t * halo_blocks_per_tile + halo_blocks_per_batch - 1, halo_blocks_per_batch)
        return (b * halo_blocks_per_batch + prev, pool_col)

    return pl.pallas_call(
        functools.partial(_merge_kernel, tiles_per_batch=tiles_per_batch, seq=seq),
        grid=(batch * tiles_per_batch,),
        in_specs=[
            pl.BlockSpec((tm, DN_WIDTH), row),
            pl.BlockSpec((tm, POOL_WIDTH), lambda i: (i, pool_col)),
            pl.BlockSpec((POOL_HALO, POOL_WIDTH), halo_map),
            pl.BlockSpec((tm, D_MODEL), lambda i: (i, COL_GP // D_MODEL)),
            pl.BlockSpec((tm, D_MODEL), lambda i: (i, COL_GD // D_MODEL)),
            pl.BlockSpec((None, tm, D_MODEL), lambda i: (i // tiles_per_batch, lax.rem(i, tiles_per_batch), 0)),
            pl.BlockSpec((CHUNK, D_MODEL), const),
            pl.BlockSpec((DN_WIDTH, D_MODEL), const),
            pl.BlockSpec((D_MODEL, D_MODEL), const),
            pl.BlockSpec((POOL_GROUPS, POOL_GROUP_DIM, POOL_GROUP_DIM), lambda i: (0, 0, 0)),
            pl.BlockSpec((1, POOL_WIDTH), const),
            pl.BlockSpec((POOL_WIDTH, D_MODEL), const),
            pl.BlockSpec((1, D_MODEL), const),
            pl.BlockSpec((1, D_MODEL), const),
            pl.BlockSpec((1, D_MODEL), const),
            pl.BlockSpec((1, D_MODEL), const),
        ],
        out_specs=pl.BlockSpec((tm, D_MODEL), row),
        out_shape=jax.ShapeDtypeStruct((batch * t_len, D_MODEL), F32),
        scratch_shapes=[pltpu.VMEM((tm, D_MODEL), F32)],
        compiler_params=_params(("arbitrary",)),
        name="merge",
    )(o, proj, proj, proj, proj, x, tail, w_br_delta, w_out, pool_w, pool_scale, w_br_pool, g0, b0, g1, b1)


def _tile(ref, k):
    return ref[k * SUBLANES:(k + 1) * SUBLANES, :]


def _argmax_tournament(vals, ids):
    nodes = list(zip(vals, ids))
    while len(nodes) > 1:
        nxt = []
        for p in range(0, len(nodes) - 1, 2):
            (va, ia), (vb, ib) = nodes[p], nodes[p + 1]
            first = va >= vb
            nxt.append((jnp.maximum(va, vb), jnp.where(first, ia, ib)))
        if len(nodes) % 2:
            nxt.append(nodes[-1])
        nodes = nxt
    return nodes[0]


def _query_kernel(x_ref, wq_ref, k1_ref, k2_ref, ia_ref, ib_ref, gate_ref, s1_ref, s2_ref, m1_ref, i1_ref, m2_ref,
                  i2_ref, c_ref, e_ref, sc_ref, ex_ref):
    tq = x_ref.shape[0]
    xb = x_ref[...].astype(BF16)
    q = _dot(xb, wq_ref[...])
    for h in range(PEER_HEADS):
        q1 = q[:, (2 * h) * PEER_HALF:(2 * h + 1) * PEER_HALF].astype(BF16)
        q2 = q[:, (2 * h + 1) * PEER_HALF:(2 * h + 2) * PEER_HALF].astype(BF16)
        s1_ref[pl.ds(h, N_KEYS, stride=SUBLANES), :] = _dot_nt(k1_ref[h], q1)
        s2_ref[pl.ds(h, N_KEYS, stride=SUBLANES), :] = _dot_nt(k2_ref[h], q2)

    neg_inf = jnp.float32(-jnp.inf)

    def sub_round(r, carry):
        for s_ref, m_ref, i_ref in ((s1_ref, m1_ref, i1_ref), (s2_ref, m2_ref, i2_ref)):
            vals = [_tile(s_ref, k) for k in range(N_KEYS)]
            best, arg = _argmax_tournament(vals, list(range(N_KEYS)))
            m_ref[r] = best
            i_ref[r] = arg
            for k in range(N_KEYS):
                s_ref[k * SUBLANES:(k + 1) * SUBLANES, :] = jnp.where(arg == k, neg_inf, vals[k])
        return carry

    lax.fori_loop(0, PEER_TOPK, sub_round, 0)

    for j, (a, b) in enumerate(_CAND):
        c_ref[j * SUBLANES:(j + 1) * SUBLANES, :] = m1_ref[a] + m2_ref[b]
        e_ref[j * SUBLANES:(j + 1) * SUBLANES, :] = i1_ref[a] * N_KEYS + i2_ref[b]

    def pair_round(r, carry):
        vals = [_tile(c_ref, j) for j in range(len(_CAND))]
        ids = [_tile(e_ref, j) for j in range(len(_CAND))]
        best, arg = _argmax_tournament(vals, ids)
        sc_ref[r] = best
        ex_ref[r] = arg
        for j in range(len(_CAND)):
            c_ref[j * SUBLANES:(j + 1) * SUBLANES, :] = jnp.where(ids[j] == arg, neg_inf, vals[j])
        return carry

    lax.fori_loop(0, PEER_TOPK, pair_round, 0)

    sc = sc_ref[...]
    ex = jnp.exp(sc - sc[0:1])
    gate = ex / jnp.sum(ex, axis=0, keepdims=True)
    gate_ref[...] = gate.reshape(N_SLOTS, tq).T
    expert = ex_ref[...].reshape(N_SLOTS, tq).T
    ia_ref[...] = lax.shift_right_logical(expert, N_KEYS.bit_length() - 1)
    ib_ref[...] = lax.bitwise_and(expert, N_KEYS - 1)


def _token_tile_map(seq, tile):
    per_batch = seq // tile
    return lambda i, *_: (i // per_batch, lax.rem(i, per_batch), 0)


def _peer_query(h1, wq, k1, k2, *, seq, tq):
    n_tok = h1.shape[0] * seq
    slot_shape = (PEER_TOPK, PEER_HEADS, tq)
    return pl.pallas_call(
        _query_kernel,
        grid=(n_tok // tq,),
        in_specs=[
            pl.BlockSpec((None, tq, D_MODEL), _token_tile_map(seq, tq)),
            pl.BlockSpec((D_MODEL, 2 * PEER_HALF * PEER_HEADS), lambda i: (0, 0)),
            pl.BlockSpec((PEER_HEADS, N_KEYS, PEER_HALF), lambda i: (0, 0, 0)),
            pl.BlockSpec((PEER_HEADS, N_KEYS, PEER_HALF), lambda i: (0, 0, 0)),
        ],
        out_specs=[pl.BlockSpec((tq, N_SLOTS), lambda i: (i, 0)) for _ in range(3)],
        out_shape=[
            jax.ShapeDtypeStruct((n_tok, N_SLOTS), I32),
            jax.ShapeDtypeStruct((n_tok, N_SLOTS), I32),
            jax.ShapeDtypeStruct((n_tok, N_SLOTS), F32),
        ],
        scratch_shapes=[
            pltpu.VMEM((N_KEYS * SUBLANES, tq), F32),
            pltpu.VMEM((N_KEYS * SUBLANES, tq), F32),
            pltpu.VMEM(slot_shape, F32),
            pltpu.VMEM(slot_shape, I32),
            pltpu.VMEM(slot_shape, F32),
            pltpu.VMEM(slot_shape, I32),
            pltpu.VMEM((len(_CAND) * SUBLANES, tq), F32),
            pltpu.VMEM((len(_CAND) * SUBLANES, tq), I32),
            pltpu.VMEM(slot_shape, F32),
            pltpu.VMEM(slot_shape, I32),
        ],
        compiler_params=_params(("arbitrary",)),
        name="peer_query",
    )(h1, wq, k1, k2)


IA_PER_STEP = 16
EXPERTS_PER_STEP = IA_PER_STEP * N_KEYS
PEER_STEPS = (N_KEYS * N_KEYS) // EXPERTS_PER_STEP
PEER_PIECES = 2
TABLE_STREAMS = 4


def _table_specs():
    rows = EXPERTS_PER_STEP // TABLE_STREAMS
    return [pl.BlockSpec((rows, D_MODEL), lambda i, j, p=p: (j * TABLE_STREAMS + p, 0)) for p in range(TABLE_STREAMS)]


def _one_hot_rows(idx_row):
    iota = lax.broadcasted_iota(I32, (N_KEYS, N_SLOTS), 0)
    return jnp.where(iota == idx_row, 1.0, 0.0).astype(BF16)


def _block_diag(a, b):
    zero = jnp.zeros_like(a)
    return jnp.concatenate([jnp.concatenate([a, zero], axis=1), jnp.concatenate([zero, b], axis=1)], axis=0)


def _hidden_kernel(x_ref, *refs, tb, hp):
    u_refs = refs[:TABLE_STREAMS]
    ia_ref, ib_ref, gate_ref, act_ref, xb_ref, h0_ref, h1_ref, hs_ref = refs[TABLE_STREAMS:]
    i = pl.program_id(0)
    j = pl.program_id(1)
    tokens = tb // PEER_STEPS // PEER_PIECES
    parts = TABLE_STREAMS // PEER_PIECES
    ia_pairs = IA_PER_STEP // PEER_PIECES // 2

    @pl.when(jnp.logical_and(i == 0, j == 0))
    def _():
        h1_ref[...] = jnp.zeros_like(h1_ref)

    @pl.when(j == 0)
    def _():
        xb_ref[...] = x_ref[...].astype(BF16)

    def step(fill_ref, drain_ref):
        iota = lax.broadcasted_iota(I32, (N_KEYS, N_SLOTS), 0)
        for q in range(PEER_PIECES):
            base = pl.multiple_of((j * PEER_PIECES + q) * tokens, tokens)
            ia_rows = ia_ref[pl.ds(base, tokens), :]
            ib_rows = ib_ref[pl.ds(base, tokens), :]
            lhs, rhs = [], []
            for p in range(tokens // 2):
                tiles = []
                for t in (2 * p, 2 * p + 1):
                    start = pl.multiple_of((base + t) * hp, SUBLANES)
                    tiles.append(pltpu.bitcast(drain_ref[pl.ds(start, N_KEYS // 2), :], BF16))
                lhs.append(jnp.concatenate(tiles, axis=1))
                rhs.append(_block_diag(_one_hot_rows(ib_rows[2 * p:2 * p + 1, :]),
                                       _one_hot_rows(ib_rows[2 * p + 1:2 * p + 2, :])))

            experts = jnp.concatenate([u_refs[q * parts + p][...] for p in range(parts)], axis=0)
            hc = _dot_nt(xb_ref[...], experts)
            for m in range(ia_pairs):
                even = hc[:, (2 * m) * N_KEYS:(2 * m + 1) * N_KEYS]
                odd = hc[:, (2 * m + 1) * N_KEYS:(2 * m + 2) * N_KEYS]
                packed = pltpu.pack_elementwise([even, odd], packed_dtype=BF16)
                pair = (j * PEER_PIECES + q) * ia_pairs + m
                fill_ref[pl.ds(pair, tb, stride=hp), :] = lax.bitcast_convert_type(packed, I32)

            rows = []
            for p in range(tokens // 2):
                both = _dot(lhs[p], rhs[p])
                for k in range(2):
                    mine = both[:, k * N_SLOTS:(k + 1) * N_SLOTS]
                    picked = jnp.where(iota == ia_rows[2 * p + k:2 * p + k + 1, :], mine, 0.0)
                    rows.append(jnp.sum(picked, axis=0, keepdims=True))
            hs_ref[pl.ds(base, tokens), :] = jnp.concatenate(rows, axis=0)

    parity = lax.rem(i, 2)

    @pl.when(parity == 0)
    def _():
        step(h0_ref, h1_ref)

    @pl.when(parity == 1)
    def _():
        step(h1_ref, h0_ref)

    @pl.when(j == PEER_STEPS - 1)
    def _():
        hid = hs_ref[...]
        gelu = 0.5 * hid * (1.0 + lax.erf(hid * (2.0 ** -0.5)))
        act_ref[...] = gelu * gate_ref[...]


def _peer_hidden(h1, u_tab, ia, ib, gate, *, seq, tb):
    n_tiles = h1.shape[0] * seq // tb
    hp = N_KEYS // 2 + SUBLANES
    x_map = _token_tile_map(seq, tb)
    ahead = lambda i, j: x_map(jnp.minimum(i, n_tiles - 1))
    behind = lambda i, j: (jnp.maximum(i - 1, 0), 0)
    return pl.pallas_call(
        functools.partial(_hidden_kernel, tb=tb, hp=hp),
        grid=(n_tiles + 1, PEER_STEPS),
        in_specs=[
            pl.BlockSpec((None, tb, D_MODEL), ahead),
            *_table_specs(),
            pl.BlockSpec((tb, N_SLOTS), behind),
            pl.BlockSpec((tb, N_SLOTS), behind),
            pl.BlockSpec((tb, N_SLOTS), behind),
        ],
        out_specs=pl.BlockSpec((tb, N_SLOTS), behind),
        out_shape=jax.ShapeDtypeStruct((n_tiles * tb, N_SLOTS), F32),
        scratch_shapes=[
            pltpu.VMEM((tb, D_MODEL), BF16),
            pltpu.VMEM((tb * hp, N_KEYS), I32),
            pltpu.VMEM((tb * hp, N_KEYS), I32),
            pltpu.VMEM((tb, N_SLOTS), F32),
        ],
        compiler_params=_params(("arbitrary", "arbitrary")),
        name="peer_hidden",
    )(h1, *([u_tab] * TABLE_STREAMS), ia, ib, gate)


def _output_kernel(act_ref, ia_ref, ib_ref, *refs, tb, hp):
    v_refs = refs[:TABLE_STREAMS]
    h1_ref, g_ref, b_ref, out_ref, a0_ref, a1_ref, acc_ref = refs[TABLE_STREAMS:]
    i = pl.program_id(0)
    j = pl.program_id(1)
    tokens = tb // PEER_STEPS // PEER_PIECES
    parts = TABLE_STREAMS // PEER_PIECES
    ia_per_piece = IA_PER_STEP // PEER_PIECES

    @pl.when(jnp.logical_and(i == 0, j == 0))
    def _():
        a1_ref[...] = jnp.zeros_like(a1_ref)

    @pl.when(j == 0)
    def _():
        acc_ref[...] = jnp.zeros_like(acc_ref)

    def step(fill_ref, drain_ref):
        iota = lax.broadcasted_iota(I32, (N_KEYS, N_SLOTS), 0)
        acc = acc_ref[...]
        for q in range(PEER_PIECES):
            base = pl.multiple_of((j * PEER_PIECES + q) * tokens, tokens)
            ia_rows = ia_ref[pl.ds(base, tokens), :]
            ib_rows = ib_ref[pl.ds(base, tokens), :]
            act_rows = act_ref[pl.ds(base, tokens), :]
            lhs, sels = [], []
            for p in range(tokens // 2):
                spread = [jnp.where(iota == ia_rows[t:t + 1, :], act_rows[t:t + 1, :], 0.0).astype(BF16)
                          for t in (2 * p, 2 * p + 1)]
                lhs.append(jnp.concatenate(spread, axis=1))
                sels += [_one_hot_rows(ib_rows[t:t + 1, :]) for t in (2 * p, 2 * p + 1)]

            tiles = []
            for t in range(ia_per_piece):
                ia = j * IA_PER_STEP + q * ia_per_piece + t
                start = pl.multiple_of(ia * hp, SUBLANES)
                tiles.append(pltpu.bitcast(drain_ref[pl.ds(start, tb // 2), :], BF16))
            table_rows = jnp.concatenate([v_refs[q * parts + p][...] for p in range(parts)], axis=0)
            acc = acc + _dot(jnp.concatenate(tiles, axis=1), table_rows)

            pair_base = (j * PEER_PIECES + q) * (tokens // 2)
            for p in range(tokens // 2):
                both = _dot_nt(lhs[p], _block_diag(sels[2 * p], sels[2 * p + 1]))
                packed = pltpu.pack_elementwise([both[:, :N_KEYS], both[:, N_KEYS:]], packed_dtype=BF16)
                fill_ref[pl.ds(pair_base + p, N_KEYS, stride=hp), :] = lax.bitcast_convert_type(packed, I32)
        acc_ref[...] = acc

    parity = lax.rem(i, 2)

    @pl.when(parity == 0)
    def _():
        step(a0_ref, a1_ref)

    @pl.when(parity == 1)
    def _():
        step(a1_ref, a0_ref)

    @pl.when(j == PEER_STEPS - 1)
    def _():
        out_ref[...] = _layer_norm(ALPHA * h1_ref[...] + acc_ref[...], g_ref[...], b_ref[...])


def _peer_output(act, ia, ib, v_tab, h1, g2, b2, *, seq, tb):
    n_tiles = h1.shape[0] * seq // tb
    hp = tb // 2 + SUBLANES
    x_map = _token_tile_map(seq, tb)
    ahead = lambda i, j: (jnp.minimum(i, n_tiles - 1), 0)
    behind = lambda i, j: (jnp.maximum(i - 1, 0), 0)
    const = lambda i, j: (0, 0)
    return pl.pallas_call(
        functools.partial(_output_kernel, tb=tb, hp=hp),
        grid=(n_tiles + 1, PEER_STEPS),
        in_specs=[
            pl.BlockSpec((tb, N_SLOTS), ahead),
            pl.BlockSpec((tb, N_SLOTS), ahead),
            pl.BlockSpec((tb, N_SLOTS), ahead),
            *_table_specs(),
            pl.BlockSpec((None, tb, D_MODEL), lambda i, j: x_map(jnp.maximum(i - 1, 0))),
            pl.BlockSpec((1, D_MODEL), const),
            pl.BlockSpec((1, D_MODEL), const),
        ],
        out_specs=pl.BlockSpec((tb, D_MODEL), behind),
        out_shape=jax.ShapeDtypeStruct((n_tiles * tb, D_MODEL), F32),
        scratch_shapes=[
            pltpu.VMEM((N_KEYS * hp, N_KEYS), I32),
            pltpu.VMEM((N_KEYS * hp, N_KEYS), I32),
            pltpu.VMEM((tb, D_MODEL), F32),
        ],
        compiler_params=_params(("arbitrary", "arbitrary")),
        name="peer_output",
    )(act, ia, ib, *([v_tab] * TABLE_STREAMS), h1, g2, b2)


PEER_TOKEN_TILE = 512


def kernel(x, meta, ln0_g, ln0_b, w_in, pool_w, pool_scale, w_br_pool, conv_w, a_log, dt_bias, onorm_g, w_br_delta, w_out, ln1_g, ln1_b, peer_wq, peer_k1, peer_k2, peer_u, peer_v, ln2_g, ln2_b):
    batch, seq, d = x.shape
    assert d == D_MODEL and w_in.shape[0] == 1 and seq % PEER_TOKEN_TILE == 0 and batch % DELTA_STREAMS == 0
    t_len = seq + CHUNK
    row = lambda p: p.reshape(1, -1).astype(F32)

    tail = jnp.concatenate([jnp.zeros((PAD, d), x.dtype), meta.astype(x.dtype)], axis=0)

    w = w_in[0]
    c_pool, c_z = POOL_WIDTH, POOL_WIDTH + 4 * DN_WIDTH
    c_b, c_a = c_z + DN_HEADS, c_z + 2 * DN_HEADS
    w_main = jnp.concatenate([w[:, c_pool:c_z], w[:, c_a:], w[:, :c_pool]], axis=1).astype(BF16)
    w_ba = jnp.zeros((d, BA_WIDTH), F32)
    w_ba = w_ba.at[:, :DN_HEADS].set(w[:, c_z:c_b]).at[:, LANES:LANES + DN_HEADS].set(w[:, c_b:c_a]).astype(BF16)
    lane_pad = lambda p: jnp.zeros((1, LANES), F32).at[0, :DN_HEADS].set(p.astype(F32))

    proj, ba = _inproj(x, tail, row(ln0_g), row(ln0_b), w_main, w_ba, seq=seq, t_len=t_len)
    qkvn = _conv_qkv(proj, conv_w[0].astype(F32), batch=batch, seq=seq, t_len=t_len)
    o = _delta_rule(qkvn, ba, proj, lane_pad(-jnp.exp(a_log[0].astype(F32))), lane_pad(dt_bias[0]),
                    row(onorm_g[0]), batch=batch, t_len=t_len)
    h1 = _merge(o, proj, x, tail, w_br_delta[0].astype(BF16), w_out[0].astype(BF16), pool_w[0].astype(BF16),
                row(pool_scale[0]), w_br_pool[0].astype(BF16), row(ln0_g), row(ln0_b), row(ln1_g[0]), row(ln1_b[0]),
                batch=batch, seq=seq, t_len=t_len)
    h1 = h1.reshape(batch, t_len, d)

    ia, ib, gate = _peer_query(h1, peer_wq[0].astype(BF16), peer_k1[0].astype(BF16), peer_k2[0].astype(BF16),
                               seq=seq, tq=LANES)

    act = _peer_hidden(h1, peer_u[0].astype(BF16), ia, ib, gate, seq=seq, tb=PEER_TOKEN_TILE)
    out = _peer_output(act, ia, ib, peer_v[0].astype(BF16), h1, row(ln2_g[0]), row(ln2_b[0]),
                       seq=seq, tb=PEER_TOKEN_TILE)
    return out.reshape(batch, seq, d)
```

```python
import functools

import jax
import jax.numpy as jnp
from jax import lax
from jax.experimental import pallas as pl
from jax.experimental.pallas import tpu as pltpu

F32 = jnp.float32
BF16 = jnp.bfloat16
I32 = jnp.int32

D_MODEL = 1024
N_META = 16
CHUNK = 64
PAD = CHUNK - N_META
POOL_GROUPS = 4
POOL_GROUP_DIM = 128
POOL_WIDTH = POOL_GROUPS * POOL_GROUP_DIM
POOL_WINDOWS = (2, 4, 8, 16)
POOL_HALO = 16
DN_HEADS = 8
DN_HEAD_DIM = 128
DN_WIDTH = DN_HEADS * DN_HEAD_DIM
CONV_WIDTH = 4
PEER_HEADS = 8
PEER_HALF = 128
N_KEYS = 128
PEER_TOPK = 16
N_SLOTS = PEER_HEADS * PEER_TOPK
LN_EPS = 1e-5
RMS_EPS = 1e-6
L2_EPS = 1e-6
ALPHA = 2.0 ** 0.25

COL_QKV = 0
COL_Z = 3 * DN_WIDTH
COL_GP = COL_Z + DN_WIDTH
COL_GD = COL_GP + D_MODEL
COL_POOL = COL_GD + D_MODEL
N_PROJ = COL_POOL + POOL_WIDTH
BA_WIDTH = 256

SUBLANES = 8
LANES = 128
V7X_VMEM_BYTES = 64 * 1024 * 1024
VMEM_LIMIT = V7X_VMEM_BYTES - 4 * 1024 * 1024

_CAND = [(a, b) for a in range(PEER_TOPK) for b in range(PEER_TOPK) if (a + 1) * (b + 1) <= PEER_TOPK]


def _params(sem):
    return pltpu.CompilerParams(dimension_semantics=sem, vmem_limit_bytes=VMEM_LIMIT)


def _layer_norm(x, g, b):
    mu = jnp.mean(x, axis=-1, keepdims=True)
    xc = x - mu
    var = jnp.mean(xc * xc, axis=-1, keepdims=True)
    return xc * lax.rsqrt(var + LN_EPS) * g + b


def _zero_pad_rows(y, row, seq):
    return jnp.where(row >= seq, jnp.where(row < seq + PAD, 0.0, y), y)


def _sigmoid(x):
    return 1.0 / (1.0 + jnp.exp(-x))


def _dot(a, b):
    return jnp.dot(a.astype(BF16), b.astype(BF16), preferred_element_type=F32)


def _dot_nt(a, b):
    return lax.dot_general(a.astype(BF16), b.astype(BF16), (((1,), (1,)), ((), ())), preferred_element_type=F32)


def _assemble_rows(x_ref, tail_ref, rows_ref, tile_in_batch, tiles_per_batch):
    body = rows_ref.shape[0] - CHUNK

    @pl.when(tile_in_batch < tiles_per_batch - 1)
    def _():
        rows_ref[...] = x_ref[...]

    @pl.when(tile_in_batch == tiles_per_batch - 1)
    def _():
        rows_ref[:body] = x_ref[:body]
        rows_ref[body:] = tail_ref[...]


def _inproj_kernel(x_ref, tail_ref, g_ref, b_ref, w_ref, wba_ref, proj_ref, ba_ref, rows_ref, xn_ref, *,
                   tiles_per_batch, seq):
    i = pl.program_id(0)
    j = pl.program_id(1)
    tm = x_ref.shape[0]

    @pl.when(j == 0)
    def _():
        _assemble_rows(x_ref, tail_ref, rows_ref, lax.rem(i, tiles_per_batch), tiles_per_batch)
        y = _layer_norm(rows_ref[...], g_ref[...], b_ref[...])
        row = lax.broadcasted_iota(I32, (tm, 1), 0) + lax.rem(i, tiles_per_batch) * tm
        xn = _zero_pad_rows(y, row, seq).astype(BF16)
        xn_ref[...] = xn
        ba_ref[...] = _dot(xn, wba_ref[...])

    proj_ref[...] = _dot(xn_ref[...], w_ref[...]).astype(proj_ref.dtype)


def _inproj(x, tail, ln_g, ln_b, w_main, w_ba, *, seq, t_len):
    rows = x.shape[0] * t_len
    tiles_per_batch = 4
    tm = t_len // tiles_per_batch
    tn = N_PROJ // 4
    kern = functools.partial(_inproj_kernel, tiles_per_batch=tiles_per_batch, seq=seq)
    return pl.pallas_call(
        kern,
        grid=(rows // tm, N_PROJ // tn),
        in_specs=[
            pl.BlockSpec((None, tm, D_MODEL), lambda i, j: (i // tiles_per_batch, lax.rem(i, tiles_per_batch), 0)),
            pl.BlockSpec((CHUNK, D_MODEL), lambda i, j: (0, 0)),
            pl.BlockSpec((1, D_MODEL), lambda i, j: (0, 0)),
            pl.BlockSpec((1, D_MODEL), lambda i, j: (0, 0)),
            pl.BlockSpec((D_MODEL, tn), lambda i, j: (0, j)),
            pl.BlockSpec((D_MODEL, BA_WIDTH), lambda i, j: (0, 0)),
        ],
        out_specs=[
            pl.BlockSpec((tm, tn), lambda i, j: (i, j)),
            pl.BlockSpec((tm, BA_WIDTH), lambda i, j: (i, 0)),
        ],
        out_shape=[
            jax.ShapeDtypeStruct((rows, N_PROJ), BF16),
            jax.ShapeDtypeStruct((rows, BA_WIDTH), F32),
        ],
        scratch_shapes=[pltpu.VMEM((tm, D_MODEL), F32), pltpu.VMEM((tm, D_MODEL), BF16)],
        compiler_params=_params(("arbitrary", "arbitrary")),
        name="inproj",
    )(x, tail, ln_g, ln_b, w_main, w_ba)


def _pool_branch_tile(xp_ref, halo_ref, pw_ref, ps_ref, wbp_ref, first_row, seq):
    tm = xp_ref.shape[0]
    v = jnp.concatenate([halo_ref[...].astype(F32), xp_ref[...].astype(F32)], axis=0)
    row = lax.broadcasted_iota(I32, (tm, POOL_GROUP_DIM), 0) + first_row
    meta_pos = row - (seq + PAD)
    outs = []
    for gi, w in enumerate(POOL_WINDOWS):
        vg = v[:, gi * POOL_GROUP_DIM:(gi + 1) * POOL_GROUP_DIM]
        s = vg
        shift = 1
        while shift < w:
            s = s + pltpu.roll(s, shift, axis=0)
            shift *= 2
        s = s[POOL_HALO:]
        xg = vg[POOL_HALO:]
        count = jnp.where(meta_pos >= 0, jnp.minimum(meta_pos + 1, w), w).astype(F32)
        pooled = s / count - xg
        mixed = _dot(pooled.astype(BF16), pw_ref[gi])
        outs.append(mixed)
    y_pool = jnp.concatenate(outs, axis=1) * ps_ref[...]
    return _dot(y_pool.astype(BF16), wbp_ref[...])


def _conv_kernel(x_ref, w_ref, o_ref, *, seq):
    s = pl.program_id(1)
    x = x_ref[...].astype(F32)
    t_len = x.shape[0]
    w = w_ref[...]
    y = x * w[CONV_WIDTH - 1:CONV_WIDTH, :]
    for lag in range(1, CONV_WIDTH):
        y = y + pltpu.roll(x, lag, axis=0) * w[CONV_WIDTH - 1 - lag:CONV_WIDTH - lag, :]
    y = y * _sigmoid(y)
    ss = jnp.sum(y * y, axis=-1, keepdims=True)
    q_scale = jnp.where(s < DN_HEADS, DN_HEAD_DIM ** -0.5, 1.0).astype(F32)
    fac = jnp.where(s < 2 * DN_HEADS, lax.rsqrt(ss + L2_EPS) * q_scale, 1.0)
    row = lax.broadcasted_iota(I32, (t_len, 1), 0)
    o_ref[...] = _zero_pad_rows(y * fac, row, seq).astype(o_ref.dtype)


def _conv_qkv(proj, conv_w, *, batch, seq, t_len):
    rows = proj.shape[0]
    n_slabs = 3 * DN_HEADS
    return pl.pallas_call(
        functools.partial(_conv_kernel, seq=seq),
        grid=(batch, n_slabs),
        in_specs=[
            pl.BlockSpec((t_len, DN_HEAD_DIM), lambda b, s: (b, COL_QKV // DN_HEAD_DIM + s)),
            pl.BlockSpec((CONV_WIDTH, DN_HEAD_DIM), lambda b, s: (0, s)),
        ],
        out_specs=pl.BlockSpec((t_len, DN_HEAD_DIM), lambda b, s: (b, s)),
        out_shape=jax.ShapeDtypeStruct((rows, 3 * DN_WIDTH), BF16),
        compiler_params=_params(("arbitrary", "arbitrary")),
        name="conv_qkv",
    )(proj, conv_w)


def _inv_unit_lower(a, masks, eye):
    m8, m16, m32, m64 = masks
    n = [x * m8 for x in a]
    n2 = [_dot(x, x) for x in n]
    n4 = [_dot(x, x) for x in n2]
    t = [eye - x for x in n]
    t = [x + _dot(x, y) for x, y in zip(t, n2)]
    t = [x + _dot(x, y) for x, y in zip(t, n4)]
    for m in (m16, m32, m64):
        at = [_dot(x * m, y) for x, y in zip(a, t)]
        t = [x - _dot(x, y) for x, y in zip(t, at)]
    return t


DELTA_STREAMS = 4


def _delta_kernel(q_ref, k_ref, v_ref, ba_ref, z_ref, nega_ref, dtb_ref, og_ref, o_ref, state_ref):
    n = DELTA_STREAMS
    c = pl.program_id(1)

    @pl.when(c == 0)
    def _():
        state_ref[...] = jnp.zeros_like(state_ref)

    r = lax.broadcasted_iota(I32, (CHUNK, CHUNK), 0)
    cc = lax.broadcasted_iota(I32, (CHUNK, CHUNK), 1)
    causal = r >= cc
    strict = r > cc
    eye = jnp.where(r == cc, 1.0, 0.0).astype(F32)
    tril = jnp.where(causal, 1.0, 0.0).astype(F32)

    def blk(x, s):
        return lax.shift_right_logical(x, s)

    lower = jnp.where(strict, 1.0, 0.0).astype(F32)
    m8 = jnp.where(blk(r, 3) == blk(cc, 3), lower, 0.0)
    m16 = jnp.where((blk(r, 4) == blk(cc, 4)) & (blk(r, 3) != blk(cc, 3)), lower, 0.0)
    m32 = jnp.where((blk(r, 5) == blk(cc, 5)) & (blk(r, 4) != blk(cc, 4)), lower, 0.0)
    m64 = jnp.where(blk(r, 5) != blk(cc, 5), lower, 0.0)
    masks = (m8, m16, m32, m64)

    beta_all, gcum, gcum_t = [], [], []
    for s in range(n):
        ba = ba_ref[s]
        beta_all.append(_sigmoid(ba[:, :LANES]))
        x = ba[:, LANES:] + dtb_ref[...]
        softplus = jnp.maximum(x, 0.0) + jnp.log(1.0 + jnp.exp(-jnp.abs(x)))
        g_all = nega_ref[...] * softplus
        gcum.append(lax.dot_general(tril, g_all, (((1,), (0,)), ((), ())), precision=lax.Precision.HIGHEST,
                                    preferred_element_type=F32))
        gcum_t.append(gcum[s].T)

    chains = [(s, h) for s in range(n) for h in range(DN_HEADS)]
    ids = range(len(chains))
    sl = [slice(h * DN_HEAD_DIM, (h + 1) * DN_HEAD_DIM) for _, h in chains]
    q = [q_ref[s, :, sl[i]].astype(F32) for i, (s, h) in enumerate(chains)]
    k = [k_ref[s, :, sl[i]].astype(F32) for i, (s, h) in enumerate(chains)]
    v = [v_ref[s, :, sl[i]].astype(F32) for i, (s, h) in enumerate(chains)]
    beta = [beta_all[s][:, h:h + 1] for s, h in chains]
    gc = [gcum[s][:, h:h + 1] for s, h in chains]
    gr = [gcum_t[s][h:h + 1, :] for s, h in chains]
    g_last = [gcum[s][CHUNK - 1:CHUNK, h:h + 1] for s, h in chains]
    decay = [jnp.where(causal, jnp.exp(jnp.where(causal, gc[i] - gr[i], 0.0)), 0.0) for i in ids]
    eg = [jnp.exp(gc[i]) for i in ids]
    kb = [k[i] * beta[i] for i in ids]
    kq = [_dot_nt(jnp.concatenate([kb[i], q[i]], axis=0), k[i]) for i in ids]
    a_mat = [jnp.where(strict, kq[i][:CHUNK] * decay[i], 0.0) for i in ids]
    t_inv = _inv_unit_lower(a_mat, masks, eye)
    rhs = [jnp.concatenate([v[i] * beta[i], kb[i] * eg[i]], axis=1) for i in ids]
    uw = [_dot(t_inv[i], rhs[i]) for i in ids]
    s_prev = [state_ref[i] for i in ids]
    wq = [jnp.concatenate([uw[i][:, DN_HEAD_DIM:], q[i] * eg[i]], axis=0) for i in ids]
    ws = [_dot(wq[i], s_prev[i]) for i in ids]
    v_new = [uw[i][:, :DN_HEAD_DIM] - ws[i][:CHUNK] for i in ids]
    attn = [kq[i][CHUNK:] * decay[i] for i in ids]
    intra = [_dot(attn[i], v_new[i]) for i in ids]
    k_dec_t = [(k[i] * jnp.exp(g_last[i] - gc[i])).T for i in ids]
    kv = [_dot(k_dec_t[i], v_new[i]) for i in ids]
    for i, (s, h) in enumerate(chains):
        state_ref[i] = s_prev[i] * jnp.exp(g_last[i]) + kv[i]
        out = ws[i][CHUNK:] + intra[i]
        out = out * lax.rsqrt(jnp.mean(out * out, axis=-1, keepdims=True) + RMS_EPS) * og_ref[...]
        zh = z_ref[s, :, sl[i]].astype(F32)
        o_ref[s, :, sl[i]] = (out * (zh * _sigmoid(zh))).astype(o_ref.dtype)


def _delta_rule(qkvn, ba, proj, neg_a, dt_bias, onorm_g, *, batch, t_len):
    rows = qkvn.shape[0]
    n_chunks = t_len // CHUNK

    n = DELTA_STREAMS
    assert batch % n == 0
    groups = batch // n
    by_stream = lambda a: a.reshape(groups, n, t_len, a.shape[-1])

    def chunk_block(col):
        return lambda g, c: (g, 0, lax.rem(c + n_chunks - 1, n_chunks), col)

    const = lambda g, c: (0, 0)
    out = pl.pallas_call(
        _delta_kernel,
        grid=(groups, n_chunks),
        in_specs=[
            pl.BlockSpec((None, n, CHUNK, DN_WIDTH), chunk_block(0)),
            pl.BlockSpec((None, n, CHUNK, DN_WIDTH), chunk_block(1)),
            pl.BlockSpec((None, n, CHUNK, DN_WIDTH), chunk_block(2)),
            pl.BlockSpec((None, n, CHUNK, BA_WIDTH), chunk_block(0)),
            pl.BlockSpec((None, n, CHUNK, DN_WIDTH), chunk_block(COL_Z // DN_WIDTH)),
            pl.BlockSpec((1, LANES), const),
            pl.BlockSpec((1, LANES), const),
            pl.BlockSpec((1, DN_HEAD_DIM), const),
        ],
        out_specs=pl.BlockSpec((None, n, CHUNK, DN_WIDTH), chunk_block(0)),
        out_shape=jax.ShapeDtypeStruct((groups, n, t_len, DN_WIDTH), BF16),
        scratch_shapes=[pltpu.VMEM((n * DN_HEADS, DN_HEAD_DIM, DN_HEAD_DIM), F32)],
        compiler_params=_params(("arbitrary", "arbitrary")),
        name="delta_rule",
    )(by_stream(qkvn), by_stream(qkvn), by_stream(qkvn), by_stream(ba), by_stream(proj), neg_a, dt_bias, onorm_g)
    return out.reshape(rows, DN_WIDTH)


def _merge_kernel(o_ref, xp_ref, halo_ref, gp_ref, gd_ref, x_ref, tail_ref, wd_ref, wo_ref, pw_ref, ps_ref, wbp_ref,
                  g0_ref, b0_ref, g1_ref, b1_ref, h1_ref, rows_ref, *, tiles_per_batch, seq):
    tile_in_batch = lax.rem(pl.program_id(0), tiles_per_batch)
    first_row = tile_in_batch * o_ref.shape[0]
    y_a = _pool_branch_tile(xp_ref, halo_ref, pw_ref, ps_ref, wbp_ref, first_row, seq)
    y_b = _dot(o_ref[...], wd_ref[...])
    merged = _sigmoid(gp_ref[...].astype(F32)) * y_a + _sigmoid(gd_ref[...].astype(F32)) * y_b
    mix = _dot(merged.astype(BF16), wo_ref[...])
    _assemble_rows(x_ref, tail_ref, rows_ref, tile_in_batch, tiles_per_batch)
    h = _layer_norm(rows_ref[...], g0_ref[...], b0_ref[...])
    h1_ref[...] = _layer_norm(ALPHA * h + mix, g1_ref[...], b1_ref[...])


def _merge(o, proj, x, tail, w_br_delta, w_out, pool_w, pool_scale, w_br_pool, g0, b0, g1, b1, *, batch, seq, t_len):
    bf16_rows = 2 * SUBLANES
    tiles_per_batch = next(n for n in (10, 8, 4, 2, 1) if t_len % (n * bf16_rows) == 0)
    tm = t_len // tiles_per_batch
    halo_blocks_per_batch = t_len // POOL_HALO
    halo_blocks_per_tile = tm // POOL_HALO
    pool_col = COL_POOL // POOL_WIDTH
    row = lambda i: (i, 0)
    const = lambda i: (0, 0)

    def halo_map(i):
        b, t = i // tiles_per_batch, lax.rem(i, tiles_per_batch)
        prev = lax.rem(t * halo_blocks_per_tile + halo_blocks_per_batch - 1, halo_blocks_per_batch)
        return (b * halo_blocks_per_batch + prev, pool_col)

    return pl.pallas_call(
        functools.partial(_merge_kernel, tiles_per_batch=tiles_per_batch, seq=seq),
        grid=(batch * tiles_per_batch,),
        in_specs=[
            pl.BlockSpec((tm, DN_WIDTH), row),
            pl.BlockSpec((tm, POOL_WIDTH), lambda i: (i, pool_col)),
            pl.BlockSpec((POOL_HALO, POOL_WIDTH), halo_map),
            pl.BlockSpec((tm, D_MODEL), lambda i: (i, COL_GP // D_MODEL)),
            pl.BlockSpec((tm, D_MODEL), lambda i: (i, COL_GD // D_MODEL)),
            pl.BlockSpec((None, tm, D_MODEL), lambda i: (i // tiles_per_batch, lax.rem(i, tiles_per_batch), 0)),
            pl.BlockSpec((CHUNK, D_MODEL), const),
            pl.BlockSpec((DN_WIDTH, D_MODEL), const),
            pl.BlockSpec((D_MODEL, D_MODEL), const),
            pl.BlockSpec((POOL_GROUPS, POOL_GROUP_DIM, POOL_GROUP_DIM), lambda i: (0, 0, 0)),
            pl.BlockSpec((1, POOL_WIDTH), const),
            pl.BlockSpec((POOL_WIDTH, D_MODEL), const),
            pl.BlockSpec((1, D_MODEL), const),
            pl.BlockSpec((1, D_MODEL), const),
            pl.BlockSpec((1, D_MODEL), const),
            pl.BlockSpec((1, D_MODEL), const),
        ],
        out_specs=pl.BlockSpec((tm, D_MODEL), row),
        out_shape=jax.ShapeDtypeStruct((batch * t_len, D_MODEL), F32),
        scratch_shapes=[pltpu.VMEM((tm, D_MODEL), F32)],
        compiler_params=_params(("arbitrary",)),
        name="merge",
    )(o, proj, proj, proj, proj, x, tail, w_br_delta, w_out, pool_w, pool_scale, w_br_pool, g0, b0, g1, b1)


def _tile(ref, k):
    return ref[k * SUBLANES:(k + 1) * SUBLANES, :]


def _argmax_tournament(vals, ids):
    nodes = list(zip(vals, ids))
    while len(nodes) > 1:
        nxt = []
        for p in range(0, len(nodes) - 1, 2):
            (va, ia), (vb, ib) = nodes[p], nodes[p + 1]
            first = va >= vb
            nxt.append((jnp.maximum(va, vb), jnp.where(first, ia, ib)))
        if len(nodes) % 2:
            nxt.append(nodes[-1])
        nodes = nxt
    return nodes[0]


def _query_kernel(x_ref, wq_ref, k1_ref, k2_ref, ia_ref, ib_ref, gate_ref, s1_ref, s2_ref, m1_ref, i1_ref, m2_ref,
                  i2_ref, c_ref, e_ref, sc_ref, ex_ref):
    tq = x_ref.shape[0]
    xb = x_ref[...].astype(BF16)
    q = _dot(xb, wq_ref[...])
    for h in range(PEER_HEADS):
        q1 = q[:, (2 * h) * PEER_HALF:(2 * h + 1) * PEER_HALF].astype(BF16)
        q2 = q[:, (2 * h + 1) * PEER_HALF:(2 * h + 2) * PEER_HALF].astype(BF16)
        s1_ref[pl.ds(h, N_KEYS, stride=SUBLANES), :] = _dot_nt(k1_ref[h], q1)
        s2_ref[pl.ds(h, N_KEYS, stride=SUBLANES), :] = _dot_nt(k2_ref[h], q2)

    neg_inf = jnp.float32(-jnp.inf)

    def sub_round(r, carry):
        for s_ref, m_ref, i_ref in ((s1_ref, m1_ref, i1_ref), (s2_ref, m2_ref, i2_ref)):
            vals = [_tile(s_ref, k) for k in range(N_KEYS)]
            best, arg = _argmax_tournament(vals, list(range(N_KEYS)))
            m_ref[r] = best
            i_ref[r] = arg
            for k in range(N_KEYS):
                s_ref[k * SUBLANES:(k + 1) * SUBLANES, :] = jnp.where(arg == k, neg_inf, vals[k])
        return carry

    lax.fori_loop(0, PEER_TOPK, sub_round, 0)

    for j, (a, b) in enumerate(_CAND):
        c_ref[j * SUBLANES:(j + 1) * SUBLANES, :] = m1_ref[a] + m2_ref[b]
        e_ref[j * SUBLANES:(j + 1) * SUBLANES, :] = i1_ref[a] * N_KEYS + i2_ref[b]

    def pair_round(r, carry):
        vals = [_tile(c_ref, j) for j in range(len(_CAND))]
        ids = [_tile(e_ref, j) for j in range(len(_CAND))]
        best, arg = _argmax_tournament(vals, ids)
        sc_ref[r] = best
        ex_ref[r] = arg
        for j in range(len(_CAND)):
            c_ref[j * SUBLANES:(j + 1) * SUBLANES, :] = jnp.where(ids[j] == arg, neg_inf, vals[j])
        return carry

    lax.fori_loop(0, PEER_TOPK, pair_round, 0)

    sc = sc_ref[...]
    ex = jnp.exp(sc - sc[0:1])
    gate = ex / jnp.sum(ex, axis=0, keepdims=True)
    gate_ref[...] = gate.reshape(N_SLOTS, tq).T
    expert = ex_ref[...].reshape(N_SLOTS, tq).T
    ia_ref[...] = lax.shift_right_logical(expert, N_KEYS.bit_length() - 1)
    ib_ref[...] = lax.bitwise_and(expert, N_KEYS - 1)


def _token_tile_map(seq, tile):
    per_batch = seq // tile
    return lambda i, *_: (i // per_batch, lax.rem(i, per_batch), 0)


def _peer_query(h1, wq, k1, k2, *, seq, tq):
    n_tok = h1.shape[0] * seq
    slot_shape = (PEER_TOPK, PEER_HEADS, tq)
    return pl.pallas_call(
        _query_kernel,
        grid=(n_tok // tq,),
        in_specs=[
            pl.BlockSpec((None, tq, D_MODEL), _token_tile_map(seq, tq)),
            pl.BlockSpec((D_MODEL, 2 * PEER_HALF * PEER_HEADS), lambda i: (0, 0)),
            pl.BlockSpec((PEER_HEADS, N_KEYS, PEER_HALF), lambda i: (0, 0, 0)),
            pl.BlockSpec((PEER_HEADS, N_KEYS, PEER_HALF), lambda i: (0, 0, 0)),
        ],
        out_specs=[pl.BlockSpec((tq, N_SLOTS), lambda i: (i, 0)) for _ in range(3)],
        out_shape=[
            jax.ShapeDtypeStruct((n_tok, N_SLOTS), I32),
            jax.ShapeDtypeStruct((n_tok, N_SLOTS), I32),
            jax.ShapeDtypeStruct((n_tok, N_SLOTS), F32),
        ],
        scratch_shapes=[
            pltpu.VMEM((N_KEYS * SUBLANES, tq), F32),
            pltpu.VMEM((N_KEYS * SUBLANES, tq), F32),
            pltpu.VMEM(slot_shape, F32),
            pltpu.VMEM(slot_shape, I32),
            pltpu.VMEM(slot_shape, F32),
            pltpu.VMEM(slot_shape, I32),
            pltpu.VMEM((len(_CAND) * SUBLANES, tq), F32),
            pltpu.VMEM((len(_CAND) * SUBLANES, tq), I32),
            pltpu.VMEM(slot_shape, F32),
            pltpu.VMEM(slot_shape, I32),
        ],
        compiler_params=_params(("arbitrary",)),
        name="peer_query",
    )(h1, wq, k1, k2)


IA_PER_STEP = 16
EXPERTS_PER_STEP = IA_PER_STEP * N_KEYS
PEER_STEPS = (N_KEYS * N_KEYS) // EXPERTS_PER_STEP
PEER_PIECES = 2
TABLE_STREAMS = 4


def _table_specs():
    rows = EXPERTS_PER_STEP // TABLE_STREAMS
    return [pl.BlockSpec((rows, D_MODEL), lambda i, j, p=p: (j * TABLE_STREAMS + p, 0)) for p in range(TABLE_STREAMS)]


def _one_hot_rows(idx_row):
    iota = lax.broadcasted_iota(I32, (N_KEYS, N_SLOTS), 0)
    return jnp.where(iota == idx_row, 1.0, 0.0).astype(BF16)


def _block_diag(a, b):
    zero = jnp.zeros_like(a)
    return jnp.concatenate([jnp.concatenate([a, zero], axis=1), jnp.concatenate([zero, b], axis=1)], axis=0)


def _hidden_kernel(x_ref, *refs, tb, hp):
    u_refs = refs[:TABLE_STREAMS]
    ia_ref, ib_ref, gate_ref, act_ref, xb_ref, h0_ref, h1_ref, hs_ref = refs[TABLE_STREAMS:]
    i = pl.program_id(0)
    j = pl.program_id(1)
    tokens = tb // PEER_STEPS // PEER_PIECES
    parts = TABLE_STREAMS // PEER_PIECES
    ia_pairs = IA_PER_STEP // PEER_PIECES // 2

    @pl.when(jnp.logical_and(i == 0, j == 0))
    def _():
        h1_ref[...] = jnp.zeros_like(h1_ref)

    @pl.when(j == 0)
    def _():
        xb_ref[...] = x_ref[...].astype(BF16)

    def step(fill_ref, drain_ref):
        iota = lax.broadcasted_iota(I32, (N_KEYS, N_SLOTS), 0)
        for q in range(PEER_PIECES):
            base = pl.multiple_of((j * PEER_PIECES + q) * tokens, tokens)
            ia_rows = ia_ref[pl.ds(base, tokens), :]
            ib_rows = ib_ref[pl.ds(base, tokens), :]
            lhs, rhs = [], []
            for p in range(tokens // 2):
                tiles = []
                for t in (2 * p, 2 * p + 1):
                    start = pl.multiple_of((base + t) * hp, SUBLANES)
                    tiles.append(pltpu.bitcast(drain_ref[pl.ds(start, N_KEYS // 2), :], BF16))
                lhs.append(jnp.concatenate(tiles, axis=1))
                rhs.append(_block_diag(_one_hot_rows(ib_rows[2 * p:2 * p + 1, :]),
                                       _one_hot_rows(ib_rows[2 * p + 1:2 * p + 2, :])))

            experts = jnp.concatenate([u_refs[q * parts + p][...] for p in range(parts)], axis=0)
            hc = _dot_nt(xb_ref[...], experts)
            for m in range(ia_pairs):
                even = hc[:, (2 * m) * N_KEYS:(2 * m + 1) * N_KEYS]
                odd = hc[:, (2 * m + 1) * N_KEYS:(2 * m + 2) * N_KEYS]
                packed = pltpu.pack_elementwise([even, odd], packed_dtype=BF16)
                pair = (j * PEER_PIECES + q) * ia_pairs + m
                fill_ref[pl.ds(pair, tb, stride=hp), :] = lax.bitcast_convert_type(packed, I32)

            rows = []
            for p in range(tokens // 2):
                both = _dot(lhs[p], rhs[p])
                for k in range(2):
                    mine = both[:, k * N_SLOTS:(k + 1) * N_SLOTS]
                    picked = jnp.where(iota == ia_rows[2 * p + k:2 * p + k + 1, :], mine, 0.0)
                    rows.append(jnp.sum(picked, axis=0, keepdims=True))
            hs_ref[pl.ds(base, tokens), :] = jnp.concatenate(rows, axis=0)

    parity = lax.rem(i, 2)

    @pl.when(parity == 0)
    def _():
        step(h0_ref, h1_ref)

    @pl.when(parity == 1)
    def _():
        step(h1_ref, h0_ref)

    @pl.when(j == PEER_STEPS - 1)
    def _():
        hid = hs_ref[...]
        gelu = 0.5 * hid * (1.0 + lax.erf(hid * (2.0 ** -0.5)))
        act_ref[...] = gelu * gate_ref[...]


def _peer_hidden(h1, u_tab, ia, ib, gate, *, seq, tb):
    n_tiles = h1.shape[0] * seq // tb
    hp = N_KEYS // 2 + SUBLANES
    x_map = _token_tile_map(seq, tb)
    ahead = lambda i, j: x_map(jnp.minimum(i, n_tiles - 1))
    behind = lambda i, j: (jnp.maximum(i - 1, 0), 0)
    return pl.pallas_call(
        functools.partial(_hidden_kernel, tb=tb, hp=hp),
        grid=(n_tiles + 1, PEER_STEPS),
        in_specs=[
            pl.BlockSpec((None, tb, D_MODEL), ahead),
            *_table_specs(),
            pl.BlockSpec((tb, N_SLOTS), behind),
            pl.BlockSpec((tb, N_SLOTS), behind),
            pl.BlockSpec((tb, N_SLOTS), behind),
        ],
        out_specs=pl.BlockSpec((tb, N_SLOTS), behind),
        out_shape=jax.ShapeDtypeStruct((n_tiles * tb, N_SLOTS), F32),
        scratch_shapes=[
            pltpu.VMEM((tb, D_MODEL), BF16),
            pltpu.VMEM((tb * hp, N_KEYS), I32),
            pltpu.VMEM((tb * hp, N_KEYS), I32),
            pltpu.VMEM((tb, N_SLOTS), F32),
        ],
        compiler_params=_params(("arbitrary", "arbitrary")),
        name="peer_hidden",
    )(h1, *([u_tab] * TABLE_STREAMS), ia, ib, gate)


def _output_kernel(act_ref, ia_ref, ib_ref, *refs, tb, hp):
    v_refs = refs[:TABLE_STREAMS]
    h1_ref, g_ref, b_ref, out_ref, a0_ref, a1_ref, acc_ref = refs[TABLE_STREAMS:]
    i = pl.program_id(0)
    j = pl.program_id(1)
    tokens = tb // PEER_STEPS // PEER_PIECES
    parts = TABLE_STREAMS // PEER_PIECES
    ia_per_piece = IA_PER_STEP // PEER_PIECES

    @pl.when(jnp.logical_and(i == 0, j == 0))
    def _():
        a1_ref[...] = jnp.zeros_like(a1_ref)

    @pl.when(j == 0)
    def _():
        acc_ref[...] = jnp.zeros_like(acc_ref)

    def step(fill_ref, drain_ref):
        iota = lax.broadcasted_iota(I32, (N_KEYS, N_SLOTS), 0)
        acc = acc_ref[...]
        for q in range(PEER_PIECES):
            base = pl.multiple_of((j * PEER_PIECES + q) * tokens, tokens)
            ia_rows = ia_ref[pl.ds(base, tokens), :]
            ib_rows = ib_ref[pl.ds(base, tokens), :]
            act_rows = act_ref[pl.ds(base, tokens), :]
            lhs, sels = [], []
            for p in range(tokens // 2):
                spread = [jnp.where(iota == ia_rows[t:t + 1, :], act_rows[t:t + 1, :], 0.0).astype(BF16)
                          for t in (2 * p, 2 * p + 1)]
                lhs.append(jnp.concatenate(spread, axis=1))
                sels += [jnp.where(iota == ib_rows[t:t + 1, :], 1.0, 0.0).T.astype(BF16)
                         for t in (2 * p, 2 * p + 1)]

            tiles = []
            for t in range(ia_per_piece):
                ia = j * IA_PER_STEP + q * ia_per_piece + t
                start = pl.multiple_of(ia * hp, SUBLANES)
                tiles.append(pltpu.bitcast(drain_ref[pl.ds(start, tb // 2), :], BF16))
            table_rows = jnp.concatenate([v_refs[q * parts + p][...] for p in range(parts)], axis=0)
            acc = acc + _dot(jnp.concatenate(tiles, axis=1), table_rows)

            pair_base = (j * PEER_PIECES + q) * (tokens // 2)
            for p in range(tokens // 2):
                both = _dot(lhs[p], _block_diag(sels[2 * p], sels[2 * p + 1]))
                packed = pltpu.pack_elementwise([both[:, :N_KEYS], both[:, N_KEYS:]], packed_dtype=BF16)
                fill_ref[pl.ds(pair_base + p, N_KEYS, stride=hp), :] = lax.bitcast_convert_type(packed, I32)
        acc_ref[...] = acc

    parity = lax.rem(i, 2)

    @pl.when(parity == 0)
    def _():
        step(a0_ref, a1_ref)

    @pl.when(parity == 1)
    def _():
        step(a1_ref, a0_ref)

    @pl.when(j == PEER_STEPS - 1)
    def _():
        out_ref[...] = _layer_norm(ALPHA * h1_ref[...] + acc_ref[...], g_ref[...], b_ref[...])


def _peer_output(act, ia, ib, v_tab, h1, g2, b2, *, seq, tb):
    n_tiles = h1.shape[0] * seq // tb
    hp = tb // 2 + SUBLANES
    x_map = _token_tile_map(seq, tb)
    ahead = lambda i, j: (jnp.minimum(i, n_tiles - 1), 0)
    behind = lambda i, j: (jnp.maximum(i - 1, 0), 0)
    const = lambda i, j: (0, 0)
    return pl.pallas_call(
        functools.partial(_output_kernel, tb=tb, hp=hp),
        grid=(n_tiles + 1, PEER_STEPS),
        in_specs=[
            pl.BlockSpec((tb, N_SLOTS), ahead),
            pl.BlockSpec((tb, N_SLOTS), ahead),
            pl.BlockSpec((tb, N_SLOTS), ahead),
            *_table_specs(),
            pl.BlockSpec((None, tb, D_MODEL), lambda i, j: x_map(jnp.maximum(i - 1, 0))),
            pl.BlockSpec((1, D_MODEL), const),
            pl.BlockSpec((1, D_MODEL), const),
        ],
        out_specs=pl.BlockSpec((tb, D_MODEL), behind),
        out_shape=jax.ShapeDtypeStruct((n_tiles * tb, D_MODEL), F32),
        scratch_shapes=[
            pltpu.VMEM((N_KEYS * hp, N_KEYS), I32),
            pltpu.VMEM((N_KEYS * hp, N_KEYS), I32),
            pltpu.VMEM((tb, D_MODEL), F32),
        ],
        compiler_params=_params(("arbitrary", "arbitrary")),
        name="peer_output",
    )(act, ia, ib, *([v_tab] * TABLE_STREAMS), h1, g2, b2)


PEER_TOKEN_TILE = 512


def kernel(x, meta, ln0_g, ln0_b, w_in, pool_w, pool_scale, w_br_pool, conv_w, a_log, dt_bias, onorm_g, w_br_delta, w_out, ln1_g, ln1_b, peer_wq, peer_k1, peer_k2, peer_u, peer_v, ln2_g, ln2_b):
    batch, seq, d = x.shape
    assert d == D_MODEL and w_in.shape[0] == 1 and seq % PEER_TOKEN_TILE == 0 and batch % DELTA_STREAMS == 0
    t_len = seq + CHUNK
    row = lambda p: p.reshape(1, -1).astype(F32)

    tail = jnp.concatenate([jnp.zeros((PAD, d), x.dtype), meta.astype(x.dtype)], axis=0)

    w = w_in[0]
    c_pool, c_z = POOL_WIDTH, POOL_WIDTH + 4 * DN_WIDTH
    c_b, c_a = c_z + DN_HEADS, c_z + 2 * DN_HEADS
    w_main = jnp.concatenate([w[:, c_pool:c_z], w[:, c_a:], w[:, :c_pool]], axis=1).astype(BF16)
    w_ba = jnp.zeros((d, BA_WIDTH), F32)
    w_ba = w_ba.at[:, :DN_HEADS].set(w[:, c_z:c_b]).at[:, LANES:LANES + DN_HEADS].set(w[:, c_b:c_a]).astype(BF16)
    lane_pad = lambda p: jnp.zeros((1, LANES), F32).at[0, :DN_HEADS].set(p.astype(F32))

    proj, ba = _inproj(x, tail, row(ln0_g), row(ln0_b), w_main, w_ba, seq=seq, t_len=t_len)
    qkvn = _conv_qkv(proj, conv_w[0].astype(F32), batch=batch, seq=seq, t_len=t_len)
    o = _delta_rule(qkvn, ba, proj, lane_pad(-jnp.exp(a_log[0].astype(F32))), lane_pad(dt_bias[0]),
                    row(onorm_g[0]), batch=batch, t_len=t_len)
    h1 = _merge(o, proj, x, tail, w_br_delta[0].astype(BF16), w_out[0].astype(BF16), pool_w[0].astype(BF16),
                row(pool_scale[0]), w_br_pool[0].astype(BF16), row(ln0_g), row(ln0_b), row(ln1_g[0]), row(ln1_b[0]),
                batch=batch, seq=seq, t_len=t_len)
    h1 = h1.reshape(batch, t_len, d)

    ia, ib, gate = _peer_query(h1, peer_wq[0].astype(BF16), peer_k1[0].astype(BF16), peer_k2[0].astype(BF16),
                               seq=seq, tq=LANES)

    act = _peer_hidden(h1, peer_u[0].astype(BF16), ia, ib, gate, seq=seq, tb=PEER_TOKEN_TILE)
    out = _peer_output(act, ia, ib, peer_v[0].astype(BF16), h1, row(ln2_g[0]), row(ln2_b[0]),
                       seq=seq, tb=PEER_TOKEN_TILE)
    return out.reshape(batch, seq, d)
```

```python
import functools

import jax
import jax.numpy as jnp
from jax import lax
from jax.experimental import pallas as pl
from jax.experimental.pallas import tpu as pltpu

F32 = jnp.float32
BF16 = jnp.bfloat16
I32 = jnp.int32

D_MODEL = 1024
N_META = 16
CHUNK = 64
PAD = CHUNK - N_META
POOL_GROUPS = 4
POOL_GROUP_DIM = 128
POOL_WIDTH = POOL_GROUPS * POOL_GROUP_DIM
POOL_WINDOWS = (2, 4, 8, 16)
POOL_HALO = 16
DN_HEADS = 8
DN_HEAD_DIM = 128
DN_WIDTH = DN_HEADS * DN_HEAD_DIM
CONV_WIDTH = 4
PEER_HEADS = 8
PEER_HALF = 128
N_KEYS = 128
PEER_TOPK = 16
N_SLOTS = PEER_HEADS * PEER_TOPK
LN_EPS = 1e-5
RMS_EPS = 1e-6
L2_EPS = 1e-6
ALPHA = 2.0 ** 0.25

COL_QKV = 0
COL_Z = 3 * DN_WIDTH
COL_GP = COL_Z + DN_WIDTH
COL_GD = COL_GP + D_MODEL
COL_POOL = COL_GD + D_MODEL
N_PROJ = COL_POOL + POOL_WIDTH
BA_WIDTH = 256

SUBLANES = 8
LANES = 128
V7X_VMEM_BYTES = 64 * 1024 * 1024
VMEM_LIMIT = V7X_VMEM_BYTES - 4 * 1024 * 1024

_CAND = [(a, b) for a in range(PEER_TOPK) for b in range(PEER_TOPK) if (a + 1) * (b + 1) <= PEER_TOPK]


def _params(sem):
    return pltpu.CompilerParams(dimension_semantics=sem, vmem_limit_bytes=VMEM_LIMIT)


def _layer_norm(x, g, b):
    mu = jnp.mean(x, axis=-1, keepdims=True)
    xc = x - mu
    var = jnp.mean(xc * xc, axis=-1, keepdims=True)
    return xc * lax.rsqrt(var + LN_EPS) * g + b


def _zero_pad_rows(y, row, seq):
    return jnp.where(row >= seq, jnp.where(row < seq + PAD, 0.0, y), y)


def _sigmoid(x):
    return 1.0 / (1.0 + jnp.exp(-x))


def _dot(a, b):
    return jnp.dot(a.astype(BF16), b.astype(BF16), preferred_element_type=F32)


def _dot_nt(a, b):
    return lax.dot_general(a.astype(BF16), b.astype(BF16), (((1,), (1,)), ((), ())), preferred_element_type=F32)


def _assemble_rows(x_ref, tail_ref, rows_ref, tile_in_batch, tiles_per_batch):
    body = rows_ref.shape[0] - CHUNK

    @pl.when(tile_in_batch < tiles_per_batch - 1)
    def _():
        rows_ref[...] = x_ref[...]

    @pl.when(tile_in_batch == tiles_per_batch - 1)
    def _():
        rows_ref[:body] = x_ref[:body]
        rows_ref[body:] = tail_ref[...]


def _inproj_kernel(x_ref, tail_ref, g_ref, b_ref, w_ref, wba_ref, proj_ref, ba_ref, rows_ref, xn_ref, *,
                   tiles_per_batch, seq):
    i = pl.program_id(0)
    j = pl.program_id(1)
    tm = x_ref.shape[0]

    @pl.when(j == 0)
    def _():
        _assemble_rows(x_ref, tail_ref, rows_ref, lax.rem(i, tiles_per_batch), tiles_per_batch)
        y = _layer_norm(rows_ref[...], g_ref[...], b_ref[...])
        row = lax.broadcasted_iota(I32, (tm, 1), 0) + lax.rem(i, tiles_per_batch) * tm
        xn = _zero_pad_rows(y, row, seq).astype(BF16)
        xn_ref[...] = xn
        ba_ref[...] = _dot(xn, wba_ref[...])

    proj_ref[...] = _dot(xn_ref[...], w_ref[...]).astype(proj_ref.dtype)


def _inproj(x, tail, ln_g, ln_b, w_main, w_ba, *, seq, t_len):
    rows = x.shape[0] * t_len
    tiles_per_batch = 4
    tm = t_len // tiles_per_batch
    tn = N_PROJ // 4
    kern = functools.partial(_inproj_kernel, tiles_per_batch=tiles_per_batch, seq=seq)
    return pl.pallas_call(
        kern,
        grid=(rows // tm, N_PROJ // tn),
        in_specs=[
            pl.BlockSpec((None, tm, D_MODEL), lambda i, j: (i // tiles_per_batch, lax.rem(i, tiles_per_batch), 0)),
            pl.BlockSpec((CHUNK, D_MODEL), lambda i, j: (0, 0)),
            pl.BlockSpec((1, D_MODEL), lambda i, j: (0, 0)),
            pl.BlockSpec((1, D_MODEL), lambda i, j: (0, 0)),
            pl.BlockSpec((D_MODEL, tn), lambda i, j: (0, j)),
            pl.BlockSpec((D_MODEL, BA_WIDTH), lambda i, j: (0, 0)),
        ],
        out_specs=[
            pl.BlockSpec((tm, tn), lambda i, j: (i, j)),
            pl.BlockSpec((tm, BA_WIDTH), lambda i, j: (i, 0)),
        ],
        out_shape=[
            jax.ShapeDtypeStruct((rows, N_PROJ), BF16),
            jax.ShapeDtypeStruct((rows, BA_WIDTH), F32),
        ],
        scratch_shapes=[pltpu.VMEM((tm, D_MODEL), F32), pltpu.VMEM((tm, D_MODEL), BF16)],
        compiler_params=_params(("arbitrary", "arbitrary")),
        name="inproj",
    )(x, tail, ln_g, ln_b, w_main, w_ba)


def _pool_branch_tile(xp_ref, halo_ref, pw_ref, ps_ref, wbp_ref, first_row, seq):
    tm = xp_ref.shape[0]
    v = jnp.concatenate([halo_ref[...].astype(F32), xp_ref[...].astype(F32)], axis=0)
    row = lax.broadcasted_iota(I32, (tm, POOL_GROUP_DIM), 0) + first_row
    meta_pos = row - (seq + PAD)
    outs = []
    for gi, w in enumerate(POOL_WINDOWS):
        vg = v[:, gi * POOL_GROUP_DIM:(gi + 1) * POOL_GROUP_DIM]
        s = vg
        shift = 1
        while shift < w:
            s = s + pltpu.roll(s, shift, axis=0)
            shift *= 2
        s = s[POOL_HALO:]
        xg = vg[POOL_HALO:]
        count = jnp.where(meta_pos >= 0, jnp.minimum(meta_pos + 1, w), w).astype(F32)
        pooled = s / count - xg
        mixed = _dot(pooled.astype(BF16), pw_ref[gi])
        outs.append(mixed)
    y_pool = jnp.concatenate(outs, axis=1) * ps_ref[...]
    return _dot(y_pool.astype(BF16), wbp_ref[...])


def _conv_kernel(x_ref, w_ref, o_ref, *, seq):
    s = pl.program_id(1)
    x = x_ref[...].astype(F32)
    t_len = x.shape[0]
    w = w_ref[...]
    y = x * w[CONV_WIDTH - 1:CONV_WIDTH, :]
    for lag in range(1, CONV_WIDTH):
        y = y + pltpu.roll(x, lag, axis=0) * w[CONV_WIDTH - 1 - lag:CONV_WIDTH - lag, :]
    y = y * _sigmoid(y)
    ss = jnp.sum(y * y, axis=-1, keepdims=True)
    q_scale = jnp.where(s < DN_HEADS, DN_HEAD_DIM ** -0.5, 1.0).astype(F32)
    fac = jnp.where(s < 2 * DN_HEADS, lax.rsqrt(ss + L2_EPS) * q_scale, 1.0)
    row = lax.broadcasted_iota(I32, (t_len, 1), 0)
    o_ref[...] = _zero_pad_rows(y * fac, row, seq).astype(o_ref.dtype)


def _conv_qkv(proj, conv_w, *, batch, seq, t_len):
    rows = proj.shape[0]
    n_slabs = 3 * DN_HEADS
    return pl.pallas_call(
        functools.partial(_conv_kernel, seq=seq),
        grid=(batch, n_slabs),
        in_specs=[
            pl.BlockSpec((t_len, DN_HEAD_DIM), lambda b, s: (b, COL_QKV // DN_HEAD_DIM + s)),
            pl.BlockSpec((CONV_WIDTH, DN_HEAD_DIM), lambda b, s: (0, s)),
        ],
        out_specs=pl.BlockSpec((t_len, DN_HEAD_DIM), lambda b, s: (b, s)),
        out_shape=jax.ShapeDtypeStruct((rows, 3 * DN_WIDTH), BF16),
        compiler_params=_params(("arbitrary", "arbitrary")),
        name="conv_qkv",
    )(proj, conv_w)


def _inv_unit_lower(a, masks, eye):
    m8, m16, m32, m64 = masks
    n = [x * m8 for x in a]
    n2 = [_dot(x, x) for x in n]
    n4 = [_dot(x, x) for x in n2]
    t = [eye - x for x in n]
    t = [x + _dot(x, y) for x, y in zip(t, n2)]
    t = [x + _dot(x, y) for x, y in zip(t, n4)]
    for m in (m16, m32, m64):
        at = [_dot(x * m, y) for x, y in zip(a, t)]
        t = [x - _dot(x, y) for x, y in zip(t, at)]
    return t


DELTA_STREAMS = 4


def _delta_kernel(q_ref, k_ref, v_ref, ba_ref, z_ref, nega_ref, dtb_ref, og_ref, o_ref, state_ref):
    n = DELTA_STREAMS
    c = pl.program_id(1)

    @pl.when(c == 0)
    def _():
        state_ref[...] = jnp.zeros_like(state_ref)

    r = lax.broadcasted_iota(I32, (CHUNK, CHUNK), 0)
    cc = lax.broadcasted_iota(I32, (CHUNK, CHUNK), 1)
    causal = r >= cc
    strict = r > cc
    eye = jnp.where(r == cc, 1.0, 0.0).astype(F32)
    tril = jnp.where(causal, 1.0, 0.0).astype(F32)

    def blk(x, s):
        return lax.shift_right_logical(x, s)

    lower = jnp.where(strict, 1.0, 0.0).astype(F32)
    m8 = jnp.where(blk(r, 3) == blk(cc, 3), lower, 0.0)
    m16 = jnp.where((blk(r, 4) == blk(cc, 4)) & (blk(r, 3) != blk(cc, 3)), lower, 0.0)
    m32 = jnp.where((blk(r, 5) == blk(cc, 5)) & (blk(r, 4) != blk(cc, 4)), lower, 0.0)
    m64 = jnp.where(blk(r, 5) != blk(cc, 5), lower, 0.0)
    masks = (m8, m16, m32, m64)

    beta_all, gcum, gcum_t = [], [], []
    for s in range(n):
        ba = ba_ref[s]
        beta_all.append(_sigmoid(ba[:, :LANES]))
        x = ba[:, LANES:] + dtb_ref[...]
        softplus = jnp.maximum(x, 0.0) + jnp.log(1.0 + jnp.exp(-jnp.abs(x)))
        g_all = nega_ref[...] * softplus
        gcum.append(lax.dot_general(tril, g_all, (((1,), (0,)), ((), ())), precision=lax.Precision.HIGHEST,
                                    preferred_element_type=F32))
        gcum_t.append(gcum[s].T)

    chains = [(s, h) for s in range(n) for h in range(DN_HEADS)]
    ids = range(len(chains))
    sl = [slice(h * DN_HEAD_DIM, (h + 1) * DN_HEAD_DIM) for _, h in chains]
    q = [q_ref[s, :, sl[i]].astype(F32) for i, (s, h) in enumerate(chains)]
    k = [k_ref[s, :, sl[i]].astype(F32) for i, (s, h) in enumerate(chains)]
    v = [v_ref[s, :, sl[i]].astype(F32) for i, (s, h) in enumerate(chains)]
    beta = [beta_all[s][:, h:h + 1] for s, h in chains]
    gc = [gcum[s][:, h:h + 1] for s, h in chains]
    gr = [gcum_t[s][h:h + 1, :] for s, h in chains]
    g_last = [gcum[s][CHUNK - 1:CHUNK, h:h + 1] for s, h in chains]
    decay = [jnp.where(causal, jnp.exp(jnp.where(causal, gc[i] - gr[i], 0.0)), 0.0) for i in ids]
    eg = [jnp.exp(gc[i]) for i in ids]
    kb = [k[i] * beta[i] for i in ids]
    kq = [_dot_nt(jnp.concatenate([kb[i], q[i]], axis=0), k[i]) for i in ids]
    a_mat = [jnp.where(strict, kq[i][:CHUNK] * decay[i], 0.0) for i in ids]
    t_inv = _inv_unit_lower(a_mat, masks, eye)
    rhs = [jnp.concatenate([v[i] * beta[i], kb[i] * eg[i]], axis=1) for i in ids]
    uw = [_dot(t_inv[i], rhs[i]) for i in ids]
    s_prev = [state_ref[i] for i in ids]
    wq = [jnp.concatenate([uw[i][:, DN_HEAD_DIM:], q[i] * eg[i]], axis=0) for i in ids]
    ws = [_dot(wq[i], s_prev[i]) for i in ids]
    v_new = [uw[i][:, :DN_HEAD_DIM] - ws[i][:CHUNK] for i in ids]
    attn = [kq[i][CHUNK:] * decay[i] for i in ids]
    intra = [_dot(attn[i], v_new[i]) for i in ids]
    k_dec_t = [(k[i] * jnp.exp(g_last[i] - gc[i])).T for i in ids]
    kv = [_dot(k_dec_t[i], v_new[i]) for i in ids]
    for i, (s, h) in enumerate(chains):
        state_ref[i] = s_prev[i] * jnp.exp(g_last[i]) + kv[i]
        out = ws[i][CHUNK:] + intra[i]
        out = out * lax.rsqrt(jnp.mean(out * out, axis=-1, keepdims=True) + RMS_EPS) * og_ref[...]
        zh = z_ref[s, :, sl[i]].astype(F32)
        o_ref[s, :, sl[i]] = (out * (zh * _sigmoid(zh))).astype(o_ref.dtype)


def _delta_rule(qkvn, ba, proj, neg_a, dt_bias, onorm_g, *, batch, t_len):
    rows = qkvn.shape[0]
    n_chunks = t_len // CHUNK

    n = DELTA_STREAMS
    assert batch % n == 0
    groups = batch // n
    by_stream = lambda a: a.reshape(groups, n, t_len, a.shape[-1])

    def chunk_block(col):
        return lambda g, c: (g, 0, lax.rem(c + n_chunks - 1, n_chunks), col)

    const = lambda g, c: (0, 0)
    out = pl.pallas_call(
        _delta_kernel,
        grid=(groups, n_chunks),
        in_specs=[
            pl.BlockSpec((None, n, CHUNK, DN_WIDTH), chunk_block(0)),
            pl.BlockSpec((None, n, CHUNK, DN_WIDTH), chunk_block(1)),
            pl.BlockSpec((None, n, CHUNK, DN_WIDTH), chunk_block(2)),
            pl.BlockSpec((None, n, CHUNK, BA_WIDTH), chunk_block(0)),
            pl.BlockSpec((None, n, CHUNK, DN_WIDTH), chunk_block(COL_Z // DN_WIDTH)),
            pl.BlockSpec((1, LANES), const),
            pl.BlockSpec((1, LANES), const),
            pl.BlockSpec((1, DN_HEAD_DIM), const),
        ],
        out_specs=pl.BlockSpec((None, n, CHUNK, DN_WIDTH), chunk_block(0)),
        out_shape=jax.ShapeDtypeStruct((groups, n, t_len, DN_WIDTH), BF16),
        scratch_shapes=[pltpu.VMEM((n * DN_HEADS, DN_HEAD_DIM, DN_HEAD_DIM), F32)],
        compiler_params=_params(("arbitrary", "arbitrary")),
        name="delta_rule",
    )(by_stream(qkvn), by_stream(qkvn), by_stream(qkvn), by_stream(ba), by_stream(proj), neg_a, dt_bias, onorm_g)
    return out.reshape(rows, DN_WIDTH)


def _merge_kernel(o_ref, xp_ref, halo_ref, gp_ref, gd_ref, x_ref, tail_ref, wd_ref, wo_ref, pw_ref, ps_ref, wbp_ref,
                  g0_ref, b0_ref, g1_ref, b1_ref, h1_ref, rows_ref, *, tiles_per_batch, seq):
    tile_in_batch = lax.rem(pl.program_id(0), tiles_per_batch)
    first_row = tile_in_batch * o_ref.shape[0]
    y_a = _pool_branch_tile(xp_ref, halo_ref, pw_ref, ps_ref, wbp_ref, first_row, seq)
    y_b = _dot(o_ref[...], wd_ref[...])
    merged = _sigmoid(gp_ref[...].astype(F32)) * y_a + _sigmoid(gd_ref[...].astype(F32)) * y_b
    mix = _dot(merged.astype(BF16), wo_ref[...])
    _assemble_rows(x_ref, tail_ref, rows_ref, tile_in_batch, tiles_per_batch)
    h = _layer_norm(rows_ref[...], g0_ref[...], b0_ref[...])
    h1_ref[...] = _layer_norm(ALPHA * h + mix, g1_ref[...], b1_ref[...])


def _merge(o, proj, x, tail, w_br_delta, w_out, pool_w, pool_scale, w_br_pool, g0, b0, g1, b1, *, batch, seq, t_len):
    bf16_rows = 2 * SUBLANES
    tiles_per_batch = next(n for n in (10, 8, 4, 2, 1) if t_len % (n * bf16_rows) == 0)
    tm = t_len // tiles_per_batch
    halo_blocks_per_batch = t_len // POOL_HALO
    halo_blocks_per_tile = tm // POOL_HALO
    pool_col = COL_POOL // POOL_WIDTH
    row = lambda i: (i, 0)
    const = lambda i: (0, 0)

    def halo_map(i):
        b, t = i // tiles_per_batch, lax.rem(i, tiles_per_batch)
        prev = lax.rem(t * halo_blocks_per_tile + halo_blocks_per_batch - 1, halo_blocks_per_batch)
        return (b * halo_blocks_per_batch + prev, pool_col)

    return pl.pallas_call(
        functools.partial(_merge_kernel, tiles_per_batch=tiles_per_batch, seq=seq),
        grid=(batch * tiles_per_batch,),
        in_specs=[
            pl.BlockSpec((tm, DN_WIDTH), row),
            pl.BlockSpec((tm, POOL_WIDTH), lambda i: (i, pool_col)),
            pl.BlockSpec((POOL_HALO, POOL_WIDTH), halo_map),
            pl.BlockSpec((tm, D_MODEL), lambda i: (i, COL_GP // D_MODEL)),
            pl.BlockSpec((tm, D_MODEL), lambda i: (i, COL_GD // D_MODEL)),
            pl.BlockSpec((None, tm, D_MODEL), lambda i: (i // tiles_per_batch, lax.rem(i, tiles_per_batch), 0)),
            pl.BlockSpec((CHUNK, D_MODEL), const),
            pl.BlockSpec((DN_WIDTH, D_MODEL), const),
            pl.BlockSpec((D_MODEL, D_MODEL), const),
            pl.BlockSpec((POOL_GROUPS, POOL_GROUP_DIM, POOL_GROUP_DIM), lambda i: (0, 0, 0)),
            pl.BlockSpec((1, POOL_WIDTH), const),
            pl.BlockSpec((POOL_WIDTH, D_MODEL), const),
            pl.BlockSpec((1, D_MODEL), const),
            pl.BlockSpec((1, D_MODEL), const),
            pl.BlockSpec((1, D_MODEL), const),
            pl.BlockSpec((1, D_MODEL), const),
        ],
        out_specs=pl.BlockSpec((tm, D_MODEL), row),
        out_shape=jax.ShapeDtypeStruct((batch * t_len, D_MODEL), F32),
        scratch_shapes=[pltpu.VMEM((tm, D_MODEL), F32)],
        compiler_params=_params(("arbitrary",)),
        name="merge",
    )(o, proj, proj, proj, proj, x, tail, w_br_delta, w_out, pool_w, pool_scale, w_br_pool, g0, b0, g1, b1)


def _tile(ref, k):
    return ref[k * SUBLANES:(k + 1) * SUBLANES, :]


def _argmax_tournament(vals, ids):
    nodes = list(zip(vals, ids))
    while len(nodes) > 1:
        nxt = []
        for p in range(0, len(nodes) - 1, 2):
            (va, ia), (vb, ib) = nodes[p], nodes[p + 1]
            first = va >= vb
            nxt.append((jnp.maximum(va, vb), jnp.where(first, ia, ib)))
        if len(nodes) % 2:
            nxt.append(nodes[-1])
        nodes = nxt
    return nodes[0]


def _query_kernel(x_ref, wq_ref, k1_ref, k2_ref, ia_ref, ib_ref, gate_ref, s1_ref, s2_ref, m1_ref, i1_ref, m2_ref,
                  i2_ref, c_ref, e_ref, sc_ref, ex_ref):
    tq = x_ref.shape[0]
    xb = x_ref[...].astype(BF16)
    q = _dot(xb, wq_ref[...])
    for h in range(PEER_HEADS):
        q1 = q[:, (2 * h) * PEER_HALF:(2 * h + 1) * PEER_HALF].astype(BF16)
        q2 = q[:, (2 * h + 1) * PEER_HALF:(2 * h + 2) * PEER_HALF].astype(BF16)
        s1_ref[pl.ds(h, N_KEYS, stride=SUBLANES), :] = _dot_nt(k1_ref[h], q1)
        s2_ref[pl.ds(h, N_KEYS, stride=SUBLANES), :] = _dot_nt(k2_ref[h], q2)

    neg_inf = jnp.float32(-jnp.inf)

    def sub_round(r, carry):
        for s_ref, m_ref, i_ref in ((s1_ref, m1_ref, i1_ref), (s2_ref, m2_ref, i2_ref)):
            vals = [_tile(s_ref, k) for k in range(N_KEYS)]
            best, arg = _argmax_tournament(vals, list(range(N_KEYS)))
            m_ref[r] = best
            i_ref[r] = arg
            for k in range(N_KEYS):
                s_ref[k * SUBLANES:(k + 1) * SUBLANES, :] = jnp.where(arg == k, neg_inf, vals[k])
        return carry

    lax.fori_loop(0, PEER_TOPK, sub_round, 0)

    for j, (a, b) in enumerate(_CAND):
        c_ref[j * SUBLANES:(j + 1) * SUBLANES, :] = m1_ref[a] + m2_ref[b]
        e_ref[j * SUBLANES:(j + 1) * SUBLANES, :] = i1_ref[a] * N_KEYS + i2_ref[b]

    def pair_round(r, carry):
        vals = [_tile(c_ref, j) for j in range(len(_CAND))]
        ids = [_tile(e_ref, j) for j in range(len(_CAND))]
        best, arg = _argmax_tournament(vals, ids)
        sc_ref[r] = best
        ex_ref[r] = arg
        for j in range(len(_CAND)):
            c_ref[j * SUBLANES:(j + 1) * SUBLANES, :] = jnp.where(ids[j] == arg, neg_inf, vals[j])
        return carry

    lax.fori_loop(0, PEER_TOPK, pair_round, 0)

    sc = sc_ref[...]
    ex = jnp.exp(sc - sc[0:1])
    gate = ex / jnp.sum(ex, axis=0, keepdims=True)
    gate_ref[...] = gate.reshape(N_SLOTS, tq).T
    expert = ex_ref[...].reshape(N_SLOTS, tq).T
    ia_ref[...] = lax.shift_right_logical(expert, N_KEYS.bit_length() - 1)
    ib_ref[...] = lax.bitwise_and(expert, N_KEYS - 1)


def _token_tile_map(seq, tile):
    per_batch = seq // tile
    return lambda i, *_: (i // per_batch, lax.rem(i, per_batch), 0)


def _peer_query(h1, wq, k1, k2, *, seq, tq):
    n_tok = h1.shape[0] * seq
    slot_shape = (PEER_TOPK, PEER_HEADS, tq)
    return pl.pallas_call(
        _query_kernel,
        grid=(n_tok // tq,),
        in_specs=[
            pl.BlockSpec((None, tq, D_MODEL), _token_tile_map(seq, tq)),
            pl.BlockSpec((D_MODEL, 2 * PEER_HALF * PEER_HEADS), lambda i: (0, 0)),
            pl.BlockSpec((PEER_HEADS, N_KEYS, PEER_HALF), lambda i: (0, 0, 0)),
            pl.BlockSpec((PEER_HEADS, N_KEYS, PEER_HALF), lambda i: (0, 0, 0)),
        ],
        out_specs=[pl.BlockSpec((tq, N_SLOTS), lambda i: (i, 0)) for _ in range(3)],
        out_shape=[
            jax.ShapeDtypeStruct((n_tok, N_SLOTS), I32),
            jax.ShapeDtypeStruct((n_tok, N_SLOTS), I32),
            jax.ShapeDtypeStruct((n_tok, N_SLOTS), F32),
        ],
        scratch_shapes=[
            pltpu.VMEM((N_KEYS * SUBLANES, tq), F32),
            pltpu.VMEM((N_KEYS * SUBLANES, tq), F32),
            pltpu.VMEM(slot_shape, F32),
            pltpu.VMEM(slot_shape, I32),
            pltpu.VMEM(slot_shape, F32),
            pltpu.VMEM(slot_shape, I32),
            pltpu.VMEM((len(_CAND) * SUBLANES, tq), F32),
            pltpu.VMEM((len(_CAND) * SUBLANES, tq), I32),
            pltpu.VMEM(slot_shape, F32),
            pltpu.VMEM(slot_shape, I32),
        ],
        compiler_params=_params(("arbitrary",)),
        name="peer_query",
    )(h1, wq, k1, k2)


IA_PER_STEP = 16
EXPERTS_PER_STEP = IA_PER_STEP * N_KEYS
PEER_STEPS = (N_KEYS * N_KEYS) // EXPERTS_PER_STEP
PEER_PIECES = 2
TABLE_STREAMS = 4


def _table_specs():
    rows = EXPERTS_PER_STEP // TABLE_STREAMS
    return [pl.BlockSpec((rows, D_MODEL), lambda i, j, p=p: (j * TABLE_STREAMS + p, 0)) for p in range(TABLE_STREAMS)]


def _one_hot_rows(idx_row):
    iota = lax.broadcasted_iota(I32, (N_KEYS, N_SLOTS), 0)
    return jnp.where(iota == idx_row, 1.0, 0.0).astype(BF16)


def _block_diag(a, b):
    zero = jnp.zeros_like(a)
    return jnp.concatenate([jnp.concatenate([a, zero], axis=1), jnp.concatenate([zero, b], axis=1)], axis=0)


def _hidden_kernel(x_ref, *refs, tb, hp):
    u_refs = refs[:TABLE_STREAMS]
    ia_ref, ib_ref, gate_ref, act_ref, xb_ref, h0_ref, h1_ref, hs_ref = refs[TABLE_STREAMS:]
    i = pl.program_id(0)
    j = pl.program_id(1)
    tokens = tb // PEER_STEPS // PEER_PIECES
    parts = TABLE_STREAMS // PEER_PIECES
    ia_pairs = IA_PER_STEP // PEER_PIECES // 2

    @pl.when(jnp.logical_and(i == 0, j == 0))
    def _():
        h1_ref[...] = jnp.zeros_like(h1_ref)

    @pl.when(j == 0)
    def _():
        xb_ref[...] = x_ref[...].astype(BF16)

    def step(fill_ref, drain_ref):
        iota = lax.broadcasted_iota(I32, (N_KEYS, N_SLOTS), 0)
        for q in range(PEER_PIECES):
            base = pl.multiple_of((j * PEER_PIECES + q) * tokens, tokens)
            ia_rows = ia_ref[pl.ds(base, tokens), :]
            ib_rows = ib_ref[pl.ds(base, tokens), :]
            lhs, rhs = [], []
            for p in range(tokens // 2):
                tiles = []
                for t in (2 * p, 2 * p + 1):
                    start = pl.multiple_of((base + t) * hp, SUBLANES)
                    tiles.append(pltpu.bitcast(drain_ref[pl.ds(start, N_KEYS // 2), :], BF16))
                lhs.append(jnp.concatenate(tiles, axis=1))
                rhs.append(_block_diag(_one_hot_rows(ib_rows[2 * p:2 * p + 1, :]),
                                       _one_hot_rows(ib_rows[2 * p + 1:2 * p + 2, :])))

            experts = jnp.concatenate([u_refs[q * parts + p][...] for p in range(parts)], axis=1)
            hc = _dot(xb_ref[...], experts)
            for m in range(ia_pairs):
                even = hc[:, (2 * m) * N_KEYS:(2 * m + 1) * N_KEYS]
                odd = hc[:, (2 * m + 1) * N_KEYS:(2 * m + 2) * N_KEYS]
                packed = pltpu.pack_elementwise([even, odd], packed_dtype=BF16)
                pair = (j * PEER_PIECES + q) * ia_pairs + m
                fill_ref[pl.ds(pair, tb, stride=hp), :] = lax.bitcast_convert_type(packed, I32)

            rows = []
            for p in range(tokens // 2):
                both = _dot(lhs[p], rhs[p])
                for k in range(2):
                    mine = both[:, k * N_SLOTS:(k + 1) * N_SLOTS]
                    picked = jnp.where(iota == ia_rows[2 * p + k:2 * p + k + 1, :], mine, 0.0)
                    rows.append(jnp.sum(picked, axis=0, keepdims=True))
            hs_ref[pl.ds(base, tokens), :] = jnp.concatenate(rows, axis=0)

    parity = lax.rem(i, 2)

    @pl.when(parity == 0)
    def _():
        step(h0_ref, h1_ref)

    @pl.when(parity == 1)
    def _():
        step(h1_ref, h0_ref)

    @pl.when(j == PEER_STEPS - 1)
    def _():
        hid = hs_ref[...]
        gelu = 0.5 * hid * (1.0 + lax.erf(hid * (2.0 ** -0.5)))
        act_ref[...] = gelu * gate_ref[...]


def _peer_hidden(h1, u_tab, ia, ib, gate, *, seq, tb):
    n_tiles = h1.shape[0] * seq // tb
    hp = N_KEYS // 2 + SUBLANES
    x_map = _token_tile_map(seq, tb)
    ahead = lambda i, j: x_map(jnp.minimum(i, n_tiles - 1))
    behind = lambda i, j: (jnp.maximum(i - 1, 0), 0)
    return pl.pallas_call(
        functools.partial(_hidden_kernel, tb=tb, hp=hp),
        grid=(n_tiles + 1, PEER_STEPS),
        in_specs=[
            pl.BlockSpec((None, tb, D_MODEL), ahead),
            *[pl.BlockSpec((D_MODEL, EXPERTS_PER_STEP // TABLE_STREAMS), lambda i, j, p=p: (0, j * TABLE_STREAMS + p))
              for p in range(TABLE_STREAMS)],
            pl.BlockSpec((tb, N_SLOTS), behind),
            pl.BlockSpec((tb, N_SLOTS), behind),
            pl.BlockSpec((tb, N_SLOTS), behind),
        ],
        out_specs=pl.BlockSpec((tb, N_SLOTS), behind),
        out_shape=jax.ShapeDtypeStruct((n_tiles * tb, N_SLOTS), F32),
        scratch_shapes=[
            pltpu.VMEM((tb, D_MODEL), BF16),
            pltpu.VMEM((tb * hp, N_KEYS), I32),
            pltpu.VMEM((tb * hp, N_KEYS), I32),
            pltpu.VMEM((tb, N_SLOTS), F32),
        ],
        compiler_params=_params(("arbitrary", "arbitrary")),
        name="peer_hidden",
    )(h1, *([u_tab] * TABLE_STREAMS), ia, ib, gate)


def _output_kernel(act_ref, ia_ref, ib_ref, *refs, tb, hp):
    v_refs = refs[:TABLE_STREAMS]
    h1_ref, g_ref, b_ref, out_ref, a0_ref, a1_ref, acc_ref = refs[TABLE_STREAMS:]
    i = pl.program_id(0)
    j = pl.program_id(1)
    tokens = tb // PEER_STEPS // PEER_PIECES
    parts = TABLE_STREAMS // PEER_PIECES
    ia_per_piece = IA_PER_STEP // PEER_PIECES

    @pl.when(jnp.logical_and(i == 0, j == 0))
    def _():
        a1_ref[...] = jnp.zeros_like(a1_ref)

    @pl.when(j == 0)
    def _():
        acc_ref[...] = jnp.zeros_like(acc_ref)

    def step(fill_ref, drain_ref):
        iota = lax.broadcasted_iota(I32, (N_KEYS, N_SLOTS), 0)
        acc = acc_ref[...]
        for q in range(PEER_PIECES):
            base = pl.multiple_of((j * PEER_PIECES + q) * tokens, tokens)
            ia_rows = ia_ref[pl.ds(base, tokens), :]
            ib_rows = ib_ref[pl.ds(base, tokens), :]
            act_rows = act_ref[pl.ds(base, tokens), :]
            lhs, sels = [], []
            for p in range(tokens // 2):
                spread = [jnp.where(iota == ia_rows[t:t + 1, :], act_rows[t:t + 1, :], 0.0).astype(BF16)
                          for t in (2 * p, 2 * p + 1)]
                lhs.append(jnp.concatenate(spread, axis=1))
                sels += [jnp.where(iota == ib_rows[t:t + 1, :], 1.0, 0.0).T.astype(BF16)
                         for t in (2 * p, 2 * p + 1)]

            tiles = []
            for t in range(ia_per_piece):
                ia = j * IA_PER_STEP + q * ia_per_piece + t
                start = pl.multiple_of(ia * hp, SUBLANES)
                tiles.append(pltpu.bitcast(drain_ref[pl.ds(start, tb // 2), :], BF16))
            table_rows = jnp.concatenate([v_refs[q * parts + p][...] for p in range(parts)], axis=0)
            acc = acc + _dot(jnp.concatenate(tiles, axis=1), table_rows)

            pair_base = (j * PEER_PIECES + q) * (tokens // 2)
            for p in range(tokens // 2):
                both = _dot(lhs[p], _block_diag(sels[2 * p], sels[2 * p + 1]))
                packed = pltpu.pack_elementwise([both[:, :N_KEYS], both[:, N_KEYS:]], packed_dtype=BF16)
                fill_ref[pl.ds(pair_base + p, N_KEYS, stride=hp), :] = lax.bitcast_convert_type(packed, I32)
        acc_ref[...] = acc

    parity = lax.rem(i, 2)

    @pl.when(parity == 0)
    def _():
        step(a0_ref, a1_ref)

    @pl.when(parity == 1)
    def _():
        step(a1_ref, a0_ref)

    @pl.when(j == PEER_STEPS - 1)
    def _():
        out_ref[...] = _layer_norm(ALPHA * h1_ref[...] + acc_ref[...], g_ref[...], b_ref[...])


def _peer_output(act, ia, ib, v_tab, h1, g2, b2, *, seq, tb):
    n_tiles = h1.shape[0] * seq // tb
    hp = tb // 2 + SUBLANES
    x_map = _token_tile_map(seq, tb)
    ahead = lambda i, j: (jnp.minimum(i, n_tiles - 1), 0)
    behind = lambda i, j: (jnp.maximum(i - 1, 0), 0)
    const = lambda i, j: (0, 0)
    return pl.pallas_call(
        functools.partial(_output_kernel, tb=tb, hp=hp),
        grid=(n_tiles + 1, PEER_STEPS),
        in_specs=[
            pl.BlockSpec((tb, N_SLOTS), ahead),
            pl.BlockSpec((tb, N_SLOTS), ahead),
            pl.BlockSpec((tb, N_SLOTS), ahead),
            *_table_specs(),
            pl.BlockSpec((None, tb, D_MODEL), lambda i, j: x_map(jnp.maximum(i - 1, 0))),
            pl.BlockSpec((1, D_MODEL), const),
            pl.BlockSpec((1, D_MODEL), const),
        ],
        out_specs=pl.BlockSpec((tb, D_MODEL), behind),
        out_shape=jax.ShapeDtypeStruct((n_tiles * tb, D_MODEL), F32),
        scratch_shapes=[
            pltpu.VMEM((N_KEYS * hp, N_KEYS), I32),
            pltpu.VMEM((N_KEYS * hp, N_KEYS), I32),
            pltpu.VMEM((tb, D_MODEL), F32),
        ],
        compiler_params=_params(("arbitrary", "arbitrary")),
        name="peer_output",
    )(act, ia, ib, *([v_tab] * TABLE_STREAMS), h1, g2, b2)


PEER_TOKEN_TILE = 512


def kernel(x, meta, ln0_g, ln0_b, w_in, pool_w, pool_scale, w_br_pool, conv_w, a_log, dt_bias, onorm_g, w_br_delta, w_out, ln1_g, ln1_b, peer_wq, peer_k1, peer_k2, peer_u, peer_v, ln2_g, ln2_b):
    batch, seq, d = x.shape
    assert d == D_MODEL and w_in.shape[0] == 1 and seq % PEER_TOKEN_TILE == 0 and batch % DELTA_STREAMS == 0
    t_len = seq + CHUNK
    row = lambda p: p.reshape(1, -1).astype(F32)

    tail = jnp.concatenate([jnp.zeros((PAD, d), x.dtype), meta.astype(x.dtype)], axis=0)

    w = w_in[0]
    c_pool, c_z = POOL_WIDTH, POOL_WIDTH + 4 * DN_WIDTH
    c_b, c_a = c_z + DN_HEADS, c_z + 2 * DN_HEADS
    w_main = jnp.concatenate([w[:, c_pool:c_z], w[:, c_a:], w[:, :c_pool]], axis=1).astype(BF16)
    w_ba = jnp.zeros((d, BA_WIDTH), F32)
    w_ba = w_ba.at[:, :DN_HEADS].set(w[:, c_z:c_b]).at[:, LANES:LANES + DN_HEADS].set(w[:, c_b:c_a]).astype(BF16)
    lane_pad = lambda p: jnp.zeros((1, LANES), F32).at[0, :DN_HEADS].set(p.astype(F32))

    proj, ba = _inproj(x, tail, row(ln0_g), row(ln0_b), w_main, w_ba, seq=seq, t_len=t_len)
    qkvn = _conv_qkv(proj, conv_w[0].astype(F32), batch=batch, seq=seq, t_len=t_len)
    o = _delta_rule(qkvn, ba, proj, lane_pad(-jnp.exp(a_log[0].astype(F32))), lane_pad(dt_bias[0]),
                    row(onorm_g[0]), batch=batch, t_len=t_len)
    h1 = _merge(o, proj, x, tail, w_br_delta[0].astype(BF16), w_out[0].astype(BF16), pool_w[0].astype(BF16),
                row(pool_scale[0]), w_br_pool[0].astype(BF16), row(ln0_g), row(ln0_b), row(ln1_g[0]), row(ln1_b[0]),
                batch=batch, seq=seq, t_len=t_len)
    h1 = h1.reshape(batch, t_len, d)

    ia, ib, gate = _peer_query(h1, peer_wq[0].astype(BF16), peer_k1[0].astype(BF16), peer_k2[0].astype(BF16),
                               seq=seq, tq=LANES)

    act = _peer_hidden(h1, peer_u[0].astype(BF16).T, ia, ib, gate, seq=seq, tb=PEER_TOKEN_TILE)
    out = _peer_output(act, ia, ib, peer_v[0].astype(BF16), h1, row(ln2_g[0]), row(ln2_b[0]),
                       seq=seq, tb=PEER_TOKEN_TILE)
    return out.reshape(batch, seq, d)
```
